```python
import jax, jax.numpy as jnp
from jax import lax
import numpy as np

D_MODEL = 1024
BATCH = 8
SEQ = 4096
DEPTH = 1

HEAD_DIM = 64
DIL_GROUPS = ((128, 1), (512, 4), (2048, 16))
N_DIL_GROUPS = 3
DIL_HEADS = 4
DIL_WIDTH = N_DIL_GROUPS * DIL_HEADS * HEAD_DIM
DIL_OUT = DIL_HEADS * HEAD_DIM
FOX_HEADS = 8
FOX_WIDTH = FOX_HEADS * HEAD_DIM
MEM_HEADS = 4
MEM_HEAD_DIM = 128
MEM_WIDTH = MEM_HEADS * MEM_HEAD_DIM
MEM_LEN = 256
ROT_DIM = HEAD_DIM // 4
ROPE_THETA = 500000.0
D_FF = 4 * D_MODEL
N_BRANCH = 3
BLOCK_Q = 128
IN_SPLITS = (DIL_WIDTH, DIL_WIDTH, DIL_WIDTH, FOX_WIDTH, FOX_WIDTH, FOX_WIDTH, FOX_HEADS, MEM_WIDTH)
IN_WIDTH = 3 * DIL_WIDTH + 3 * FOX_WIDTH + FOX_HEADS + MEM_WIDTH
EPS = 1e-6

kernel_name = "hybrid_gated_dilated_fox_memory_layer"


def rmsnorm(x, g):
    xf = x.astype(jnp.float32)
    y = xf * lax.rsqrt(jnp.mean(xf * xf, axis=-1, keepdims=True) + EPS)
    return (y * g.astype(jnp.float32)).astype(x.dtype)


def partial_rope(x, cos, sin):
    half = ROT_DIM // 2
    c = cos[None, :, None, None, :].astype(x.dtype)
    s = sin[None, :, None, None, :].astype(x.dtype)
    x1 = x[..., :half]
    x2 = x[..., half:ROT_DIM]
    return jnp.concatenate([x1 * c - x2 * s, x2 * c + x1 * s, x[..., ROT_DIM:]], axis=-1)


def banded_causal_attention(q, k, v, n_back):
    G, N, H, E = q.shape
    C = n_back
    nb = -(-N // C)
    pad = nb * C - N
    padw = ((0, 0), (0, pad), (0, 0), (0, 0))
    qb = jnp.pad(q.astype(jnp.float32), padw).reshape(G, nb, C, H, E)
    kb = jnp.pad(k.astype(jnp.float32), padw).reshape(G, nb, C, H, E)
    vb = jnp.pad(v.astype(jnp.float32), padw).reshape(G, nb, C, H, E)
    k_prev = jnp.concatenate([jnp.zeros_like(kb[:, :1]), kb[:, :-1]], axis=1)
    v_prev = jnp.concatenate([jnp.zeros_like(vb[:, :1]), vb[:, :-1]], axis=1)
    kk = jnp.concatenate([k_prev, kb], axis=2)
    vv = jnp.concatenate([v_prev, vb], axis=2)
    s = jnp.einsum('gbqhe,gbkhe->gbhqk', qb, kk) * (HEAD_DIM ** -0.5)
    qi = jnp.arange(C)[:, None]
    kj = jnp.arange(2 * C)[None, :]
    dist = qi + C - kj
    in_band = (dist >= 0) & (dist <= n_back)
    blk_ok = (jnp.arange(nb)[:, None] > 0) | (jnp.arange(2 * C)[None, :] >= C)
    mask = in_band[None, :, :] & blk_ok[:, None, :]
    s = jnp.where(mask[None, :, None], s, -jnp.inf)
    m = jnp.max(s, axis=-1, keepdims=True)
    p = jnp.exp(s - m)
    den = jnp.sum(p, axis=-1, keepdims=True)
    out = jnp.einsum('gbhqk,gbkhe->gbqhe', p, vv) / den.transpose(0, 1, 3, 2, 4)
    lse = (m + jnp.log(den))[..., 0].transpose(0, 1, 3, 2)
    out = out.reshape(G, nb * C, H, E)[:, :N]
    lse = lse.reshape(G, nb * C, H)[:, :N]
    return out, lse


def dilated_group(q, k, v, window, dilation):
    B, T, H, E = q.shape
    n = T // dilation

    def to_sub(a):
        return a.reshape(B, n, dilation, H, E).transpose(0, 2, 1, 3, 4).reshape(B * dilation, n, H, E)

    out, lse = banded_causal_attention(to_sub(q), to_sub(k), to_sub(v), window // dilation)
    out = out.reshape(B, dilation, n, H, E).transpose(0, 2, 1, 3, 4).reshape(B, T, H, E)
    lse = lse.reshape(B, dilation, n, H).transpose(0, 2, 1, 3).reshape(B, T, H)
    return out, lse


def fox_attention(q, k, v, logf):
    B, T, H, E = q.shape
    nq = T // BLOCK_Q
    kf = k.astype(jnp.float32)
    vf = v.astype(jnp.float32)
    c = jnp.cumsum(logf.astype(jnp.float32), axis=1)
    q_blocks = q.astype(jnp.float32).reshape(B, nq, BLOCK_Q, H, E).transpose(1, 0, 2, 3, 4)
    c_blocks = c.reshape(B, nq, BLOCK_Q, H).transpose(1, 0, 2, 3)
    starts = jnp.arange(nq) * BLOCK_Q
    c_keys = c.transpose(0, 2, 1)
    key_pos = jnp.arange(T)

    def one_block(args):
        qi, ci, start = args
        s = jnp.einsum('bqhe,bkhe->bhqk', qi, kf) * (HEAD_DIM ** -0.5)
        s = s + ci.transpose(0, 2, 1)[..., None] - c_keys[:, :, None, :]
        qpos = start + jnp.arange(BLOCK_Q)
        s = jnp.where(key_pos[None, :] <= qpos[:, None], s, -jnp.inf)
        p = jax.nn.softmax(s, axis=-1)
        return jnp.einsum('bhqk,bkhe->bqhe', p, vf)

    out = lax.map(one_block, (q_blocks, c_blocks, starts))
    return out.transpose(1, 0, 2, 3, 4).reshape(B, T, H * E)


def setup_inputs(seed: int = 0) -> dict:
    key = jax.random.key(seed)
    ks = jax.random.split(key, 24)
    L, D = DEPTH, D_MODEL

    def nrm(k, shape, fan_in):
        return jax.random.normal(k, shape, jnp.float32) * (fan_in ** -0.5)

    def gain(k, shape):
        return 1.0 + 0.05 * jax.random.normal(k, shape, jnp.float32)

    return {
        "x": jax.random.normal(ks[0], (BATCH, SEQ, D), jnp.float32),
        "mem": jax.random.normal(ks[1], (BATCH, MEM_LEN, D), jnp.float32),
        "g_mix": gain(ks[2], (L, D)),
        "w_in": nrm(ks[3], (L, D, IN_WIDTH), D),
        "b_f": 3.0 + 0.1 * jax.random.normal(ks[4], (L, FOX_HEADS), jnp.float32),
        "g_qA": gain(ks[5], (L, HEAD_DIM)),
        "g_kA": gain(ks[6], (L, HEAD_DIM)),
        "g_qB": gain(ks[7], (L, HEAD_DIM)),
        "g_kB": gain(ks[8], (L, HEAD_DIM)),
        "g_mem": gain(ks[9], (L, D)),
        "w_mem_kv": nrm(ks[10], (L, D, 2 * MEM_WIDTH), D),
        "g_qM": gain(ks[11], (L, MEM_HEAD_DIM)),
        "g_kM": gain(ks[12], (L, MEM_HEAD_DIM)),
        "w_gate": nrm(ks[13], (L, D, N_BRANCH * D), D),
        "b_gate": 0.1 * jax.random.normal(ks[14], (L, N_BRANCH * D), jnp.float32),
        "w_br_a": nrm(ks[15], (L, DIL_OUT, D), DIL_OUT),
        "w_br_b": nrm(ks[16], (L, FOX_WIDTH, D), FOX_WIDTH),
        "w_br_m": nrm(ks[17], (L, MEM_WIDTH, D), MEM_WIDTH),
        "w_out": nrm(ks[18], (L, D, D), D),
        "g_mlp": gain(ks[19], (L, D)),
        "w_up": nrm(ks[20], (L, D, D_FF), D),
        "w_down": nrm(ks[21], (L, D_FF, D), D_FF),
    }


def reference(x, mem, g_mix, w_in, b_f, g_qA, g_kA, g_qB, g_kB, g_mem, w_mem_kv, g_qM, g_kM,
              w_gate, b_gate, w_br_a, w_br_b, w_br_m, w_out, g_mlp, w_up, w_down):
    B, T, D = x.shape
    dt = x.dtype
    inv_freq = ROPE_THETA ** (-jnp.arange(0, ROT_DIM, 2, dtype=jnp.float32) / ROT_DIM)
    ang = jnp.arange(T, dtype=jnp.float32)[:, None] * inv_freq[None, :]
    cos, sin = jnp.cos(ang), jnp.sin(ang)
    offs = []
    acc = 0
    for w in IN_SPLITS[:-1]:
        acc += w
        offs.append(acc)

    for l in range(DEPTH):
        h = rmsnorm(x, g_mix[l])
        proj = h @ w_in[l]
        qa, ka, va, qb, kb, vb, fl, qm = jnp.split(proj, offs, axis=-1)

        qa = partial_rope(rmsnorm(qa.reshape(B, T, N_DIL_GROUPS, DIL_HEADS, HEAD_DIM), g_qA[l]), cos, sin)
        ka = partial_rope(rmsnorm(ka.reshape(B, T, N_DIL_GROUPS, DIL_HEADS, HEAD_DIM), g_kA[l]), cos, sin)
        va = va.reshape(B, T, N_DIL_GROUPS, DIL_HEADS, HEAD_DIM)
        outs, lses = [], []
        for gi, (win, dil) in enumerate(DIL_GROUPS):
            o, s = dilated_group(qa[:, :, gi], ka[:, :, gi], va[:, :, gi], win, dil)
            outs.append(o)
            lses.append(s)
        alpha = jax.nn.softmax(jnp.stack(lses, axis=0), axis=0)
        ya = jnp.sum(alpha[..., None] * jnp.stack(outs, axis=0), axis=0).reshape(B, T, DIL_OUT).astype(dt)

        qb = rmsnorm(qb.reshape(B, T, FOX_HEADS, HEAD_DIM), g_qB[l])
        kb = rmsnorm(kb.reshape(B, T, FOX_HEADS, HEAD_DIM), g_kB[l])
        vb = vb.reshape(B, T, FOX_HEADS, HEAD_DIM)
        logf = jax.nn.log_sigmoid(fl.astype(jnp.float32) + b_f[l].astype(jnp.float32))
        yb = fox_attention(qb, kb, vb, logf).astype(dt)

        mem_n = rmsnorm(mem, g_mem[l])
        km, vm = jnp.split(mem_n @ w_mem_kv[l], 2, axis=-1)
        km = rmsnorm(km.reshape(B, MEM_LEN, MEM_HEADS, MEM_HEAD_DIM), g_kM[l]).astype(jnp.float32)
        vm = vm.reshape(B, MEM_LEN, MEM_HEADS, MEM_HEAD_DIM).astype(jnp.float32)
        qm = rmsnorm(qm.reshape(B, T, MEM_HEADS, MEM_HEAD_DIM), g_qM[l]).astype(jnp.float32)
        sm = jnp.einsum('bthe,bshe->bhts', qm, km) * (MEM_HEAD_DIM ** -0.5)
        pm = jax.nn.softmax(sm, axis=-1)
        ym = jnp.einsum('bhts,bshe->bthe', pm, vm).reshape(B, T, MEM_WIDTH).astype(dt)

        gates = jax.nn.sigmoid(h @ w_gate[l] + b_gate[l]).reshape(B, T, N_BRANCH, D)
        merged = (gates[:, :, 0] * (ya @ w_br_a[l])
                  + gates[:, :, 1] * (yb @ w_br_b[l])
                  + gates[:, :, 2] * (ym @ w_br_m[l]))
        x = x + merged @ w_out[l]

        h2 = rmsnorm(x, g_mlp[l])
        x = x + jnp.square(jax.nn.relu(h2 @ w_up[l])) @ w_down[l]
    return x
```

```python
import functools

import jax
import jax.numpy as jnp
from jax import lax
from jax.experimental import pallas as pl
from jax.experimental.pallas import tpu as pltpu

D_MODEL = 1024
HEAD_DIM = 64
DIL_GROUPS = ((128, 1), (512, 4), (2048, 16))
N_DIL_GROUPS = 3
DIL_HEADS = 4
DIL_WIDTH = N_DIL_GROUPS * DIL_HEADS * HEAD_DIM
DIL_OUT = DIL_HEADS * HEAD_DIM
DIL_BACK = 128
FOX_HEADS = 8
FOX_WIDTH = FOX_HEADS * HEAD_DIM
MEM_HEADS = 4
MEM_HEAD_DIM = 128
MEM_WIDTH = MEM_HEADS * MEM_HEAD_DIM
ROT_DIM = HEAD_DIM // 4
ROPE_THETA = 500000.0
D_FF = 4 * D_MODEL
EPS = 1e-6

LANES = 128
MXU_COLS = 256
FL_PAD = LANES
DECAY_COLS = 6

ROW_TILE = 512
FOX_BLOCK = 256
CUMSUM_BLOCK = 256
DIL_Q_TILE = 512

BF16 = jnp.bfloat16
F32 = jnp.float32
NT_DIMS = (((1,), (1,)), ((), ()))


def _dot(a, b):
    return jnp.dot(a, b, preferred_element_type=F32)


def _dot_nt(a, b):
    return lax.dot_general(a, b, NT_DIMS, preferred_element_type=F32)


def _row_rmsnorm(x, g):
    ms = jnp.mean(x * x, axis=-1, keepdims=True)
    return x * lax.rsqrt(ms + EPS) * g


def _head_rmsnorm(y, blockdiag, gain, head_dim):
    ss = _dot((y * y).astype(BF16), blockdiag)
    return y * lax.rsqrt(ss * (1.0 / head_dim) + EPS) * gain


def _split3(c):
    hi = c.astype(BF16)
    r1 = c - hi.astype(F32)
    mid = r1.astype(BF16)
    lo = (r1 - mid.astype(F32)).astype(BF16)
    return hi, mid, lo


def _inproj_kernel(x_ref, gmix_ref, w_ref, bd64_ref, bd128_ref, gqa_ref, gka_ref, gqb_ref, gkb_ref,
                   gqm_ref, bf_ref, rc_ref, rs1_ref, rs2_ref,
                   h_ref, qa_ref, ka_ref, va_ref, qb_ref, kb_ref, vb_ref, qm_ref, lf_ref):
    h = _row_rmsnorm(x_ref[...], gmix_ref[...]).astype(BF16)
    h_ref[...] = h
    bd64 = bd64_ref[...]
    bd128 = bd128_ref[...]
    rc = rc_ref[...]
    rs1 = rs1_ref[...]
    rs2 = rs2_ref[...]
    cw = MXU_COLS

    def proj(col, width=cw):
        return _dot(h, w_ref[:, col:col + width])

    def rope(y):
        return y * rc + pltpu.roll(y, cw - ROT_DIM // 2, 1) * rs1 + pltpu.roll(y, ROT_DIM // 2, 1) * rs2

    col = 0
    for out_ref, gain_ref, bd, hd, rot in (
            (qa_ref, gqa_ref, bd64, HEAD_DIM, True),
            (ka_ref, gka_ref, bd64, HEAD_DIM, True),
            (va_ref, None, None, None, False),
            (qb_ref, gqb_ref, bd64, HEAD_DIM, False),
            (kb_ref, gkb_ref, bd64, HEAD_DIM, False),
            (vb_ref, None, None, None, False),
            (qm_ref, gqm_ref, bd128, MEM_HEAD_DIM, False)):
        width = out_ref.shape[1]
        for c in range(0, width, cw):
            y = proj(col + c)
            if gain_ref is not None:
                y = _head_rmsnorm(y, bd, gain_ref[...], hd)
            if rot:
                y = rope(y)
            out_ref[:, c:c + cw] = y.astype(out_ref.dtype)
        col += width
    z = proj(col, FL_PAD) + bf_ref[...]
    lf_ref[...] = jnp.minimum(z, 0.0) - jnp.log1p(jnp.exp(-jnp.abs(z)))


def _cumsum_kernel(lf_ref, tri_ref, cq_ref, ck_ref):
    tri = tri_ref[...]
    n_blocks = lf_ref.shape[0] // CUMSUM_BLOCK
    lane = lax.broadcasted_iota(jnp.int32, (CUMSUM_BLOCK, LANES), 1)
    one = jnp.ones((CUMSUM_BLOCK, LANES), F32)
    zero = jnp.zeros((CUMSUM_BLOCK, LANES), F32)
    carry = jnp.zeros((1, LANES), F32)
    for blk in range(n_blocks):
        rows = slice(blk * CUMSUM_BLOCK, (blk + 1) * CUMSUM_BLOCK)
        hi, mid, lo = _split3(lf_ref[rows, :])
        c = _dot(tri, hi) + _dot(tri, mid) + _dot(tri, lo) + carry
        carry = c[CUMSUM_BLOCK - 1:CUMSUM_BLOCK, :]
        for hp in range(FOX_HEADS // 2):
            qe = zero
            ke = zero
            for e in range(2):
                head = 2 * hp + e
                pieces = _split3(jnp.broadcast_to(c[:, head:head + 1], (CUMSUM_BLOCK, LANES)))
                base = DECAY_COLS * e
                for t in range(3):
                    pc = pieces[t].astype(F32)
                    qe = jnp.where(lane == base + t, pc, qe)
                    qe = jnp.where(lane == base + 3 + t, one, qe)
                    ke = jnp.where(lane == base + t, one, ke)
                    ke = jnp.where(lane == base + 3 + t, -pc, ke)
            cq_ref[hp, rows, :] = qe.astype(BF16)
            ck_ref[hp, rows, :] = ke.astype(BF16)


def _fox_kernel(q_ref, cq_ref, k_ref, ck_ref, v_ref, o_ref):
    i = pl.program_id(2)
    tq = FOX_BLOCK
    q2 = q_ref[...].astype(F32)
    ce = cq_ref[...].astype(F32)
    lane = lax.broadcasted_iota(jnp.int32, (tq, LANES), 1)
    q_aug = []
    for e in range(2):
        in_head = (lane >= e * HEAD_DIM) & (lane < (e + 1) * HEAD_DIM)
        in_decay = (lane >= e * DECAY_COLS) & (lane < (e + 1) * DECAY_COLS)
        q_aug.append(jnp.concatenate([jnp.where(in_head, q2, 0.0).astype(BF16),
                                      jnp.where(in_decay, ce, 0.0).astype(BF16)], axis=1))

    def kv_block(j):
        rows = pl.ds(pl.multiple_of(j * tq, tq), tq)
        k_aug = jnp.concatenate([k_ref[rows, :], ck_ref[rows, :]], axis=1)
        return k_aug, v_ref[rows, :]

    k_aug, v2 = kv_block(i)
    causal = (lax.broadcasted_iota(jnp.int32, (tq, tq), 1) <= lax.broadcasted_iota(jnp.int32, (tq, tq), 0))
    state = []
    for e in range(2):
        s = jnp.where(causal, _dot_nt(q_aug[e], k_aug), -jnp.inf)
        m = jnp.max(s, axis=-1, keepdims=True)
        p = jnp.exp(s - m)
        l = jnp.sum(p, axis=-1, keepdims=True)
        acc = _dot(p.astype(BF16), v2)
        state += [m, l, acc]

    def body(j, st):
        k_aug, v2 = kv_block(j)
        new = []
        for e in range(2):
            m, l, acc = st[3 * e:3 * e + 3]
            s = _dot_nt(q_aug[e], k_aug)
            m_new = jnp.maximum(m, jnp.max(s, axis=-1, keepdims=True))
            alpha = jnp.exp(m - m_new)
            p = jnp.exp(s - m_new)
            l = alpha * l + jnp.sum(p, axis=-1, keepdims=True)
            acc = alpha * acc + _dot(p.astype(BF16), v2)
            new += [m_new, l, acc]
        return tuple(new)

    st = lax.fori_loop(0, i, body, tuple(state))
    o0 = st[2] / st[1]
    o1 = st[5] / st[4]
    o_ref[...] = jnp.where(lane < HEAD_DIM, o0, o1).astype(o_ref.dtype)


def _dilated_kernel(q_ref, k_ref, kp_ref, v_ref, vp_ref, o_ref, lse_ref):
    i = pl.program_id(2)
    c = DIL_BACK
    tq = q_ref.shape[0]
    w = q_ref.shape[1]
    lane = lax.broadcasted_iota(jnp.int32, (c, w), 1)
    qi = lax.broadcasted_iota(jnp.int32, (c, 2 * c), 0)
    kj = lax.broadcasted_iota(jnp.int32, (c, 2 * c), 1)
    dist = qi + c - kj
    in_band = (dist >= 0) & (dist <= DIL_BACK)
    for sb in range(tq // c):
        rows = slice(sb * c, (sb + 1) * c)
        q = q_ref[rows, :].astype(F32)
        if sb == 0:
            kk = jnp.concatenate([kp_ref[...], k_ref[rows, :]], axis=0)
            vv = jnp.concatenate([vp_ref[...], v_ref[rows, :]], axis=0)
            mask = in_band & (kj >= jnp.where(i > 0, 0, c))
        else:
            kk = k_ref[(sb - 1) * c:(sb + 1) * c, :]
            vv = v_ref[(sb - 1) * c:(sb + 1) * c, :]
            mask = in_band
        o_all = jnp.zeros((c, w), F32)
        lse_all = jnp.zeros((c, w), F32)
        for hh in range(DIL_HEADS):
            in_head = (lane >= hh * HEAD_DIM) & (lane < (hh + 1) * HEAD_DIM)
            s = _dot_nt(jnp.where(in_head, q, 0.0).astype(BF16), kk)
            s = jnp.where(mask, s, -jnp.inf)
            m = jnp.max(s, axis=-1, keepdims=True)
            p = jnp.exp(s - m)
            den = jnp.sum(p, axis=-1, keepdims=True)
            o = _dot(p.astype(BF16), vv) / den
            o_all = jnp.where(in_head, o, o_all)
            lse_all = jnp.where(in_head, m + jnp.log(den), lse_all)
        o_ref[rows, :] = o_all.astype(o_ref.dtype)
        lse_ref[rows, :] = lse_all


def _memkv_kernel(mem_ref, g_ref, w_ref, gk_ref, km_ref, vm_ref):
    mn = _row_rmsnorm(mem_ref[...], g_ref[...]).astype(BF16)
    kv = _dot(mn, w_ref[...])
    gk = gk_ref[...]
    for hh in range(MEM_HEADS):
        cols = slice(hh * MEM_HEAD_DIM, (hh + 1) * MEM_HEAD_DIM)
        km_ref[:, cols] = _row_rmsnorm(kv[:, cols], gk).astype(km_ref.dtype)
    vm_ref[...] = kv[:, MEM_WIDTH:].astype(vm_ref.dtype)


def _merge_kernel(x_ref, h_ref, o1_ref, o2_ref, o3_ref, l1_ref, l2_ref, l3_ref, yb_ref, qm_ref, km_ref, vm_ref,
                  wg_ref, bg_ref, wa_ref, wb_ref, wm_ref, wo_ref, out_ref, merged_ref):
    l1, l2, l3 = l1_ref[...], l2_ref[...], l3_ref[...]
    mx = jnp.maximum(jnp.maximum(l1, l2), l3)
    e1, e2, e3 = jnp.exp(l1 - mx), jnp.exp(l2 - mx), jnp.exp(l3 - mx)
    ya = (e1 * o1_ref[...].astype(F32) + e2 * o2_ref[...].astype(F32) + e3 * o3_ref[...].astype(F32)) / (e1 + e2 + e3)
    ya = ya.astype(BF16)
    ym = []
    for hh in range(MEM_HEADS):
        cols = slice(hh * MEM_HEAD_DIM, (hh + 1) * MEM_HEAD_DIM)
        s = _dot_nt(qm_ref[:, cols], km_ref[:, cols])
        p = jnp.exp(s - jnp.max(s, axis=-1, keepdims=True))
        den = jnp.sum(p, axis=-1, keepdims=True)
        ym.append((_dot(p.astype(BF16), vm_ref[:, cols]) / den).astype(BF16))
    ym = jnp.concatenate(ym, axis=1)
    yb = yb_ref[...]
    h = h_ref[...]
    cw = 2 * MXU_COLS
    for c in range(0, D_MODEL, cw):
        merged = None
        for k, (y, w_ref) in enumerate(((ya, wa_ref), (yb, wb_ref), (ym, wm_ref))):
            gcol = k * D_MODEL + c
            gate = jax.nn.sigmoid(_dot(h, wg_ref[:, gcol:gcol + cw]) + bg_ref[:, gcol:gcol + cw])
            term = gate * _dot(y, w_ref[:, c:c + cw])
            merged = term if merged is None else merged + term
        merged_ref[:, c:c + cw] = merged.astype(BF16)
    out_ref[...] = x_ref[...] + _dot(merged_ref[...], wo_ref[...])


def _mlp_kernel(x_ref, g_ref, wu_ref, wd_ref, out_ref, h2_ref):
    x = x_ref[...]
    h2_ref[...] = _row_rmsnorm(x, g_ref[...]).astype(BF16)
    cw = D_MODEL
    acc = x
    for c in range(0, D_FF, cw):
        u = jnp.maximum(_dot(h2_ref[...], wu_ref[:, c:c + cw]), 0.0)
        acc = acc + _dot((u * u).astype(BF16), wd_ref[c:c + cw, :])
    out_ref[...] = acc


def _full(shape):
    return pl.BlockSpec(shape, lambda *_: (0,) * len(shape))


def _params(*sem):
    return pltpu.CompilerParams(dimension_semantics=sem)


def _blockdiag(width, head_dim):
    r = jnp.arange(width) // head_dim
    return (r[:, None] == r[None, :]).astype(BF16)


def _rope_tables(seq, width):
    half = ROT_DIM // 2
    inv_freq = ROPE_THETA ** (-jnp.arange(0, ROT_DIM, 2, dtype=F32) / ROT_DIM)
    ang = jnp.arange(seq, dtype=F32)[:, None] * inv_freq[None, :]
    cos, sin = jnp.cos(ang), jnp.sin(ang)
    ones = jnp.ones((seq, HEAD_DIM - ROT_DIM), F32)
    zeros = jnp.zeros((seq, HEAD_DIM - ROT_DIM), F32)
    zhalf = jnp.zeros((seq, half), F32)
    rc = jnp.concatenate([cos, cos, ones], axis=1)
    rs1 = jnp.concatenate([-sin, zhalf, zeros], axis=1)
    rs2 = jnp.concatenate([zhalf, sin, zeros], axis=1)
    reps = width // HEAD_DIM
    return tuple(jnp.tile(t, (1, reps)) for t in (rc, rs1, rs2))


def kernel(x, mem, g_mix, w_in, b_f, g_qA, g_kA, g_qB, g_kB, g_mem, w_mem_kv, g_qM, g_kM, w_gate, b_gate,
           w_br_a, w_br_b, w_br_m, w_out, g_mlp, w_up, w_down):
    B, T, D = x.shape
    assert D == D_MODEL and w_in.shape[0] == 1, "single-layer kernel"
    N = B * T
    mem_len = mem.shape[1]
    tm = ROW_TILE
    assert T % tm == 0 and T % FOX_BLOCK == 0
    x2d = x.reshape(N, D)

    wi = w_in[0]
    offs = [0]
    for wdt in (DIL_WIDTH, DIL_WIDTH, DIL_WIDTH, FOX_WIDTH, FOX_WIDTH, FOX_WIDTH, FOX_HEADS, MEM_WIDTH):
        offs.append(offs[-1] + wdt)
    w_fl = jnp.pad(wi[:, offs[6]:offs[7]], ((0, 0), (0, FL_PAD - FOX_HEADS)))
    w_all = jnp.concatenate([wi[:, :offs[6]], wi[:, offs[7]:], w_fl], axis=1).astype(BF16)
    n_cols = w_all.shape[1]
    bf_pad = jnp.pad(b_f[0], (0, FL_PAD - FOX_HEADS)).reshape(1, FL_PAD)
    cw = MXU_COLS
    q_scale = HEAD_DIM ** -0.5
    gqa = (jnp.tile(g_qA[0], cw // HEAD_DIM) * q_scale).reshape(1, cw)
    gka = jnp.tile(g_kA[0], cw // HEAD_DIM).reshape(1, cw)
    gqb = (jnp.tile(g_qB[0], cw // HEAD_DIM) * q_scale).reshape(1, cw)
    gkb = jnp.tile(g_kB[0], cw // HEAD_DIM).reshape(1, cw)
    gqm = (jnp.tile(g_qM[0], cw // MEM_HEAD_DIM) * MEM_HEAD_DIM ** -0.5).reshape(1, cw)
    rc, rs1, rs2 = _rope_tables(T, cw)
    bd64 = _blockdiag(cw, HEAD_DIM)
    bd128 = _blockdiag(cw, MEM_HEAD_DIM)

    row = lambda width: pl.BlockSpec((tm, width), lambda i: (i, 0))
    rope_spec = pl.BlockSpec((tm, cw), lambda i: (i % (T // tm), 0))
    out_widths = (D, DIL_WIDTH, DIL_WIDTH, DIL_WIDTH, FOX_WIDTH, FOX_WIDTH, FOX_WIDTH, MEM_WIDTH)
    h, qa, ka, va, qb, kb, vb, qm, logf = pl.pallas_call(
        _inproj_kernel,
        grid=(N // tm,),
        in_specs=[row(D), _full((1, D)), _full((D, n_cols)), _full((cw, cw)), _full((cw, cw)),
                  _full((1, cw)), _full((1, cw)), _full((1, cw)), _full((1, cw)), _full((1, cw)),
                  _full((1, FL_PAD)), rope_spec, rope_spec, rope_spec],
        out_specs=[row(wdt) for wdt in out_widths] + [row(FL_PAD)],
        out_shape=[jax.ShapeDtypeStruct((N, wdt), BF16) for wdt in out_widths]
        + [jax.ShapeDtypeStruct((N, FL_PAD), F32)],
        compiler_params=_params("parallel"),
        name="inproj",
    )(x2d, g_mix, w_all, bd64, bd128, gqa, gka, gqb, gkb, gqm, bf_pad, rc, rs1, rs2)

    n_hp = FOX_HEADS // 2
    tri = (jnp.arange(CUMSUM_BLOCK)[:, None] >= jnp.arange(CUMSUM_BLOCK)[None, :]).astype(BF16)
    ext_spec = pl.BlockSpec((None, n_hp, T, LANES), lambda b: (b, 0, 0, 0))
    cq_ext, ck_ext = pl.pallas_call(
        _cumsum_kernel,
        grid=(B,),
        in_specs=[pl.BlockSpec((T, FL_PAD), lambda b: (b, 0)), _full((CUMSUM_BLOCK, CUMSUM_BLOCK))],
        out_specs=[ext_spec, ext_spec],
        out_shape=[jax.ShapeDtypeStruct((B, n_hp, T, LANES), BF16)] * 2,
        compiler_params=_params("parallel"),
        name="decay_cumsum",
    )(logf, tri)

    nq = T // FOX_BLOCK
    q_spec = pl.BlockSpec((FOX_BLOCK, LANES), lambda b, hp, i: (b * nq + i, hp))
    kv_spec = pl.BlockSpec((T, LANES), lambda b, hp, i: (b, hp))
    yb = pl.pallas_call(
        _fox_kernel,
        grid=(B, n_hp, nq),
        in_specs=[q_spec, pl.BlockSpec((None, None, FOX_BLOCK, LANES), lambda b, hp, i: (b, hp, i, 0)),
                  kv_spec, pl.BlockSpec((None, None, T, LANES), lambda b, hp, i: (b, hp, 0, 0)), kv_spec],
        out_specs=q_spec,
        out_shape=jax.ShapeDtypeStruct((N, FOX_WIDTH), BF16),
        compiler_params=_params("parallel", "parallel", "arbitrary"),
        name="fox_attention",
    )(qb, cq_ext, kb, ck_ext, vb)

    dil_o, dil_lse = [], []
    for gi, (win, d) in enumerate(DIL_GROUPS):
        assert win // d == DIL_BACK
        n = T // d
        tq = min(DIL_Q_TILE, n)
        nblk = n // tq
        sub = tq // DIL_BACK
        view = lambda a: a.reshape(B * n, d * DIL_WIDTH)
        cur = pl.BlockSpec((tq, DIL_OUT), lambda b, r, i: (b * nblk + i, r * N_DIL_GROUPS + gi))
        prev = pl.BlockSpec((DIL_BACK, DIL_OUT),
                            lambda b, r, i: (b * (n // DIL_BACK) + jnp.maximum(i * sub - 1, 0), r * N_DIL_GROUPS + gi))
        out_spec = pl.BlockSpec((tq, DIL_OUT), lambda b, r, i: (b * nblk + i, r))
        o_g, lse_g = pl.pallas_call(
            _dilated_kernel,
            grid=(B, d, nblk),
            in_specs=[cur, cur, prev, cur, prev],
            out_specs=[out_spec, out_spec],
            out_shape=[jax.ShapeDtypeStruct((B * n, d * DIL_OUT), BF16),
                       jax.ShapeDtypeStruct((B * n, d * DIL_OUT), F32)],
            compiler_params=_params("parallel", "parallel", "arbitrary"),
            name=f"dilated_d{d}",
        )(view(qa), view(ka), view(ka), view(va), view(va))
        dil_o.append(o_g.reshape(N, DIL_OUT))
        dil_lse.append(lse_g.reshape(N, DIL_OUT))

    gkm = g_kM[0].reshape(1, MEM_HEAD_DIM)
    km, vm = pl.pallas_call(
        _memkv_kernel,
        grid=(B,),
        in_specs=[pl.BlockSpec((None, mem_len, D), lambda b: (b, 0, 0)), _full((1, D)),
                  _full((D, 2 * MEM_WIDTH)), _full((1, MEM_HEAD_DIM))],
        out_specs=[pl.BlockSpec((None, mem_len, MEM_WIDTH), lambda b: (b, 0, 0))] * 2,
        out_shape=[jax.ShapeDtypeStruct((B, mem_len, MEM_WIDTH), BF16)] * 2,
        compiler_params=_params("parallel"),
        name="mem_kv",
    )(mem, g_mem, w_mem_kv[0].astype(BF16), gkm)

    mem_spec = pl.BlockSpec((None, mem_len, MEM_WIDTH), lambda i: (i // (T // tm), 0, 0))
    x_mid = pl.pallas_call(
        _merge_kernel,
        grid=(N // tm,),
        in_specs=[row(D), row(D)] + [row(DIL_OUT)] * 6 + [row(FOX_WIDTH), row(MEM_WIDTH), mem_spec, mem_spec,
                  _full((D, 3 * D)), _full((1, 3 * D)), _full((DIL_OUT, D)), _full((FOX_WIDTH, D)),
                  _full((MEM_WIDTH, D)), _full((D, D))],
        out_specs=row(D),
        out_shape=jax.ShapeDtypeStruct((N, D), F32),
        scratch_shapes=[pltpu.VMEM((tm, D), BF16)],
        compiler_params=_params("parallel"),
        name="merge_outproj",
    )(x2d, h, *dil_o, *dil_lse, yb, qm, km, vm, w_gate[0].astype(BF16), b_gate, w_br_a[0].astype(BF16),
      w_br_b[0].astype(BF16), w_br_m[0].astype(BF16), w_out[0].astype(BF16))

    out = pl.pallas_call(
        _mlp_kernel,
        grid=(N // tm,),
        in_specs=[row(D), _full((1, D)), _full((D, D_FF)), _full((D_FF, D))],
        out_specs=row(D),
        out_shape=jax.ShapeDtypeStruct((N, D), F32),
        scratch_shapes=[pltpu.VMEM((tm, D), BF16)],
        compiler_params=_params("parallel"),
        name="mlp",
    )(x_mid, g_mlp, w_up[0].astype(BF16), w_down[0].astype(BF16))
    return out.reshape(B, T, D)
```

```python
import functools

import jax
import jax.numpy as jnp
from jax import lax
from jax.experimental import pallas as pl
from jax.experimental.pallas import tpu as pltpu

D_MODEL = 1024
HEAD_DIM = 64
DIL_GROUPS = ((128, 1), (512, 4), (2048, 16))
N_DIL_GROUPS = 3
DIL_HEADS = 4
DIL_WIDTH = N_DIL_GROUPS * DIL_HEADS * HEAD_DIM
DIL_OUT = DIL_HEADS * HEAD_DIM
DIL_BACK = 128
FOX_HEADS = 8
FOX_WIDTH = FOX_HEADS * HEAD_DIM
MEM_HEADS = 4
MEM_HEAD_DIM = 128
MEM_WIDTH = MEM_HEADS * MEM_HEAD_DIM
ROT_DIM = HEAD_DIM // 4
ROPE_THETA = 500000.0
D_FF = 4 * D_MODEL
EPS = 1e-6

LANES = 128
MXU_COLS = 256
FL_PAD = LANES

ROW_TILE = 512
FOX_Q_TILE = 1024
FOX_K_TILE = 256
FOX_COLS = 512
FOX_DEN_ROWS = 16
DECAY_ROWS = 16
CUMSUM_BLOCK = 256
DIL_Q_TILE = 512

LOG2E = 1.4426950408889634
BF16 = jnp.bfloat16
F32 = jnp.float32
NT_DIMS = (((1,), (1,)), ((), ()))


def _dot(a, b):
    return jnp.dot(a, b, preferred_element_type=F32)


def _dot_nt(a, b):
    return lax.dot_general(a, b, NT_DIMS, preferred_element_type=F32)


def _row_rmsnorm(x, g):
    ms = jnp.mean(x * x, axis=-1, keepdims=True)
    return x * lax.rsqrt(ms + EPS) * g


def _head_rmsnorm(y, blockdiag, gain, head_dim):
    ss = _dot((y * y).astype(BF16), blockdiag)
    return y * lax.rsqrt(ss * (1.0 / head_dim) + EPS) * gain


def _split3(c):
    hi = c.astype(BF16)
    r1 = c - hi.astype(F32)
    mid = r1.astype(BF16)
    lo = (r1 - mid.astype(F32)).astype(BF16)
    return hi, mid, lo


def _inproj_kernel(x_ref, gmix_ref, w_ref, bd64_ref, bd128_ref, gqa_ref, gka_ref, gqb_ref, gkb_ref,
                   gqm_ref, bf_ref, rc_ref, rs1_ref, rs2_ref,
                   h_ref, qa_ref, ka_ref, va_ref, qbt_ref, kb_ref, vbt_ref, qm_ref, lf_ref):
    h = _row_rmsnorm(x_ref[...], gmix_ref[...]).astype(BF16)
    h_ref[...] = h
    bd64 = bd64_ref[...]
    bd128 = bd128_ref[...]
    rc = rc_ref[...]
    rs1 = rs1_ref[...]
    rs2 = rs2_ref[...]
    cw = MXU_COLS

    def proj(col, width=cw):
        return _dot(h, w_ref[:, col:col + width])

    def rope(y):
        return y * rc + pltpu.roll(y, cw - ROT_DIM // 2, 1) * rs1 + pltpu.roll(y, ROT_DIM // 2, 1) * rs2

    col = 0
    for out_ref, gain_ref, bd, hd, rot, transposed in (
            (qa_ref, gqa_ref, bd64, HEAD_DIM, True, False),
            (ka_ref, gka_ref, bd64, HEAD_DIM, True, False),
            (va_ref, None, None, None, False, False),
            (qbt_ref, gqb_ref, bd64, HEAD_DIM, False, True),
            (kb_ref, gkb_ref, bd64, HEAD_DIM, False, False),
            (vbt_ref, None, None, None, False, True),
            (qm_ref, gqm_ref, bd128, MEM_HEAD_DIM, False, False)):
        width = out_ref.shape[0] if transposed else out_ref.shape[1]
        for c in range(0, width, cw):
            y = proj(col + c)
            if gain_ref is not None:
                y = _head_rmsnorm(y, bd, gain_ref[...], hd)
            if rot:
                y = rope(y)
            if transposed:
                out_ref[c:c + cw, :] = y.T.astype(out_ref.dtype)
            else:
                out_ref[:, c:c + cw] = y.astype(out_ref.dtype)
        col += width
    z = proj(col, FL_PAD) + bf_ref[...]
    lf_ref[...] = jnp.minimum(z, 0.0) - jnp.log1p(jnp.exp(-jnp.abs(z)))


def _cumsum_kernel(lf_ref, tri_ref, cqt_ref, ck_ref):
    tri = tri_ref[...]
    n_blocks = lf_ref.shape[0] // CUMSUM_BLOCK
    lane = lax.broadcasted_iota(jnp.int32, (CUMSUM_BLOCK, LANES), 1)
    row = lax.broadcasted_iota(jnp.int32, (DECAY_ROWS, CUMSUM_BLOCK), 0)
    carry = jnp.zeros((1, LANES), F32)
    for blk in range(n_blocks):
        rows = slice(blk * CUMSUM_BLOCK, (blk + 1) * CUMSUM_BLOCK)
        hi, mid, lo = _split3(lf_ref[rows, :])
        c = _dot(tri, hi) + _dot(tri, mid) + _dot(tri, lo) + carry
        carry = c[CUMSUM_BLOCK - 1:CUMSUM_BLOCK, :]
        c = c * LOG2E
        ct = c.T
        for hp in range(FOX_HEADS // 2):
            ke = jnp.zeros((CUMSUM_BLOCK, LANES), F32)
            for e in range(2):
                head = 2 * hp + e
                pieces = [t.astype(F32) for t in _split3(jnp.broadcast_to(c[:, head:head + 1], (CUMSUM_BLOCK, LANES)))]
                base = DECAY_ROWS * e
                for t in range(3):
                    ke = jnp.where(lane == base + t, 1.0, ke)
                    ke = jnp.where(lane == base + 3 + t, -pieces[t], ke)
                pieces_t = [t.astype(F32) for t in
                            _split3(jnp.broadcast_to(ct[head:head + 1, :], (DECAY_ROWS, CUMSUM_BLOCK)))]
                qe = jnp.where(row < 6, 1.0, 0.0)
                for t in range(3):
                    qe = jnp.where(row == t, pieces_t[t], qe)
                cqt_ref[hp, e, :, rows] = qe.astype(BF16)
            ck_ref[hp, rows, :] = ke.astype(BF16)


def _fox_kernel(qt_ref, cqt_ref, k_ref, ck_ref, vt_ref, o_ref, qa_ref, st_ref, m_ref, acc_ref):
    blk = pl.program_id(2)
    tq = FOX_Q_TILE
    tk = FOX_K_TILE
    hd = HEAD_DIM
    cw = FOX_COLS
    zq = jnp.zeros((hd, tq), BF16)
    zd = jnp.zeros((DECAY_ROWS, tq), BF16)
    zpad = jnp.zeros((MXU_COLS - 2 * hd - 2 * DECAY_ROWS, tq), BF16)
    for e in range(2):
        parts = [zq, zq, zd, zd, zpad]
        parts[e] = qt_ref[e * hd:(e + 1) * hd, :]
        parts[2 + e] = cqt_ref[e]
        qa_ref[e] = jnp.concatenate(parts, axis=0)
    units = [(e, c0) for e in range(2) for c0 in range(0, tq, cw)]

    def key_block(key_start):
        rows = pl.ds(pl.multiple_of(key_start, tk), tk)
        return jnp.concatenate([k_ref[rows, :], ck_ref[rows, :]], axis=1)

    def value_block(key_start):
        return vt_ref[:, pl.ds(pl.multiple_of(key_start, tk), tk)]

    def issue_scores(buf, u, k_aug):
        e, c0 = units[u]
        st_ref[buf, u] = _dot(k_aug, qa_ref[e, :, c0:c0 + cw])

    def update(buf, u, vt, key_off):
        e, c0 = units[u]
        st = st_ref[buf, u]
        if key_off is not None and key_off + tk - 1 > c0:
            key = lax.broadcasted_iota(jnp.int32, st.shape, 0) + key_off
            qry = lax.broadcasted_iota(jnp.int32, st.shape, 1) + c0
            st = jnp.where(key <= qry, st, -jnp.inf)
        m = m_ref[u]
        m_new = jnp.maximum(m, jnp.max(st, axis=0, keepdims=True))
        alpha = jnp.exp2(m - m_new)
        p = jnp.exp2(st - m_new).astype(BF16)
        vt_aug = jnp.concatenate([vt[e * hd:(e + 1) * hd, :], ones_rows], axis=0)
        acc_ref[u] = alpha * acc_ref[u] + _dot(vt_aug, p)
        m_ref[u] = m_new

    ones_rows = jnp.ones((FOX_DEN_ROWS, tk), BF16)
    m_ref[...] = jnp.full(m_ref.shape, -jnp.inf, F32)
    acc_ref[...] = jnp.zeros(acc_ref.shape, F32)
    k_first = key_block(0)
    for u in range(len(units)):
        issue_scores(0, u, k_first)

    def block_pair(jp, carry):
        for half in range(2):
            j = 2 * jp + half
            k_next = key_block((j + 1) * tk)
            vt = value_block(j * tk)
            for u in range(len(units)):
                issue_scores(1 - half, u, k_next)
                update(half, u, vt, None)
        return carry

    assert (tq // tk) % 2 == 0
    lax.fori_loop(0, blk * (tq // tk // 2), block_pair, 0)
    for jj in range(tq // tk):
        key_off = jj * tk
        active = [u for u, (e, c0) in enumerate(units) if c0 + cw > key_off]
        nxt_off = key_off + tk
        nxt_active = [u for u, (e, c0) in enumerate(units) if c0 + cw > nxt_off] if nxt_off < tq else []
        if nxt_active:
            k_next = key_block(blk * tq + nxt_off)
        vt = value_block(blk * tq + key_off)
        for u in active:
            if u in nxt_active:
                issue_scores((jj + 1) % 2, u, k_next)
            update(jj % 2, u, vt, key_off)
    per_head = []
    for e in range(2):
        cols = [acc_ref[u, :hd, :] * (1.0 / acc_ref[u, hd:hd + 1, :]) for u, unit in enumerate(units) if unit[0] == e]
        per_head.append(jnp.concatenate(cols, axis=1))
    ot = jnp.concatenate(per_head, axis=0)
    o_ref[...] = ot.T.astype(o_ref.dtype)


def _dilated_kernel(q_ref, k_ref, kp_ref, v_ref, vp_ref, o_ref, lse_ref):
    i = pl.program_id(2)
    c = DIL_BACK
    tq = q_ref.shape[0]
    w = q_ref.shape[1]
    lane = lax.broadcasted_iota(jnp.int32, (c, w), 1)
    qi = lax.broadcasted_iota(jnp.int32, (c, 2 * c), 0)
    kj = lax.broadcasted_iota(jnp.int32, (c, 2 * c), 1)
    dist = qi + c - kj
    in_band = (dist >= 0) & (dist <= DIL_BACK)
    for sb in range(tq // c):
        rows = slice(sb * c, (sb + 1) * c)
        q = q_ref[rows, :].astype(F32)
        if sb == 0:
            kk = jnp.concatenate([kp_ref[...], k_ref[rows, :]], axis=0)
            vv = jnp.concatenate([vp_ref[...], v_ref[rows, :]], axis=0)
            mask = in_band & (kj >= jnp.where(i > 0, 0, c))
        else:
            kk = k_ref[(sb - 1) * c:(sb + 1) * c, :]
            vv = v_ref[(sb - 1) * c:(sb + 1) * c, :]
            mask = in_band
        o_all = jnp.zeros((c, w), F32)
        lse_all = jnp.zeros((c, w), F32)
        for hh in range(DIL_HEADS):
            in_head = (lane >= hh * HEAD_DIM) & (lane < (hh + 1) * HEAD_DIM)
            s = _dot_nt(jnp.where(in_head, q, 0.0).astype(BF16), kk)
            s = jnp.where(mask, s, -jnp.inf)
            m = jnp.max(s, axis=-1, keepdims=True)
            p = jnp.exp(s - m)
            den = jnp.sum(p, axis=-1, keepdims=True)
            o = _dot(p.astype(BF16), vv) / den
            o_all = jnp.where(in_head, o, o_all)
            lse_all = jnp.where(in_head, m + jnp.log(den), lse_all)
        o_ref[rows, :] = o_all.astype(o_ref.dtype)
        lse_ref[rows, :] = lse_all


def _memkv_kernel(mem_ref, g_ref, w_ref, gk_ref, km_ref, vm_ref):
    mn = _row_rmsnorm(mem_ref[...], g_ref[...]).astype(BF16)
    kv = _dot(mn, w_ref[...])
    gk = gk_ref[...]
    for hh in range(MEM_HEADS):
        cols = slice(hh * MEM_HEAD_DIM, (hh + 1) * MEM_HEAD_DIM)
        km_ref[:, cols] = _row_rmsnorm(kv[:, cols], gk).astype(km_ref.dtype)
    vm_ref[...] = kv[:, MEM_WIDTH:].astype(vm_ref.dtype)


def _merge_kernel(x_ref, h_ref, o1_ref, o2_ref, o3_ref, l1_ref, l2_ref, l3_ref, yb_ref, qm_ref, km_ref, vm_ref,
                  wg_ref, bg_ref, wa_ref, wb_ref, wm_ref, wo_ref, out_ref, merged_ref):
    l1, l2, l3 = l1_ref[...], l2_ref[...], l3_ref[...]
    mx = jnp.maximum(jnp.maximum(l1, l2), l3)
    e1, e2, e3 = jnp.exp(l1 - mx), jnp.exp(l2 - mx), jnp.exp(l3 - mx)
    ya = (e1 * o1_ref[...].astype(F32) + e2 * o2_ref[...].astype(F32) + e3 * o3_ref[...].astype(F32)) / (e1 + e2 + e3)
    ya = ya.astype(BF16)
    ym = []
    for hh in range(MEM_HEADS):
        cols = slice(hh * MEM_HEAD_DIM, (hh + 1) * MEM_HEAD_DIM)
        s = _dot_nt(qm_ref[:, cols], km_ref[:, cols])
        p = jnp.exp(s - jnp.max(s, axis=-1, keepdims=True))
        den = jnp.sum(p, axis=-1, keepdims=True)
        ym.append((_dot(p.astype(BF16), vm_ref[:, cols]) / den).astype(BF16))
    ym = jnp.concatenate(ym, axis=1)
    yb = yb_ref[...]
    h = h_ref[...]
    cw = 2 * MXU_COLS
    for c in range(0, D_MODEL, cw):
        merged = None
        for k, (y, w_ref) in enumerate(((ya, wa_ref), (yb, wb_ref), (ym, wm_ref))):
            gcol = k * D_MODEL + c
            gate = jax.nn.sigmoid(_dot(h, wg_ref[:, gcol:gcol + cw]) + bg_ref[:, gcol:gcol + cw])
            term = gate * _dot(y, w_ref[:, c:c + cw])
            merged = term if merged is None else merged + term
        merged_ref[:, c:c + cw] = merged.astype(BF16)
    out_ref[...] = x_ref[...] + _dot(merged_ref[...], wo_ref[...])


def _mlp_kernel(x_ref, g_ref, wu_ref, wd_ref, out_ref, h2_ref):
    x = x_ref[...]
    h2_ref[...] = _row_rmsnorm(x, g_ref[...]).astype(BF16)
    cw = D_MODEL
    acc = x
    for c in range(0, D_FF, cw):
        u = jnp.maximum(_dot(h2_ref[...], wu_ref[:, c:c + cw]), 0.0)
        acc = acc + _dot((u * u).astype(BF16), wd_ref[c:c + cw, :])
    out_ref[...] = acc


def _full(shape):
    return pl.BlockSpec(shape, lambda *_: (0,) * len(shape))


def _params(*sem):
    return pltpu.CompilerParams(dimension_semantics=sem)


def _blockdiag(width, head_dim):
    r = jnp.arange(width) // head_dim
    return (r[:, None] == r[None, :]).astype(BF16)


def _rope_tables(seq, width):
    half = ROT_DIM // 2
    inv_freq = ROPE_THETA ** (-jnp.arange(0, ROT_DIM, 2, dtype=F32) / ROT_DIM)
    ang = jnp.arange(seq, dtype=F32)[:, None] * inv_freq[None, :]
    cos, sin = jnp.cos(ang), jnp.sin(ang)
    ones = jnp.ones((seq, HEAD_DIM - ROT_DIM), F32)
    zeros = jnp.zeros((seq, HEAD_DIM - ROT_DIM), F32)
    zhalf = jnp.zeros((seq, half), F32)
    rc = jnp.concatenate([cos, cos, ones], axis=1)
    rs1 = jnp.concatenate([-sin, zhalf, zeros], axis=1)
    rs2 = jnp.concatenate([zhalf, sin, zeros], axis=1)
    reps = width // HEAD_DIM
    return tuple(jnp.tile(t, (1, reps)) for t in (rc, rs1, rs2))


def kernel(x, mem, g_mix, w_in, b_f, g_qA, g_kA, g_qB, g_kB, g_mem, w_mem_kv, g_qM, g_kM, w_gate, b_gate,
           w_br_a, w_br_b, w_br_m, w_out, g_mlp, w_up, w_down):
    B, T, D = x.shape
    assert D == D_MODEL and w_in.shape[0] == 1, "single-layer kernel"
    N = B * T
    mem_len = mem.shape[1]
    tm = ROW_TILE
    assert T % tm == 0 and T % FOX_Q_TILE == 0 and FOX_Q_TILE % FOX_K_TILE == 0
    x2d = x.reshape(N, D)

    wi = w_in[0]
    offs = [0]
    for wdt in (DIL_WIDTH, DIL_WIDTH, DIL_WIDTH, FOX_WIDTH, FOX_WIDTH, FOX_WIDTH, FOX_HEADS, MEM_WIDTH):
        offs.append(offs[-1] + wdt)
    w_fl = jnp.pad(wi[:, offs[6]:offs[7]], ((0, 0), (0, FL_PAD - FOX_HEADS)))
    w_all = jnp.concatenate([wi[:, :offs[6]], wi[:, offs[7]:], w_fl], axis=1).astype(BF16)
    n_cols = w_all.shape[1]
    bf_pad = jnp.pad(b_f[0], (0, FL_PAD - FOX_HEADS)).reshape(1, FL_PAD)
    cw = MXU_COLS
    q_scale = HEAD_DIM ** -0.5
    gqa = (jnp.tile(g_qA[0], cw // HEAD_DIM) * q_scale).reshape(1, cw)
    gka = jnp.tile(g_kA[0], cw // HEAD_DIM).reshape(1, cw)
    gqb = (jnp.tile(g_qB[0], cw // HEAD_DIM) * (q_scale * LOG2E)).reshape(1, cw)
    gkb = jnp.tile(g_kB[0], cw // HEAD_DIM).reshape(1, cw)
    gqm = (jnp.tile(g_qM[0], cw // MEM_HEAD_DIM) * MEM_HEAD_DIM ** -0.5).reshape(1, cw)
    rc, rs1, rs2 = _rope_tables(T, cw)
    bd64 = _blockdiag(cw, HEAD_DIM)
    bd128 = _blockdiag(cw, MEM_HEAD_DIM)

    row = lambda width: pl.BlockSpec((tm, width), lambda i: (i, 0))
    rope_spec = pl.BlockSpec((tm, cw), lambda i: (i % (T // tm), 0))
    tspec = pl.BlockSpec((None, FOX_WIDTH, tm), lambda i: (i // (T // tm), 0, i % (T // tm)))
    tshape = jax.ShapeDtypeStruct((B, FOX_WIDTH, T), BF16)
    rshape = lambda width: jax.ShapeDtypeStruct((N, width), BF16)
    h, qa, ka, va, qbt, kb, vbt, qm, logf = pl.pallas_call(
        _inproj_kernel,
        grid=(N // tm,),
        in_specs=[row(D), _full((1, D)), _full((D, n_cols)), _full((cw, cw)), _full((cw, cw)),
                  _full((1, cw)), _full((1, cw)), _full((1, cw)), _full((1, cw)), _full((1, cw)),
                  _full((1, FL_PAD)), rope_spec, rope_spec, rope_spec],
        out_specs=[row(D), row(DIL_WIDTH), row(DIL_WIDTH), row(DIL_WIDTH), tspec, row(FOX_WIDTH), tspec,
                   row(MEM_WIDTH), row(FL_PAD)],
        out_shape=[rshape(D), rshape(DIL_WIDTH), rshape(DIL_WIDTH), rshape(DIL_WIDTH), tshape, rshape(FOX_WIDTH),
                   tshape, rshape(MEM_WIDTH), jax.ShapeDtypeStruct((N, FL_PAD), F32)],
        compiler_params=_params("parallel"),
        name="inproj",
    )(x2d, g_mix, w_all, bd64, bd128, gqa, gka, gqb, gkb, gqm, bf_pad, rc, rs1, rs2)

    n_hp = FOX_HEADS // 2
    tri = (jnp.arange(CUMSUM_BLOCK)[:, None] >= jnp.arange(CUMSUM_BLOCK)[None, :]).astype(BF16)
    cqt, ck_ext = pl.pallas_call(
        _cumsum_kernel,
        grid=(B,),
        in_specs=[pl.BlockSpec((T, FL_PAD), lambda b: (b, 0)), _full((CUMSUM_BLOCK, CUMSUM_BLOCK))],
        out_specs=[pl.BlockSpec((None, n_hp, 2, DECAY_ROWS, T), lambda b: (b, 0, 0, 0, 0)),
                   pl.BlockSpec((None, n_hp, T, LANES), lambda b: (b, 0, 0, 0))],
        out_shape=[jax.ShapeDtypeStruct((B, n_hp, 2, DECAY_ROWS, T), BF16),
                   jax.ShapeDtypeStruct((B, n_hp, T, LANES), BF16)],
        compiler_params=_params("parallel"),
        name="decay_cumsum",
    )(logf, tri)

    tq = FOX_Q_TILE
    nq = T // tq
    n_units = 2 * tq // FOX_COLS
    yb = pl.pallas_call(
        _fox_kernel,
        grid=(B, n_hp, nq),
        in_specs=[pl.BlockSpec((None, LANES, tq), lambda b, hp, i: (b, hp, i)),
                  pl.BlockSpec((None, None, 2, DECAY_ROWS, tq), lambda b, hp, i: (b, hp, 0, 0, i)),
                  pl.BlockSpec((T, LANES), lambda b, hp, i: (b, hp)),
                  pl.BlockSpec((None, None, T, LANES), lambda b, hp, i: (b, hp, 0, 0)),
                  pl.BlockSpec((None, LANES, T), lambda b, hp, i: (b, hp, 0))],
        out_specs=pl.BlockSpec((tq, LANES), lambda b, hp, i: (b * nq + i, hp)),
        out_shape=jax.ShapeDtypeStruct((N, FOX_WIDTH), BF16),
        scratch_shapes=[pltpu.VMEM((2, MXU_COLS, tq), BF16),
                        pltpu.VMEM((2, n_units, FOX_K_TILE, FOX_COLS), F32),
                        pltpu.VMEM((n_units, 1, FOX_COLS), F32),
                        pltpu.VMEM((n_units, HEAD_DIM + FOX_DEN_ROWS, FOX_COLS), F32)],
        compiler_params=_params("parallel", "parallel", "arbitrary"),
        name="fox_attention",
    )(qbt, cqt, kb, ck_ext, vbt)

    dil_o, dil_lse = [], []
    for gi, (win, d) in enumerate(DIL_GROUPS):
        assert win // d == DIL_BACK
        n = T // d
        tq = min(DIL_Q_TILE, n)
        nblk = n // tq
        sub = tq // DIL_BACK
        view = lambda a: a.reshape(B * n, d * DIL_WIDTH)
        cur = pl.BlockSpec((tq, DIL_OUT), lambda b, r, i: (b * nblk + i, r * N_DIL_GROUPS + gi))
        prev = pl.BlockSpec((DIL_BACK, DIL_OUT),
                            lambda b, r, i: (b * (n // DIL_BACK) + jnp.maximum(i * sub - 1, 0), r * N_DIL_GROUPS + gi))
        out_spec = pl.BlockSpec((tq, DIL_OUT), lambda b, r, i: (b * nblk + i, r))
        o_g, lse_g = pl.pallas_call(
            _dilated_kernel,
            grid=(B, d, nblk),
            in_specs=[cur, cur, prev, cur, prev],
            out_specs=[out_spec, out_spec],
            out_shape=[jax.ShapeDtypeStruct((B * n, d * DIL_OUT), BF16),
                       jax.ShapeDtypeStruct((B * n, d * DIL_OUT), F32)],
            compiler_params=_params("parallel", "parallel", "arbitrary"),
            name=f"dilated_d{d}",
        )(view(qa), view(ka), view(ka), view(va), view(va))
        dil_o.append(o_g.reshape(N, DIL_OUT))
        dil_lse.append(lse_g.reshape(N, DIL_OUT))

    gkm = g_kM[0].reshape(1, MEM_HEAD_DIM)
    km, vm = pl.pallas_call(
        _memkv_kernel,
        grid=(B,),
        in_specs=[pl.BlockSpec((None, mem_len, D), lambda b: (b, 0, 0)), _full((1, D)),
                  _full((D, 2 * MEM_WIDTH)), _full((1, MEM_HEAD_DIM))],
        out_specs=[pl.BlockSpec((None, mem_len, MEM_WIDTH), lambda b: (b, 0, 0))] * 2,
        out_shape=[jax.ShapeDtypeStruct((B, mem_len, MEM_WIDTH), BF16)] * 2,
        compiler_params=_params("parallel"),
        name="mem_kv",
    )(mem, g_mem, w_mem_kv[0].astype(BF16), gkm)

    mem_spec = pl.BlockSpec((None, mem_len, MEM_WIDTH), lambda i: (i // (T // tm), 0, 0))
    x_mid = pl.pallas_call(
        _merge_kernel,
        grid=(N // tm,),
        in_specs=[row(D), row(D)] + [row(DIL_OUT)] * 6 + [row(FOX_WIDTH), row(MEM_WIDTH), mem_spec, mem_spec,
                  _full((D, 3 * D)), _full((1, 3 * D)), _full((DIL_OUT, D)), _full((FOX_WIDTH, D)),
                  _full((MEM_WIDTH, D)), _full((D, D))],
        out_specs=row(D),
        out_shape=jax.ShapeDtypeStruct((N, D), F32),
        scratch_shapes=[pltpu.VMEM((tm, D), BF16)],
        compiler_params=_params("parallel"),
        name="merge_outproj",
    )(x2d, h, *dil_o, *dil_lse, yb, qm, km, vm, w_gate[0].astype(BF16), b_gate, w_br_a[0].astype(BF16),
      w_br_b[0].astype(BF16), w_br_m[0].astype(BF16), w_out[0].astype(BF16))

    out = pl.pallas_call(
        _mlp_kernel,
        grid=(N // tm,),
        in_specs=[row(D), _full((1, D)), _full((D, D_FF)), _full((D_FF, D))],
        out_specs=row(D),
        out_shape=jax.ShapeDtypeStruct((N, D), F32),
        scratch_shapes=[pltpu.VMEM((tm, D), BF16)],
        compiler_params=_params("parallel"),
        name="mlp",
    )(x_mid, g_mlp, w_up[0].astype(BF16), w_down[0].astype(BF16))
    return out.reshape(B, T, D)
```

```python
import functools

import jax
import jax.numpy as jnp
from jax import lax
from jax.experimental import pallas as pl
from jax.experimental.pallas import tpu as pltpu

D_MODEL = 1024
HEAD_DIM = 64
DIL_GROUPS = ((128, 1), (512, 4), (2048, 16))
N_DIL_GROUPS = 3
DIL_HEADS = 4
DIL_WIDTH = N_DIL_GROUPS * DIL_HEADS * HEAD_DIM
DIL_OUT = DIL_HEADS * HEAD_DIM
DIL_BACK = 128
FOX_HEADS = 8
FOX_WIDTH = FOX_HEADS * HEAD_DIM
MEM_HEADS = 4
MEM_HEAD_DIM = 128
MEM_WIDTH = MEM_HEADS * MEM_HEAD_DIM
ROT_DIM = HEAD_DIM // 4
ROPE_THETA = 500000.0
D_FF = 4 * D_MODEL
EPS = 1e-6

LANES = 128
MXU_COLS = 256
FL_PAD = LANES

ROW_TILE = 512
FOX_Q_TILE = 1024
FOX_K_TILE = 256
FOX_COLS = 512
FOX_DEN_ROWS = 16
DECAY_ROWS = 16
CUMSUM_BLOCK = 256
DIL_TILE = 2048
DIL_MERGE_ROWS = 256
DIL_CHAINS = 4

LOG2E = 1.4426950408889634
BF16 = jnp.bfloat16
F32 = jnp.float32
NT_DIMS = (((1,), (1,)), ((), ()))


def _dot(a, b):
    return jnp.dot(a, b, preferred_element_type=F32)


def _dot_nt(a, b):
    return lax.dot_general(a, b, NT_DIMS, preferred_element_type=F32)


def _row_rmsnorm(x, g):
    ms = jnp.mean(x * x, axis=-1, keepdims=True)
    return x * lax.rsqrt(ms + EPS) * g


def _head_rmsnorm(y, blockdiag, gain, head_dim):
    ss = _dot((y * y).astype(BF16), blockdiag)
    return y * lax.rsqrt(ss * (1.0 / head_dim) + EPS) * gain


def _split3(c):
    hi = c.astype(BF16)
    r1 = c - hi.astype(F32)
    mid = r1.astype(BF16)
    lo = (r1 - mid.astype(F32)).astype(BF16)
    return hi, mid, lo


def _inproj_kernel(x_ref, gmix_ref, w_ref, bd64_ref, bd128_ref, gqa_ref, gka_ref, gqb_ref, gkb_ref,
                   gqm_ref, bf_ref, rc_ref, rs1_ref, rs2_ref,
                   h_ref, qa0_ref, qa1_ref, qa2_ref, ka0_ref, ka1_ref, ka2_ref, va0_ref, va1_ref, va2_ref,
                   qbt_ref, kb_ref, vbt_ref, qm_ref, lf_ref, perm_ref):
    h = _row_rmsnorm(x_ref[...], gmix_ref[...]).astype(BF16)
    h_ref[...] = h
    bd64 = bd64_ref[...]
    bd128 = bd128_ref[...]
    rc = rc_ref[...]
    rs1 = rs1_ref[...]
    rs2 = rs2_ref[...]
    cw = MXU_COLS
    tm = x_ref.shape[0]

    def proj(col, width=cw):
        return _dot(h, w_ref[:, col:col + width])

    def rope(y):
        return y * rc + pltpu.roll(y, cw - ROT_DIM // 2, 1) * rs1 + pltpu.roll(y, ROT_DIM // 2, 1) * rs2

    def store_rows(out_ref, c, y):
        out_ref[:, c:c + cw] = y.astype(out_ref.dtype)

    def store_transposed(out_ref, c, y):
        out_ref[c:c + cw, :] = y.T.astype(out_ref.dtype)

    def store_by_residue(out_ref, y):
        d = out_ref.shape[0]
        if d == 1:
            out_ref[0] = y.astype(out_ref.dtype)
            return
        for half in range(cw // LANES):
            perm_ref[half] = y[:, half * LANES:(half + 1) * LANES]
        for r in range(d):
            out_ref[r] = jnp.concatenate(
                [perm_ref[half, pl.ds(r, tm // d, stride=d), :] for half in range(cw // LANES)],
                axis=1).astype(out_ref.dtype)

    rows_of = lambda ref: [(functools.partial(store_rows, ref, c)) for c in range(0, ref.shape[1], cw)]
    cols_of = lambda ref: [(functools.partial(store_transposed, ref, c)) for c in range(0, ref.shape[0], cw)]
    residues_of = lambda refs: [functools.partial(store_by_residue, ref) for ref in refs]
    col = 0
    for stores, gain_ref, bd, hd, rot in (
            (residues_of((qa0_ref, qa1_ref, qa2_ref)), gqa_ref, bd64, HEAD_DIM, True),
            (residues_of((ka0_ref, ka1_ref, ka2_ref)), gka_ref, bd64, HEAD_DIM, True),
            (residues_of((va0_ref, va1_ref, va2_ref)), None, None, None, False),
            (cols_of(qbt_ref), gqb_ref, bd64, HEAD_DIM, False),
            (rows_of(kb_ref), gkb_ref, bd64, HEAD_DIM, False),
            (cols_of(vbt_ref), None, None, None, False),
            (rows_of(qm_ref), gqm_ref, bd128, MEM_HEAD_DIM, False)):
        for store in stores:
            y = proj(col)
            if gain_ref is not None:
                y = _head_rmsnorm(y, bd, gain_ref[...], hd)
            if rot:
                y = rope(y)
            store(y)
            col += cw
    z = proj(col, FL_PAD) + bf_ref[...]
    lf_ref[...] = jnp.minimum(z, 0.0) - jnp.log1p(jnp.exp(-jnp.abs(z)))


def _cumsum_kernel(lf_ref, tri_ref, cqt_ref, ck_ref):
    tri = tri_ref[...]
    n_blocks = lf_ref.shape[0] // CUMSUM_BLOCK
    lane = lax.broadcasted_iota(jnp.int32, (CUMSUM_BLOCK, LANES), 1)
    row = lax.broadcasted_iota(jnp.int32, (DECAY_ROWS, CUMSUM_BLOCK), 0)
    carry = jnp.zeros((1, LANES), F32)
    for blk in range(n_blocks):
        rows = slice(blk * CUMSUM_BLOCK, (blk + 1) * CUMSUM_BLOCK)
        hi, mid, lo = _split3(lf_ref[rows, :])
        c = _dot(tri, hi) + _dot(tri, mid) + _dot(tri, lo) + carry
        carry = c[CUMSUM_BLOCK - 1:CUMSUM_BLOCK, :]
        c = c * LOG2E
        ct = c.T
        for hp in range(FOX_HEADS // 2):
            ke = jnp.zeros((CUMSUM_BLOCK, LANES), F32)
            for e in range(2):
                head = 2 * hp + e
                pieces = [t.astype(F32) for t in _split3(jnp.broadcast_to(c[:, head:head + 1], (CUMSUM_BLOCK, LANES)))]
                base = DECAY_ROWS * e
                for t in range(3):
                    ke = jnp.where(lane == base + t, 1.0, ke)
                    ke = jnp.where(lane == base + 3 + t, -pieces[t], ke)
                pieces_t = [t.astype(F32) for t in
                            _split3(jnp.broadcast_to(ct[head:head + 1, :], (DECAY_ROWS, CUMSUM_BLOCK)))]
                qe = jnp.where(row < 6, 1.0, 0.0)
                for t in range(3):
                    qe = jnp.where(row == t, pieces_t[t], qe)
                cqt_ref[hp, e, :, rows] = qe.astype(BF16)
            ck_ref[hp, rows, :] = ke.astype(BF16)


def _fox_kernel(qt_ref, cqt_ref, k_ref, ck_ref, vt_ref, o_ref, qa_ref, st_ref, m_ref, acc_ref):
    blk = pl.program_id(2)
    tq = FOX_Q_TILE
    tk = FOX_K_TILE
    hd = HEAD_DIM
    cw = FOX_COLS
    zq = jnp.zeros((hd, tq), BF16)
    zd = jnp.zeros((DECAY_ROWS, tq), BF16)
    zpad = jnp.zeros((MXU_COLS - 2 * hd - 2 * DECAY_ROWS, tq), BF16)
    for e in range(2):
        parts = [zq, zq, zd, zd, zpad]
        parts[e] = qt_ref[e * hd:(e + 1) * hd, :]
        parts[2 + e] = cqt_ref[e]
        qa_ref[e] = jnp.concatenate(parts, axis=0)
    units = [(e, c0) for e in range(2) for c0 in range(0, tq, cw)]

    def key_block(key_start):
        rows = pl.ds(pl.multiple_of(key_start, tk), tk)
        return jnp.concatenate([k_ref[rows, :], ck_ref[rows, :]], axis=1)

    def value_block(key_start):
        return vt_ref[:, pl.ds(pl.multiple_of(key_start, tk), tk)]

    def issue_scores(buf, u, k_aug):
        e, c0 = units[u]
        st_ref[buf, u] = _dot(k_aug, qa_ref[e, :, c0:c0 + cw])

    def update(buf, u, vt, key_off):
        e, c0 = units[u]
        st = st_ref[buf, u]
        if key_off is not None and key_off + tk - 1 > c0:
            key = lax.broadcasted_iota(jnp.int32, st.shape, 0) + key_off
            qry = lax.broadcasted_iota(jnp.int32, st.shape, 1) + c0
            st = jnp.where(key <= qry, st, -jnp.inf)
        m = m_ref[u]
        m_new = jnp.maximum(m, jnp.max(st, axis=0, keepdims=True))
        alpha = jnp.exp2(m - m_new)
        p = jnp.exp2(st - m_new).astype(BF16)
        vt_aug = jnp.concatenate([vt[e * hd:(e + 1) * hd, :], ones_rows], axis=0)
        acc_ref[u] = alpha * acc_ref[u] + _dot(vt_aug, p)
        m_ref[u] = m_new

    ones_rows = jnp.ones((FOX_DEN_ROWS, tk), BF16)
    m_ref[...] = jnp.full(m_ref.shape, -jnp.inf, F32)
    acc_ref[...] = jnp.zeros(acc_ref.shape, F32)
    k_first = key_block(0)
    for u in range(len(units)):
        issue_scores(0, u, k_first)

    def block_pair(jp, carry):
        for half in range(2):
            j = 2 * jp + half
            k_next = key_block((j + 1) * tk)
            vt = value_block(j * tk)
            for u in range(len(units)):
                issue_scores(1 - half, u, k_next)
                update(half, u, vt, None)
        return carry

    assert (tq // tk) % 2 == 0
    lax.fori_loop(0, blk * (tq // tk // 2), block_pair, 0)
    for jj in range(tq // tk):
        key_off = jj * tk
        active = [u for u, (e, c0) in enumerate(units) if c0 + cw > key_off]
        nxt_off = key_off + tk
        nxt_active = [u for u, (e, c0) in enumerate(units) if c0 + cw > nxt_off] if nxt_off < tq else []
        if nxt_active:
            k_next = key_block(blk * tq + nxt_off)
        vt = value_block(blk * tq + key_off)
        for u in active:
            if u in nxt_active:
                issue_scores((jj + 1) % 2, u, k_next)
            update(jj % 2, u, vt, key_off)
    per_head = []
    for e in range(2):
        cols = [acc_ref[u, :hd, :] * (1.0 / acc_ref[u, hd:hd + 1, :]) for u, unit in enumerate(units) if unit[0] == e]
        per_head.append(jnp.concatenate(cols, axis=1))
    ot = jnp.concatenate(per_head, axis=0)
    o_ref[...] = ot.T.astype(o_ref.dtype)


def _dilated_kernel(hm_ref, q0_ref, k0_ref, kp0_ref, v0_ref, vp0_ref, q1_ref, k1_ref, kp1_ref, v1_ref, vp1_ref,
                    q2_ref, k2_ref, kp2_ref, v2_ref, vp2_ref, ya_ref, o_scr, l_scr):
    jt = pl.program_id(1)
    c = DIL_BACK
    w = DIL_OUT
    nh = DIL_HEADS
    tile = ya_ref.shape[0]
    qi = lax.broadcasted_iota(jnp.int32, (nh * c, 2 * c), 0) & (c - 1)
    kj = lax.broadcasted_iota(jnp.int32, (nh * c, 2 * c), 1)
    dist = qi + c - kj
    in_band = (dist >= 0) & (dist <= DIL_BACK)
    in_band_first = in_band & (kj >= jnp.where(jt > 0, 0, c))
    lane = lax.broadcasted_iota(jnp.int32, (c, w), 1)
    in_head = [(lane >= hh * HEAD_DIM) & (lane < (hh + 1) * HEAD_DIM) for hh in range(nh)]

    def attend(gi, d, items, mask):
        scores = [_dot_nt(jnp.concatenate([q * hm_ref[hh] for hh in range(nh)], axis=0), kk)
                  for q, kk, _, _ in items]
        for s, (_, _, vv, tok0) in zip(scores, items):
            s = jnp.where(mask, s, -jnp.inf)
            m = jnp.max(s, axis=-1, keepdims=True)
            p = jnp.exp(s - m)
            den = jnp.sum(p, axis=-1, keepdims=True)
            o4 = _dot(p.astype(BF16), vv) * (1.0 / den)
            lse4 = m + jnp.log(den)
            o = jnp.zeros((c, w), F32)
            lse = jnp.zeros((c, w), F32)
            for hh in range(nh):
                rows = slice(hh * c, (hh + 1) * c)
                o = jnp.where(in_head[hh], o4[rows], o)
                lse = jnp.where(in_head[hh], lse4[rows], lse)
            for half in range(w // LANES):
                lanes = slice(half * LANES, (half + 1) * LANES)
                o_scr[gi, half, pl.ds(tok0, c, stride=d), :] = o[:, lanes]
                l_scr[gi, half, pl.ds(tok0, c, stride=d), :] = lse[:, lanes]

    groups = ((q0_ref, k0_ref, kp0_ref, v0_ref, vp0_ref), (q1_ref, k1_ref, kp1_ref, v1_ref, vp1_ref),
              (q2_ref, k2_ref, kp2_ref, v2_ref, vp2_ref))
    for gi, (q_ref, k_ref, kp_ref, v_ref, vp_ref) in enumerate(groups):
        d = q_ref.shape[0]
        n_sb = q_ref.shape[1] // c
        res_chunk = min(d, DIL_CHAINS)
        assert d % res_chunk == 0

        def first_item(r, q_ref=q_ref, k_ref=k_ref, kp_ref=kp_ref, v_ref=v_ref, vp_ref=vp_ref):
            kk = jnp.concatenate([kp_ref[r], k_ref[r, 0:c]], axis=0)
            vv = jnp.concatenate([vp_ref[r], v_ref[r, 0:c]], axis=0)
            return q_ref[r, 0:c], kk, vv, r

        def later_item(r, sb, d=d, q_ref=q_ref, k_ref=k_ref, v_ref=v_ref):
            start = pl.multiple_of(sb * c, c)
            window = pl.ds(start - c, 2 * c)
            return q_ref[r, pl.ds(start, c)], k_ref[r, window], v_ref[r, window], sb * (c * d) + r

        def first_chunk(it, carry, gi=gi, d=d, res_chunk=res_chunk, first_item=first_item):
            attend(gi, d, [first_item(it * res_chunk + i) for i in range(res_chunk)], in_band_first)
            return carry

        if d == res_chunk:
            first_chunk(0, 0)
        else:
            lax.fori_loop(0, d // res_chunk, first_chunk, 0)
        if n_sb > 1:
            sb_chunk = DIL_CHAINS // res_chunk if d == res_chunk else 1
            sb_chunk = max(k for k in range(1, sb_chunk + 1) if (n_sb - 1) % k == 0)
            assert d == res_chunk

            def later_chunk(it, carry, gi=gi, d=d, sb_chunk=sb_chunk, later_item=later_item):
                attend(gi, d, [later_item(r, 1 + it * sb_chunk + i) for i in range(sb_chunk) for r in range(d)],
                       in_band)
                return carry

            lax.fori_loop(0, (n_sb - 1) // sb_chunk, later_chunk, 0)

    def merge(ch, carry):
        rows = pl.ds(pl.multiple_of(ch * DIL_MERGE_ROWS, DIL_MERGE_ROWS), DIL_MERGE_ROWS)
        for half in range(w // LANES):
            l = [l_scr[g, half, rows, :] for g in range(N_DIL_GROUPS)]
            mx = jnp.maximum(jnp.maximum(l[0], l[1]), l[2])
            e = [jnp.exp(lg - mx) for lg in l]
            num = e[0] * o_scr[0, half, rows, :] + e[1] * o_scr[1, half, rows, :] + e[2] * o_scr[2, half, rows, :]
            ya_ref[rows, half * LANES:(half + 1) * LANES] = (num / (e[0] + e[1] + e[2])).astype(ya_ref.dtype)
        return carry

    lax.fori_loop(0, tile // DIL_MERGE_ROWS, merge, 0)


def _memkv_kernel(mem_ref, g_ref, w_ref, gk_ref, km_ref, vm_ref):
    mn = _row_rmsnorm(mem_ref[...], g_ref[...]).astype(BF16)
    kv = _dot(mn, w_ref[...])
    gk = gk_ref[...]
    for hh in range(MEM_HEADS):
        cols = slice(hh * MEM_HEAD_DIM, (hh + 1) * MEM_HEAD_DIM)
        km_ref[:, cols] = _row_rmsnorm(kv[:, cols], gk).astype(km_ref.dtype)
    vm_ref[...] = kv[:, MEM_WIDTH:].astype(vm_ref.dtype)


def _merge_kernel(x_ref, h_ref, ya_ref, yb_ref, qm_ref, km_ref, vm_ref,
                  wg_ref, bg_ref, wa_ref, wb_ref, wm_ref, wo_ref, out_ref, merged_ref):
    ya = ya_ref[...]
    ym = []
    for hh in range(MEM_HEADS):
        cols = slice(hh * MEM_HEAD_DIM, (hh + 1) * MEM_HEAD_DIM)
        s = _dot_nt(qm_ref[:, cols], km_ref[:, cols])
        p = jnp.exp(s - jnp.max(s, axis=-1, keepdims=True))
        den = jnp.sum(p, axis=-1, keepdims=True)
        ym.append((_dot(p.astype(BF16), vm_ref[:, cols]) / den).astype(BF16))
    ym = jnp.concatenate(ym, axis=1)
    yb = yb_ref[...]
    h = h_ref[...]
    cw = 2 * MXU_COLS
    for c in range(0, D_MODEL, cw):
        merged = None
        for k, (y, w_ref) in enumerate(((ya, wa_ref), (yb, wb_ref), (ym, wm_ref))):
            gcol = k * D_MODEL + c
            gate = jax.nn.sigmoid(_dot(h, wg_ref[:, gcol:gcol + cw]) + bg_ref[:, gcol:gcol + cw])
            term = gate * _dot(y, w_ref[:, c:c + cw])
            merged = term if merged is None else merged + term
        merged_ref[:, c:c + cw] = merged.astype(BF16)
    out_ref[...] = x_ref[...] + _dot(merged_ref[...], wo_ref[...])


def _mlp_kernel(x_ref, g_ref, wu_ref, wd_ref, out_ref, h2_ref):
    x = x_ref[...]
    h2_ref[...] = _row_rmsnorm(x, g_ref[...]).astype(BF16)
    cw = D_MODEL
    acc = x
    for c in range(0, D_FF, cw):
        u = jnp.maximum(_dot(h2_ref[...], wu_ref[:, c:c + cw]), 0.0)
        acc = acc + _dot((u * u).astype(BF16), wd_ref[c:c + cw, :])
    out_ref[...] = acc


def _full(shape):
    return pl.BlockSpec(shape, lambda *_: (0,) * len(shape))


def _params(*sem):
    return pltpu.CompilerParams(dimension_semantics=sem)


def _blockdiag(width, head_dim):
    r = jnp.arange(width) // head_dim
    return (r[:, None] == r[None, :]).astype(BF16)


def _rope_tables(seq, width):
    half = ROT_DIM // 2
    inv_freq = ROPE_THETA ** (-jnp.arange(0, ROT_DIM, 2, dtype=F32) / ROT_DIM)
    ang = jnp.arange(seq, dtype=F32)[:, None] * inv_freq[None, :]
    cos, sin = jnp.cos(ang), jnp.sin(ang)
    ones = jnp.ones((seq, HEAD_DIM - ROT_DIM), F32)
    zeros = jnp.zeros((seq, HEAD_DIM - ROT_DIM), F32)
    zhalf = jnp.zeros((seq, half), F32)
    rc = jnp.concatenate([cos, cos, ones], axis=1)
    rs1 = jnp.concatenate([-sin, zhalf, zeros], axis=1)
    rs2 = jnp.concatenate([zhalf, sin, zeros], axis=1)
    reps = width // HEAD_DIM
    return tuple(jnp.tile(t, (1, reps)) for t in (rc, rs1, rs2))


def kernel(x, mem, g_mix, w_in, b_f, g_qA, g_kA, g_qB, g_kB, g_mem, w_mem_kv, g_qM, g_kM, w_gate, b_gate,
           w_br_a, w_br_b, w_br_m, w_out, g_mlp, w_up, w_down):
    B, T, D = x.shape
    assert D == D_MODEL and w_in.shape[0] == 1, "single-layer kernel"
    N = B * T
    mem_len = mem.shape[1]
    tm = ROW_TILE
    assert T % tm == 0 and T % FOX_Q_TILE == 0 and FOX_Q_TILE % FOX_K_TILE == 0
    x2d = x.reshape(N, D)

    wi = w_in[0]
    offs = [0]
    for wdt in (DIL_WIDTH, DIL_WIDTH, DIL_WIDTH, FOX_WIDTH, FOX_WIDTH, FOX_WIDTH, FOX_HEADS, MEM_WIDTH):
        offs.append(offs[-1] + wdt)
    w_fl = jnp.pad(wi[:, offs[6]:offs[7]], ((0, 0), (0, FL_PAD - FOX_HEADS)))
    w_all = jnp.concatenate([wi[:, :offs[6]], wi[:, offs[7]:], w_fl], axis=1).astype(BF16)
    n_cols = w_all.shape[1]
    bf_pad = jnp.pad(b_f[0], (0, FL_PAD - FOX_HEADS)).reshape(1, FL_PAD)
    cw = MXU_COLS
    q_scale = HEAD_DIM ** -0.5
    gqa = (jnp.tile(g_qA[0], cw // HEAD_DIM) * q_scale).reshape(1, cw)
    gka = jnp.tile(g_kA[0], cw // HEAD_DIM).reshape(1, cw)
    gqb = (jnp.tile(g_qB[0], cw // HEAD_DIM) * (q_scale * LOG2E)).reshape(1, cw)
    gkb = jnp.tile(g_kB[0], cw // HEAD_DIM).reshape(1, cw)
    gqm = (jnp.tile(g_qM[0], cw // MEM_HEAD_DIM) * MEM_HEAD_DIM ** -0.5).reshape(1, cw)
    rc, rs1, rs2 = _rope_tables(T, cw)
    bd64 = _blockdiag(cw, HEAD_DIM)
    bd128 = _blockdiag(cw, MEM_HEAD_DIM)

    row = lambda width: pl.BlockSpec((tm, width), lambda i: (i, 0))
    rope_spec = pl.BlockSpec((tm, cw), lambda i: (i % (T // tm), 0))
    tspec = pl.BlockSpec((None, FOX_WIDTH, tm), lambda i: (i // (T // tm), 0, i % (T // tm)))
    tshape = jax.ShapeDtypeStruct((B, FOX_WIDTH, T), BF16)
    rshape = lambda width: jax.ShapeDtypeStruct((N, width), BF16)
    dils = [d for _, d in DIL_GROUPS]
    assert all(win // d == DIL_BACK and tm % d == 0 and (tm // d) % 16 == 0 for win, d in DIL_GROUPS)
    dspecs = [pl.BlockSpec((None, d, tm // d, DIL_OUT), lambda i: (i // (T // tm), 0, i % (T // tm), 0))
              for d in dils]
    dshapes = [jax.ShapeDtypeStruct((B, d, T // d, DIL_OUT), BF16) for d in dils]
    outs = pl.pallas_call(
        _inproj_kernel,
        grid=(N // tm,),
        in_specs=[row(D), _full((1, D)), _full((D, n_cols)), _full((cw, cw)), _full((cw, cw)),
                  _full((1, cw)), _full((1, cw)), _full((1, cw)), _full((1, cw)), _full((1, cw)),
                  _full((1, FL_PAD)), rope_spec, rope_spec, rope_spec],
        out_specs=[row(D)] + dspecs * 3 + [tspec, row(FOX_WIDTH), tspec, row(MEM_WIDTH), row(FL_PAD)],
        out_shape=[rshape(D)] + dshapes * 3 + [tshape, rshape(FOX_WIDTH), tshape, rshape(MEM_WIDTH),
                                               jax.ShapeDtypeStruct((N, FL_PAD), F32)],
        scratch_shapes=[pltpu.VMEM((cw // LANES, tm, LANES), F32)],
        compiler_params=_params("parallel"),
        name="inproj",
    )(x2d, g_mix, w_all, bd64, bd128, gqa, gka, gqb, gkb, gqm, bf_pad, rc, rs1, rs2)
    h, qa_g, ka_g, va_g = outs[0], outs[1:4], outs[4:7], outs[7:10]
    qbt, kb, vbt, qm, logf = outs[10:]

    n_hp = FOX_HEADS // 2
    tri = (jnp.arange(CUMSUM_BLOCK)[:, None] >= jnp.arange(CUMSUM_BLOCK)[None, :]).astype(BF16)
    cqt, ck_ext = pl.pallas_call(
        _cumsum_kernel,
        grid=(B,),
        in_specs=[pl.BlockSpec((T, FL_PAD), lambda b: (b, 0)), _full((CUMSUM_BLOCK, CUMSUM_BLOCK))],
        out_specs=[pl.BlockSpec((None, n_hp, 2, DECAY_ROWS, T), lambda b: (b, 0, 0, 0, 0)),
                   pl.BlockSpec((None, n_hp, T, LANES), lambda b: (b, 0, 0, 0))],
        out_shape=[jax.ShapeDtypeStruct((B, n_hp, 2, DECAY_ROWS, T), BF16),
                   jax.ShapeDtypeStruct((B, n_hp, T, LANES), BF16)],
        compiler_params=_params("parallel"),
        name="decay_cumsum",
    )(logf, tri)

    tq = FOX_Q_TILE
    nq = T // tq
    n_units = 2 * tq // FOX_COLS
    yb = pl.pallas_call(
        _fox_kernel,
        grid=(B, n_hp, nq),
        in_specs=[pl.BlockSpec((None, LANES, tq), lambda b, hp, i: (b, hp, i)),
                  pl.BlockSpec((None, None, 2, DECAY_ROWS, tq), lambda b, hp, i: (b, hp, 0, 0, i)),
                  pl.BlockSpec((T, LANES), lambda b, hp, i: (b, hp)),
                  pl.BlockSpec((None, None, T, LANES), lambda b, hp, i: (b, hp, 0, 0)),
                  pl.BlockSpec((None, LANES, T), lambda b, hp, i: (b, hp, 0))],
        out_specs=pl.BlockSpec((tq, LANES), lambda b, hp, i: (b * nq + i, hp)),
        out_shape=jax.ShapeDtypeStruct((N, FOX_WIDTH), BF16),
        scratch_shapes=[pltpu.VMEM((2, MXU_COLS, tq), BF16),
                        pltpu.VMEM((2, n_units, FOX_K_TILE, FOX_COLS), F32),
                        pltpu.VMEM((n_units, 1, FOX_COLS), F32),
                        pltpu.VMEM((n_units, HEAD_DIM + FOX_DEN_ROWS, FOX_COLS), F32)],
        compiler_params=_params("parallel", "parallel", "arbitrary"),
        name="fox_attention",
    )(qbt, cqt, kb, ck_ext, vbt)

    tile = DIL_TILE
    assert T % tile == 0 and all((tile // d) % DIL_BACK == 0 for d in dils)
    lane_head = jnp.arange(DIL_OUT) // HEAD_DIM
    head_mask = jnp.broadcast_to((lane_head[None, :] == jnp.arange(DIL_HEADS)[:, None])[:, None, :],
                                 (DIL_HEADS, DIL_BACK, DIL_OUT)).astype(BF16)
    dil_specs, dil_args = [], []
    for gi, d in enumerate(dils):
        per_tile = tile // d // DIL_BACK
        cur = pl.BlockSpec((None, d, tile // d, DIL_OUT), lambda b, j: (b, 0, j, 0))
        prev = pl.BlockSpec((None, d, DIL_BACK, DIL_OUT),
                            lambda b, j, per_tile=per_tile: (b, 0, jnp.maximum(j * per_tile - 1, 0), 0))
        dil_specs += [cur, cur, prev, cur, prev]
        dil_args += [qa_g[gi], ka_g[gi], ka_g[gi], va_g[gi], va_g[gi]]
    ya = pl.pallas_call(
        _dilated_kernel,
        grid=(B, T // tile),
        in_specs=[_full((DIL_HEADS, DIL_BACK, DIL_OUT))] + dil_specs,
        out_specs=pl.BlockSpec((tile, DIL_OUT), lambda b, j: (b * (T // tile) + j, 0)),
        out_shape=jax.ShapeDtypeStruct((N, DIL_OUT), BF16),
        scratch_shapes=[pltpu.VMEM((N_DIL_GROUPS, DIL_OUT // LANES, tile, LANES), F32)] * 2,
        compiler_params=_params("parallel", "arbitrary"),
        name="dilated_attention",
    )(head_mask, *dil_args)

    gkm = g_kM[0].reshape(1, MEM_HEAD_DIM)
    km, vm = pl.pallas_call(
        _memkv_kernel,
        grid=(B,),
        in_specs=[pl.BlockSpec((None, mem_len, D), lambda b: (b, 0, 0)), _full((1, D)),
                  _full((D, 2 * MEM_WIDTH)), _full((1, MEM_HEAD_DIM))],
        out_specs=[pl.BlockSpec((None, mem_len, MEM_WIDTH), lambda b: (b, 0, 0))] * 2,
        out_shape=[jax.ShapeDtypeStruct((B, mem_len, MEM_WIDTH), BF16)] * 2,
        compiler_params=_params("parallel"),
        name="mem_kv",
    )(mem, g_mem, w_mem_kv[0].astype(BF16), gkm)

    mem_spec = pl.BlockSpec((None, mem_len, MEM_WIDTH), lambda i: (i // (T // tm), 0, 0))
    x_mid = pl.pallas_call(
        _merge_kernel,
        grid=(N // tm,),
        in_specs=[row(D), row(D), row(DIL_OUT), row(FOX_WIDTH), row(MEM_WIDTH), mem_spec, mem_spec,
                  _full((D, 3 * D)), _full((1, 3 * D)), _full((DIL_OUT, D)), _full((FOX_WIDTH, D)),
                  _full((MEM_WIDTH, D)), _full((D, D))],
        out_specs=row(D),
        out_shape=jax.ShapeDtypeStruct((N, D), F32),
        scratch_shapes=[pltpu.VMEM((tm, D), BF16)],
        compiler_params=_params("parallel"),
        name="merge_outproj",
    )(x2d, h, ya, yb, qm, km, vm, w_gate[0].astype(BF16), b_gate, w_br_a[0].astype(BF16),
      w_br_b[0].astype(BF16), w_br_m[0].astype(BF16), w_out[0].astype(BF16))

    out = pl.pallas_call(
        _mlp_kernel,
        grid=(N // tm,),
        in_specs=[row(D), _full((1, D)), _full((D, D_FF)), _full((D_FF, D))],
        out_specs=row(D),
        out_shape=jax.ShapeDtypeStruct((N, D), F32),
        scratch_shapes=[pltpu.VMEM((tm, D), BF16)],
        compiler_params=_params("parallel"),
        name="mlp",
    )(x_mid, g_mlp, w_up[0].astype(BF16), w_down[0].astype(BF16))
    return out.reshape(B, T, D)
```

```python
import functools

import jax
import jax.numpy as jnp
from jax import lax
from jax.experimental import pallas as pl
from jax.experimental.pallas import tpu as pltpu

D_MODEL = 1024
HEAD_DIM = 64
DIL_GROUPS = ((128, 1), (512, 4), (2048, 16))
N_DIL_GROUPS = 3
DIL_HEADS = 4
DIL_WIDTH = N_DIL_GROUPS * DIL_HEADS * HEAD_DIM
DIL_OUT = DIL_HEADS * HEAD_DIM
DIL_BACK = 128
FOX_HEADS = 8
FOX_WIDTH = FOX_HEADS * HEAD_DIM
MEM_HEADS = 4
MEM_HEAD_DIM = 128
MEM_WIDTH = MEM_HEADS * MEM_HEAD_DIM
ROT_DIM = HEAD_DIM // 4
ROPE_THETA = 500000.0
D_FF = 4 * D_MODEL
EPS = 1e-6

LANES = 128
MXU_COLS = 256
FL_PAD = LANES

ROW_TILE = 512
FOX_Q_TILE = 1024
FOX_K_TILE = 256
FOX_COLS = 512
FOX_DEN_ROWS = 16
FOX_LOOKAHEAD = 1
FOX_BLOCKS_PER_TRIP = 4
DECAY_ROWS = 16
CUMSUM_BLOCK = 256
DIL_TILE = 2048
DIL_MERGE_ROWS = 256
DIL_CHAINS = 4

LOG2E = 1.4426950408889634
BF16 = jnp.bfloat16
F32 = jnp.float32
NT_DIMS = (((1,), (1,)), ((), ()))


def _dot(a, b):
    return jnp.dot(a, b, preferred_element_type=F32)


def _dot_nt(a, b):
    return lax.dot_general(a, b, NT_DIMS, preferred_element_type=F32)


def _row_rmsnorm(x, g):
    ms = jnp.mean(x * x, axis=-1, keepdims=True)
    return x * lax.rsqrt(ms + EPS) * g


def _head_rmsnorm(y, blockdiag, gain, head_dim):
    ss = _dot((y * y).astype(BF16), blockdiag)
    return y * lax.rsqrt(ss * (1.0 / head_dim) + EPS) * gain


def _split3(c):
    hi = c.astype(BF16)
    r1 = c - hi.astype(F32)
    mid = r1.astype(BF16)
    lo = (r1 - mid.astype(F32)).astype(BF16)
    return hi, mid, lo


def _inproj_kernel(x_ref, gmix_ref, w_ref, bd64_ref, bd128_ref, gqa_ref, gka_ref, gqb_ref, gkb_ref,
                   gqm_ref, bf_ref, rc_ref, rs1_ref, rs2_ref,
                   h_ref, qa0_ref, qa1_ref, qa2_ref, ka0_ref, ka1_ref, ka2_ref, va0_ref, va1_ref, va2_ref,
                   qbt_ref, kb_ref, vbt_ref, qm_ref, lf_ref, perm_ref):
    h = _row_rmsnorm(x_ref[...], gmix_ref[...]).astype(BF16)
    h_ref[...] = h
    bd64 = bd64_ref[...]
    bd128 = bd128_ref[...]
    rc = rc_ref[...]
    rs1 = rs1_ref[...]
    rs2 = rs2_ref[...]
    cw = MXU_COLS
    tm = x_ref.shape[0]

    def proj(col, width=cw):
        return _dot(h, w_ref[:, col:col + width])

    def rope(y):
        return y * rc + pltpu.roll(y, cw - ROT_DIM // 2, 1) * rs1 + pltpu.roll(y, ROT_DIM // 2, 1) * rs2

    def store_rows(out_ref, c, y):
        out_ref[:, c:c + cw] = y.astype(out_ref.dtype)

    def store_transposed(out_ref, c, y):
        out_ref[c:c + cw, :] = y.T.astype(out_ref.dtype)

    def store_by_residue(out_ref, y):
        d = out_ref.shape[0]
        if d == 1:
            out_ref[0] = y.astype(out_ref.dtype)
            return
        for half in range(cw // LANES):
            perm_ref[half] = y[:, half * LANES:(half + 1) * LANES]
        for r in range(d):
            out_ref[r] = jnp.concatenate(
                [perm_ref[half, pl.ds(r, tm // d, stride=d), :] for half in range(cw // LANES)],
                axis=1).astype(out_ref.dtype)

    rows_of = lambda ref: [(functools.partial(store_rows, ref, c)) for c in range(0, ref.shape[1], cw)]
    cols_of = lambda ref: [(functools.partial(store_transposed, ref, c)) for c in range(0, ref.shape[0], cw)]
    residues_of = lambda refs: [functools.partial(store_by_residue, ref) for ref in refs]
    chunks = [(store, gain_ref, bd, hd, rot)
              for stores, gain_ref, bd, hd, rot in (
                  (residues_of((qa0_ref, qa1_ref, qa2_ref)), gqa_ref, bd64, HEAD_DIM, True),
                  (residues_of((ka0_ref, ka1_ref, ka2_ref)), gka_ref, bd64, HEAD_DIM, True),
                  (residues_of((va0_ref, va1_ref, va2_ref)), None, None, None, False),
                  (cols_of(qbt_ref), gqb_ref, bd64, HEAD_DIM, False),
                  (rows_of(kb_ref), gkb_ref, bd64, HEAD_DIM, False),
                  (cols_of(vbt_ref), None, None, None, False),
                  (rows_of(qm_ref), gqm_ref, bd128, MEM_HEAD_DIM, False))
              for store in stores]
    y_next = proj(0)
    for idx, (store, gain_ref, bd, hd, rot) in enumerate(chunks):
        y = y_next
        y_next = proj((idx + 1) * cw) if idx + 1 < len(chunks) else None
        if gain_ref is not None:
            y = _head_rmsnorm(y, bd, gain_ref[...], hd)
        if rot:
            y = rope(y)
        store(y)
    col = len(chunks) * cw
    z = proj(col, FL_PAD) + bf_ref[...]
    lf_ref[...] = jnp.minimum(z, 0.0) - jnp.log1p(jnp.exp(-jnp.abs(z)))


def _cumsum_kernel(lf_ref, tri_ref, cqt_ref, ck_ref):
    tri = tri_ref[...]
    n_blocks = lf_ref.shape[0] // CUMSUM_BLOCK
    lane = lax.broadcasted_iota(jnp.int32, (CUMSUM_BLOCK, LANES), 1)
    row = lax.broadcasted_iota(jnp.int32, (DECAY_ROWS, CUMSUM_BLOCK), 0)
    carry = jnp.zeros((1, LANES), F32)
    for blk in range(n_blocks):
        rows = slice(blk * CUMSUM_BLOCK, (blk + 1) * CUMSUM_BLOCK)
        hi, mid, lo = _split3(lf_ref[rows, :])
        c = _dot(tri, hi) + _dot(tri, mid) + _dot(tri, lo) + carry
        carry = c[CUMSUM_BLOCK - 1:CUMSUM_BLOCK, :]
        c = c * LOG2E
        ct = c.T
        for hp in range(FOX_HEADS // 2):
            ke = jnp.zeros((CUMSUM_BLOCK, LANES), F32)
            for e in range(2):
                head = 2 * hp + e
                pieces = [t.astype(F32) for t in _split3(jnp.broadcast_to(c[:, head:head + 1], (CUMSUM_BLOCK, LANES)))]
                base = DECAY_ROWS * e
                for t in range(3):
                    ke = jnp.where(lane == base + t, 1.0, ke)
                    ke = jnp.where(lane == base + 3 + t, -pieces[t], ke)
                pieces_t = [t.astype(F32) for t in
                            _split3(jnp.broadcast_to(ct[head:head + 1, :], (DECAY_ROWS, CUMSUM_BLOCK)))]
                qe = jnp.where(row < 6, 1.0, 0.0)
                for t in range(3):
                    qe = jnp.where(row == t, pieces_t[t], qe)
                cqt_ref[hp, e, :, rows] = qe.astype(BF16)
            ck_ref[hp, rows, :] = ke.astype(BF16)


def _fox_kernel(qt_ref, cqt_ref, k_ref, ck_ref, vt_ref, o_ref, qa_ref, st_ref, m_ref, acc_ref):
    blk = pl.program_id(2)
    tq = FOX_Q_TILE
    tk = FOX_K_TILE
    hd = HEAD_DIM
    cw = FOX_COLS
    zq = jnp.zeros((hd, tq), BF16)
    zd = jnp.zeros((DECAY_ROWS, tq), BF16)
    zpad = jnp.zeros((MXU_COLS - 2 * hd - 2 * DECAY_ROWS, tq), BF16)
    for e in range(2):
        parts = [zq, zq, zd, zd, zpad]
        parts[e] = qt_ref[e * hd:(e + 1) * hd, :]
        parts[2 + e] = cqt_ref[e]
        qa_ref[e] = jnp.concatenate(parts, axis=0)
    units = [(e, c0) for e in range(2) for c0 in range(0, tq, cw)]

    def key_block(key_start):
        rows = pl.ds(pl.multiple_of(key_start, tk), tk)
        return jnp.concatenate([k_ref[rows, :], ck_ref[rows, :]], axis=1)

    def value_block(key_start):
        return vt_ref[:, pl.ds(pl.multiple_of(key_start, tk), tk)]

    def issue_scores(buf, u, k_aug):
        e, c0 = units[u]
        st_ref[buf, u] = _dot(k_aug, qa_ref[e, :, c0:c0 + cw])

    def update(buf, u, vt, key_off):
        e, c0 = units[u]
        st = st_ref[buf, u]
        if key_off is not None and key_off + tk - 1 > c0:
            key = lax.broadcasted_iota(jnp.int32, st.shape, 0) + key_off
            qry = lax.broadcasted_iota(jnp.int32, st.shape, 1) + c0
            st = jnp.where(key <= qry, st, -jnp.inf)
        m = m_ref[u]
        m_new = jnp.maximum(m, jnp.max(st, axis=0, keepdims=True))
        alpha = jnp.exp2(m - m_new)
        p = jnp.exp2(st - m_new).astype(BF16)
        vt_aug = jnp.concatenate([vt[e * hd:(e + 1) * hd, :], ones_rows], axis=0)
        acc_ref[u] = alpha * acc_ref[u] + _dot(vt_aug, p)
        m_ref[u] = m_new

    ones_rows = jnp.ones((FOX_DEN_ROWS, tk), BF16)
    m_ref[...] = jnp.full(m_ref.shape, -jnp.inf, F32)
    acc_ref[...] = jnp.zeros(acc_ref.shape, F32)
    k_first = key_block(0)
    for u in range(len(units)):
        issue_scores(0, u, k_first)

    per_trip = FOX_BLOCKS_PER_TRIP
    assert per_trip % 2 == 0 and (tq // tk) % per_trip == 0

    def block_group(jg, carry):
        for step in range(per_trip):
            j = per_trip * jg + step
            k_next = key_block((j + 1) * tk)
            vt = value_block(j * tk)
            for u in range(min(FOX_LOOKAHEAD, len(units))):
                issue_scores(1 - step % 2, u, k_next)
            for u in range(len(units)):
                if u + FOX_LOOKAHEAD < len(units):
                    issue_scores(1 - step % 2, u + FOX_LOOKAHEAD, k_next)
                update(step % 2, u, vt, None)
        return carry

    lax.fori_loop(0, blk * (tq // tk // per_trip), block_group, 0)
    for jj in range(tq // tk):
        key_off = jj * tk
        active = [u for u, (e, c0) in enumerate(units) if c0 + cw > key_off]
        nxt_off = key_off + tk
        nxt_active = [u for u, (e, c0) in enumerate(units) if c0 + cw > nxt_off] if nxt_off < tq else []
        if nxt_active:
            k_next = key_block(blk * tq + nxt_off)
        vt = value_block(blk * tq + key_off)
        for u in active:
            if u in nxt_active:
                issue_scores((jj + 1) % 2, u, k_next)
            update(jj % 2, u, vt, key_off)
    per_head = []
    for e in range(2):
        cols = [acc_ref[u, :hd, :] * (1.0 / acc_ref[u, hd:hd + 1, :]) for u, unit in enumerate(units) if unit[0] == e]
        per_head.append(jnp.concatenate(cols, axis=1))
    ot = jnp.concatenate(per_head, axis=0)
    o_ref[...] = ot.T.astype(o_ref.dtype)


def _dilated_kernel(hm_ref, q0_ref, k0_ref, kp0_ref, v0_ref, vp0_ref, q1_ref, k1_ref, kp1_ref, v1_ref, vp1_ref,
                    q2_ref, k2_ref, kp2_ref, v2_ref, vp2_ref, ya_ref, o_scr, l_scr):
    jt = pl.program_id(1)
    c = DIL_BACK
    w = DIL_OUT
    nh = DIL_HEADS
    tile = ya_ref.shape[0]
    qi = lax.broadcasted_iota(jnp.int32, (nh * c, 2 * c), 0) & (c - 1)
    kj = lax.broadcasted_iota(jnp.int32, (nh * c, 2 * c), 1)
    dist = qi + c - kj
    in_band = (dist >= 0) & (dist <= DIL_BACK)
    in_band_first = in_band & (kj >= jnp.where(jt > 0, 0, c))
    lane = lax.broadcasted_iota(jnp.int32, (c, w), 1)
    in_head = [(lane >= hh * HEAD_DIM) & (lane < (hh + 1) * HEAD_DIM) for hh in range(nh)]

    def attend(gi, d, items, mask):
        scores = [_dot_nt(jnp.concatenate([q * hm_ref[hh] for hh in range(nh)], axis=0), kk)
                  for q, kk, _, _ in items]
        for s, (_, _, vv, tok0) in zip(scores, items):
            s = jnp.where(mask, s, -jnp.inf)
            m = jnp.max(s, axis=-1, keepdims=True)
            p = jnp.exp(s - m)
            den = jnp.sum(p, axis=-1, keepdims=True)
            o4 = _dot(p.astype(BF16), vv) * (1.0 / den)
            lse4 = m + jnp.log(den)
            o = jnp.zeros((c, w), F32)
            lse = jnp.zeros((c, w), F32)
            for hh in range(nh):
                rows = slice(hh * c, (hh + 1) * c)
                o = jnp.where(in_head[hh], o4[rows], o)
                lse = jnp.where(in_head[hh], lse4[rows], lse)
            for half in range(w // LANES):
                lanes = slice(half * LANES, (half + 1) * LANES)
                o_scr[gi, half, pl.ds(tok0, c, stride=d), :] = o[:, lanes]
                l_scr[gi, half, pl.ds(tok0, c, stride=d), :] = lse[:, lanes]

    groups = ((q0_ref, k0_ref, kp0_ref, v0_ref, vp0_ref), (q1_ref, k1_ref, kp1_ref, v1_ref, vp1_ref),
              (q2_ref, k2_ref, kp2_ref, v2_ref, vp2_ref))
    for gi, (q_ref, k_ref, kp_ref, v_ref, vp_ref) in enumerate(groups):
        d = q_ref.shape[0]
        n_sb = q_ref.shape[1] // c
        res_chunk = min(d, DIL_CHAINS)
        assert d % res_chunk == 0

        def first_item(r, q_ref=q_ref, k_ref=k_ref, kp_ref=kp_ref, v_ref=v_ref, vp_ref=vp_ref):
            kk = jnp.concatenate([kp_ref[r], k_ref[r, 0:c]], axis=0)
            vv = jnp.concatenate([vp_ref[r], v_ref[r, 0:c]], axis=0)
            return q_ref[r, 0:c], kk, vv, r

        def later_item(r, sb, d=d, q_ref=q_ref, k_ref=k_ref, v_ref=v_ref):
            start = pl.multiple_of(sb * c, c)
            window = pl.ds(start - c, 2 * c)
            return q_ref[r, pl.ds(start, c)], k_ref[r, window], v_ref[r, window], sb * (c * d) + r

        def first_chunk(it, carry, gi=gi, d=d, res_chunk=res_chunk, first_item=first_item):
            attend(gi, d, [first_item(it * res_chunk + i) for i in range(res_chunk)], in_band_first)
            return carry

        if d == res_chunk:
            first_chunk(0, 0)
        else:
            lax.fori_loop(0, d // res_chunk, first_chunk, 0)
        if n_sb > 1:
            sb_chunk = DIL_CHAINS // res_chunk if d == res_chunk else 1
            sb_chunk = max(k for k in range(1, sb_chunk + 1) if (n_sb - 1) % k == 0)
            assert d == res_chunk

            def later_chunk(it, carry, gi=gi, d=d, sb_chunk=sb_chunk, later_item=later_item):
                attend(gi, d, [later_item(r, 1 + it * sb_chunk + i) for i in range(sb_chunk) for r in range(d)],
                       in_band)
                return carry

            lax.fori_loop(0, (n_sb - 1) // sb_chunk, later_chunk, 0)

    def merge(ch, carry):
        rows = pl.ds(pl.multiple_of(ch * DIL_MERGE_ROWS, DIL_MERGE_ROWS), DIL_MERGE_ROWS)
        for half in range(w // LANES):
            l = [l_scr[g, half, rows, :] for g in range(N_DIL_GROUPS)]
            mx = jnp.maximum(jnp.maximum(l[0], l[1]), l[2])
            e = [jnp.exp(lg - mx) for lg in l]
            num = e[0] * o_scr[0, half, rows, :] + e[1] * o_scr[1, half, rows, :] + e[2] * o_scr[2, half, rows, :]
            ya_ref[rows, half * LANES:(half + 1) * LANES] = (num / (e[0] + e[1] + e[2])).astype(ya_ref.dtype)
        return carry

    lax.fori_loop(0, tile // DIL_MERGE_ROWS, merge, 0)


def _memkv_kernel(mem_ref, g_ref, w_ref, gk_ref, km_ref, vm_ref):
    mn = _row_rmsnorm(mem_ref[...], g_ref[...]).astype(BF16)
    kv = _dot(mn, w_ref[...])
    gk = gk_ref[...]
    for hh in range(MEM_HEADS):
        cols = slice(hh * MEM_HEAD_DIM, (hh + 1) * MEM_HEAD_DIM)
        km_ref[:, cols] = _row_rmsnorm(kv[:, cols], gk).astype(km_ref.dtype)
    vm_ref[...] = kv[:, MEM_WIDTH:].astype(vm_ref.dtype)


def _merge_kernel(x_ref, h_ref, ya_ref, yb_ref, qm_ref, km_ref, vm_ref,
                  wg_ref, bg_ref, wa_ref, wb_ref, wm_ref, wo_ref, out_ref, merged_ref):
    ya = ya_ref[...]
    ym = []
    for hh in range(MEM_HEADS):
        cols = slice(hh * MEM_HEAD_DIM, (hh + 1) * MEM_HEAD_DIM)
        s = _dot_nt(qm_ref[:, cols], km_ref[:, cols])
        p = jnp.exp(s - jnp.max(s, axis=-1, keepdims=True))
        den = jnp.sum(p, axis=-1, keepdims=True)
        ym.append((_dot(p.astype(BF16), vm_ref[:, cols]) / den).astype(BF16))
    ym = jnp.concatenate(ym, axis=1)
    yb = yb_ref[...]
    h = h_ref[...]
    cw = 2 * MXU_COLS
    for c in range(0, D_MODEL, cw):
        merged = None
        for k, (y, w_ref) in enumerate(((ya, wa_ref), (yb, wb_ref), (ym, wm_ref))):
            gcol = k * D_MODEL + c
            gate = jax.nn.sigmoid(_dot(h, wg_ref[:, gcol:gcol + cw]) + bg_ref[:, gcol:gcol + cw])
            term = gate * _dot(y, w_ref[:, c:c + cw])
            merged = term if merged is None else merged + term
        merged_ref[:, c:c + cw] = merged.astype(BF16)
    out_ref[...] = x_ref[...] + _dot(merged_ref[...], wo_ref[...])


def _mlp_kernel(x_ref, g_ref, wu_ref, wd_ref, out_ref, h2_ref):
    x = x_ref[...]
    h2_ref[...] = _row_rmsnorm(x, g_ref[...]).astype(BF16)
    cw = D_MODEL
    acc = x
    for c in range(0, D_FF, cw):
        u = jnp.maximum(_dot(h2_ref[...], wu_ref[:, c:c + cw]), 0.0)
        acc = acc + _dot((u * u).astype(BF16), wd_ref[c:c + cw, :])
    out_ref[...] = acc


def _full(shape):
    return pl.BlockSpec(shape, lambda *_: (0,) * len(shape))


def _params(*sem):
    return pltpu.CompilerParams(dimension_semantics=sem)


def _blockdiag(width, head_dim):
    r = jnp.arange(width) // head_dim
    return (r[:, None] == r[None, :]).astype(BF16)


def _rope_tables(seq, width):
    half = ROT_DIM // 2
    inv_freq = ROPE_THETA ** (-jnp.arange(0, ROT_DIM, 2, dtype=F32) / ROT_DIM)
    ang = jnp.arange(seq, dtype=F32)[:, None] * inv_freq[None, :]
    cos, sin = jnp.cos(ang), jnp.sin(ang)
    ones = jnp.ones((seq, HEAD_DIM - ROT_DIM), F32)
    zeros = jnp.zeros((seq, HEAD_DIM - ROT_DIM), F32)
    zhalf = jnp.zeros((seq, half), F32)
    rc = jnp.concatenate([cos, cos, ones], axis=1)
    rs1 = jnp.concatenate([-sin, zhalf, zeros], axis=1)
    rs2 = jnp.concatenate([zhalf, sin, zeros], axis=1)
    reps = width // HEAD_DIM
    return tuple(jnp.tile(t, (1, reps)) for t in (rc, rs1, rs2))


def kernel(x, mem, g_mix, w_in, b_f, g_qA, g_kA, g_qB, g_kB, g_mem, w_mem_kv, g_qM, g_kM, w_gate, b_gate,
           w_br_a, w_br_b, w_br_m, w_out, g_mlp, w_up, w_down):
    B, T, D = x.shape
    assert D == D_MODEL and w_in.shape[0] == 1, "single-layer kernel"
    N = B * T
    mem_len = mem.shape[1]
    tm = ROW_TILE
    assert T % tm == 0 and T % FOX_Q_TILE == 0 and FOX_Q_TILE % FOX_K_TILE == 0
    x2d = x.reshape(N, D)

    wi = w_in[0]
    offs = [0]
    for wdt in (DIL_WIDTH, DIL_WIDTH, DIL_WIDTH, FOX_WIDTH, FOX_WIDTH, FOX_WIDTH, FOX_HEADS, MEM_WIDTH):
        offs.append(offs[-1] + wdt)
    w_fl = jnp.pad(wi[:, offs[6]:offs[7]], ((0, 0), (0, FL_PAD - FOX_HEADS)))
    w_all = jnp.concatenate([wi[:, :offs[6]], wi[:, offs[7]:], w_fl], axis=1).astype(BF16)
    n_cols = w_all.shape[1]
    bf_pad = jnp.pad(b_f[0], (0, FL_PAD - FOX_HEADS)).reshape(1, FL_PAD)
    cw = MXU_COLS
    q_scale = HEAD_DIM ** -0.5
    gqa = (jnp.tile(g_qA[0], cw // HEAD_DIM) * q_scale).reshape(1, cw)
    gka = jnp.tile(g_kA[0], cw // HEAD_DIM).reshape(1, cw)
    gqb = (jnp.tile(g_qB[0], cw // HEAD_DIM) * (q_scale * LOG2E)).reshape(1, cw)
    gkb = jnp.tile(g_kB[0], cw // HEAD_DIM).reshape(1, cw)
    gqm = (jnp.tile(g_qM[0], cw // MEM_HEAD_DIM) * MEM_HEAD_DIM ** -0.5).reshape(1, cw)
    rc, rs1, rs2 = _rope_tables(T, cw)
    bd64 = _blockdiag(cw, HEAD_DIM)
    bd128 = _blockdiag(cw, MEM_HEAD_DIM)

    row = lambda width: pl.BlockSpec((tm, width), lambda i: (i, 0))
    rope_spec = pl.BlockSpec((tm, cw), lambda i: (i % (T // tm), 0))
    tspec = pl.BlockSpec((None, FOX_WIDTH, tm), lambda i: (i // (T // tm), 0, i % (T // tm)))
    tshape = jax.ShapeDtypeStruct((B, FOX_WIDTH, T), BF16)
    rshape = lambda width: jax.ShapeDtypeStruct((N, width), BF16)
    dils = [d for _, d in DIL_GROUPS]
    assert all(win // d == DIL_BACK and tm % d == 0 and (tm // d) % 16 == 0 for win, d in DIL_GROUPS)
    dspecs = [pl.BlockSpec((None, d, tm // d, DIL_OUT), lambda i: (i // (T // tm), 0, i % (T // tm), 0))
              for d in dils]
    dshapes = [jax.ShapeDtypeStruct((B, d, T // d, DIL_OUT), BF16) for d in dils]
    outs = pl.pallas_call(
        _inproj_kernel,
        grid=(N // tm,),
        in_specs=[row(D), _full((1, D)), _full((D, n_cols)), _full((cw, cw)), _full((cw, cw)),
                  _full((1, cw)), _full((1, cw)), _full((1, cw)), _full((1, cw)), _full((1, cw)),
                  _full((1, FL_PAD)), rope_spec, rope_spec, rope_spec],
        out_specs=[row(D)] + dspecs * 3 + [tspec, row(FOX_WIDTH), tspec, row(MEM_WIDTH), row(FL_PAD)],
        out_shape=[rshape(D)] + dshapes * 3 + [tshape, rshape(FOX_WIDTH), tshape, rshape(MEM_WIDTH),
                                               jax.ShapeDtypeStruct((N, FL_PAD), F32)],
        scratch_shapes=[pltpu.VMEM((cw // LANES, tm, LANES), F32)],
        compiler_params=_params("parallel"),
        name="inproj",
    )(x2d, g_mix, w_all, bd64, bd128, gqa, gka, gqb, gkb, gqm, bf_pad, rc, rs1, rs2)
    h, qa_g, ka_g, va_g = outs[0], outs[1:4], outs[4:7], outs[7:10]
    qbt, kb, vbt, qm, logf = outs[10:]

    n_hp = FOX_HEADS // 2
    tri = (jnp.arange(CUMSUM_BLOCK)[:, None] >= jnp.arange(CUMSUM_BLOCK)[None, :]).astype(BF16)
    cqt, ck_ext = pl.pallas_call(
        _cumsum_kernel,
        grid=(B,),
        in_specs=[pl.BlockSpec((T, FL_PAD), lambda b: (b, 0)), _full((CUMSUM_BLOCK, CUMSUM_BLOCK))],
        out_specs=[pl.BlockSpec((None, n_hp, 2, DECAY_ROWS, T), lambda b: (b, 0, 0, 0, 0)),
                   pl.BlockSpec((None, n_hp, T, LANES), lambda b: (b, 0, 0, 0))],
        out_shape=[jax.ShapeDtypeStruct((B, n_hp, 2, DECAY_ROWS, T), BF16),
                   jax.ShapeDtypeStruct((B, n_hp, T, LANES), BF16)],
        compiler_params=_params("parallel"),
        name="decay_cumsum",
    )(logf, tri)

    tq = FOX_Q_TILE
    nq = T // tq
    n_units = 2 * tq // FOX_COLS
    yb = pl.pallas_call(
        _fox_kernel,
        grid=(B, n_hp, nq),
        in_specs=[pl.BlockSpec((None, LANES, tq), lambda b, hp, i: (b, hp, i)),
                  pl.BlockSpec((None, None, 2, DECAY_ROWS, tq), lambda b, hp, i: (b, hp, 0, 0, i)),
                  pl.BlockSpec((T, LANES), lambda b, hp, i: (b, hp)),
                  pl.BlockSpec((None, None, T, LANES), lambda b, hp, i: (b, hp, 0, 0)),
                  pl.BlockSpec((None, LANES, T), lambda b, hp, i: (b, hp, 0))],
        out_specs=pl.BlockSpec((tq, LANES), lambda b, hp, i: (b * nq + i, hp)),
        out_shape=jax.ShapeDtypeStruct((N, FOX_WIDTH), BF16),
        scratch_shapes=[pltpu.VMEM((2, MXU_COLS, tq), BF16),
                        pltpu.VMEM((2, n_units, FOX_K_TILE, FOX_COLS), F32),
                        pltpu.VMEM((n_units, 1, FOX_COLS), F32),
                        pltpu.VMEM((n_units, HEAD_DIM + FOX_DEN_ROWS, FOX_COLS), F32)],
        compiler_params=_params("parallel", "parallel", "arbitrary"),
        name="fox_attention",
    )(qbt, cqt, kb, ck_ext, vbt)

    tile = DIL_TILE
    assert T % tile == 0 and all((tile // d) % DIL_BACK == 0 for d in dils)
    lane_head = jnp.arange(DIL_OUT) // HEAD_DIM
    head_mask = jnp.broadcast_to((lane_head[None, :] == jnp.arange(DIL_HEADS)[:, None])[:, None, :],
                                 (DIL_HEADS, DIL_BACK, DIL_OUT)).astype(BF16)
    dil_specs, dil_args = [], []
    for gi, d in enumerate(dils):
        per_tile = tile // d // DIL_BACK
        cur = pl.BlockSpec((None, d, tile // d, DIL_OUT), lambda b, j: (b, 0, j, 0))
        prev = pl.BlockSpec((None, d, DIL_BACK, DIL_OUT),
                            lambda b, j, per_tile=per_tile: (b, 0, jnp.maximum(j * per_tile - 1, 0), 0))
        dil_specs += [cur, cur, prev, cur, prev]
        dil_args += [qa_g[gi], ka_g[gi], ka_g[gi], va_g[gi], va_g[gi]]
    ya = pl.pallas_call(
        _dilated_kernel,
        grid=(B, T // tile),
        in_specs=[_full((DIL_HEADS, DIL_BACK, DIL_OUT))] + dil_specs,
        out_specs=pl.BlockSpec((tile, DIL_OUT), lambda b, j: (b * (T // tile) + j, 0)),
        out_shape=jax.ShapeDtypeStruct((N, DIL_OUT), BF16),
        scratch_shapes=[pltpu.VMEM((N_DIL_GROUPS, DIL_OUT // LANES, tile, LANES), F32)] * 2,
        compiler_params=_params("parallel", "arbitrary"),
        name="dilated_attention",
    )(head_mask, *dil_args)

    gkm = g_kM[0].reshape(1, MEM_HEAD_DIM)
    km, vm = pl.pallas_call(
        _memkv_kernel,
        grid=(B,),
        in_specs=[pl.BlockSpec((None, mem_len, D), lambda b: (b, 0, 0)), _full((1, D)),
                  _full((D, 2 * MEM_WIDTH)), _full((1, MEM_HEAD_DIM))],
        out_specs=[pl.BlockSpec((None, mem_len, MEM_WIDTH), lambda b: (b, 0, 0))] * 2,
        out_shape=[jax.ShapeDtypeStruct((B, mem_len, MEM_WIDTH), BF16)] * 2,
        compiler_params=_params("parallel"),
        name="mem_kv",
    )(mem, g_mem, w_mem_kv[0].astype(BF16), gkm)

    mem_spec = pl.BlockSpec((None, mem_len, MEM_WIDTH), lambda i: (i // (T // tm), 0, 0))
    x_mid = pl.pallas_call(
        _merge_kernel,
        grid=(N // tm,),
        in_specs=[row(D), row(D), row(DIL_OUT), row(FOX_WIDTH), row(MEM_WIDTH), mem_spec, mem_spec,
                  _full((D, 3 * D)), _full((1, 3 * D)), _full((DIL_OUT, D)), _full((FOX_WIDTH, D)),
                  _full((MEM_WIDTH, D)), _full((D, D))],
        out_specs=row(D),
        out_shape=jax.ShapeDtypeStruct((N, D), F32),
        scratch_shapes=[pltpu.VMEM((tm, D), BF16)],
        compiler_params=_params("parallel"),
        name="merge_outproj",
    )(x2d, h, ya, yb, qm, km, vm, w_gate[0].astype(BF16), b_gate, w_br_a[0].astype(BF16),
      w_br_b[0].astype(BF16), w_br_m[0].astype(BF16), w_out[0].astype(BF16))

    out = pl.pallas_call(
        _mlp_kernel,
        grid=(N // tm,),
        in_specs=[row(D), _full((1, D)), _full((D, D_FF)), _full((D_FF, D))],
        out_specs=row(D),
        out_shape=jax.ShapeDtypeStruct((N, D), F32),
        scratch_shapes=[pltpu.VMEM((tm, D), BF16)],
        compiler_params=_params("parallel"),
        name="mlp",
    )(x_mid, g_mlp, w_up[0].astype(BF16), w_down[0].astype(BF16))
    return out.reshape(B, T, D)
```

```python
import functools

import jax
import jax.numpy as jnp
import numpy as np
from jax import lax
from jax.experimental import pallas as pl
from jax.experimental.pallas import tpu as pltpu

D_MODEL = 1024
HEAD_DIM = 64
DIL_GROUPS = ((128, 1), (512, 4), (2048, 16))
N_DIL_GROUPS = 3
DIL_HEADS = 4
DIL_WIDTH = N_DIL_GROUPS * DIL_HEADS * HEAD_DIM
DIL_OUT = DIL_HEADS * HEAD_DIM
DIL_BACK = 128
FOX_HEADS = 8
FOX_WIDTH = FOX_HEADS * HEAD_DIM
MEM_HEADS = 4
MEM_HEAD_DIM = 128
MEM_WIDTH = MEM_HEADS * MEM_HEAD_DIM
ROT_DIM = HEAD_DIM // 4
ROPE_THETA = 500000.0
D_FF = 4 * D_MODEL
EPS = 1e-6

LANES = 128
MXU_COLS = 256
FL_PAD = LANES

ROW_TILE = 512
FOX_Q_TILE = 1024
FOX_K_TILE = 256
FOX_COLS = 512
FOX_DEN_ROWS = 16
FOX_LOOKAHEAD = 1
FOX_BLOCKS_PER_TRIP = 4
DECAY_ROWS = 16
CUMSUM_BLOCK = 256
DIL_TILE = 2048
DIL_MERGE_ROWS = 256
DIL_CHAINS = 4

LOG2E = 1.4426950408889634
BF16 = jnp.bfloat16
F32 = jnp.float32
NT_DIMS = (((1,), (1,)), ((), ()))


def _dot(a, b):
    return jnp.dot(a, b, preferred_element_type=F32)


def _dot_nt(a, b):
    return lax.dot_general(a, b, NT_DIMS, preferred_element_type=F32)


def _row_rmsnorm(x, g):
    ms = jnp.mean(x * x, axis=-1, keepdims=True)
    return x * lax.rsqrt(ms + EPS) * g


def _head_rmsnorm(y, blockdiag, gain, head_dim):
    ss = _dot((y * y).astype(BF16), blockdiag)
    return y * lax.rsqrt(ss * (1.0 / head_dim) + EPS) * gain


def _split3(c):
    hi = c.astype(BF16)
    r1 = c - hi.astype(F32)
    mid = r1.astype(BF16)
    lo = (r1 - mid.astype(F32)).astype(BF16)
    return hi, mid, lo


def _inproj_kernel(x_ref, gmix_ref, w_ref, bd64_ref, bd128_ref, gqa_ref, gka_ref, gqb_ref, gkb_ref,
                   gqm_ref, bf_ref, rc_ref, rs1_ref, rs2_ref,
                   h_ref, qa0_ref, qa1_ref, qa2_ref, ka0_ref, ka1_ref, ka2_ref, va0_ref, va1_ref, va2_ref,
                   qbt_ref, kb_ref, vbt_ref, qm_ref, lf_ref, perm_ref):
    h = _row_rmsnorm(x_ref[...], gmix_ref[...]).astype(BF16)
    h_ref[...] = h
    bd64 = bd64_ref[...]
    bd128 = bd128_ref[...]
    rc = rc_ref[...]
    rs1 = rs1_ref[...]
    rs2 = rs2_ref[...]
    cw = MXU_COLS
    tm = x_ref.shape[0]

    def proj(col, width=cw):
        return _dot(h, w_ref[:, col:col + width])

    def rope(y):
        return y * rc + pltpu.roll(y, cw - ROT_DIM // 2, 1) * rs1 + pltpu.roll(y, ROT_DIM // 2, 1) * rs2

    def store_rows(out_ref, c, y):
        out_ref[:, c:c + cw] = y.astype(out_ref.dtype)

    def store_transposed(out_ref, c, y):
        out_ref[c:c + cw, :] = y.T.astype(out_ref.dtype)

    def store_by_residue(out_ref, y):
        d = out_ref.shape[0]
        if d == 1:
            out_ref[0] = y.astype(out_ref.dtype)
            return
        for half in range(cw // LANES):
            perm_ref[half] = y[:, half * LANES:(half + 1) * LANES]
        for r in range(d):
            out_ref[r] = jnp.concatenate(
                [perm_ref[half, pl.ds(r, tm // d, stride=d), :] for half in range(cw // LANES)],
                axis=1).astype(out_ref.dtype)

    rows_of = lambda ref: [(functools.partial(store_rows, ref, c)) for c in range(0, ref.shape[1], cw)]
    cols_of = lambda ref: [(functools.partial(store_transposed, ref, c)) for c in range(0, ref.shape[0], cw)]
    residues_of = lambda refs: [functools.partial(store_by_residue, ref) for ref in refs]
    chunks = [(store, gain_ref, bd, hd, rot)
              for stores, gain_ref, bd, hd, rot in (
                  (residues_of((qa0_ref, qa1_ref, qa2_ref)), gqa_ref, bd64, HEAD_DIM, True),
                  (residues_of((ka0_ref, ka1_ref, ka2_ref)), gka_ref, bd64, HEAD_DIM, True),
                  (residues_of((va0_ref, va1_ref, va2_ref)), None, None, None, False),
                  (cols_of(qbt_ref), gqb_ref, bd64, HEAD_DIM, False),
                  (rows_of(kb_ref), gkb_ref, bd64, HEAD_DIM, False),
                  (cols_of(vbt_ref), None, None, None, False),
                  (rows_of(qm_ref), gqm_ref, bd128, MEM_HEAD_DIM, False))
              for store in stores]
    y_next = proj(0)
    for idx, (store, gain_ref, bd, hd, rot) in enumerate(chunks):
        y = y_next
        y_next = proj((idx + 1) * cw) if idx + 1 < len(chunks) else None
        if gain_ref is not None:
            y = _head_rmsnorm(y, bd, gain_ref[...], hd)
        if rot:
            y = rope(y)
        store(y)
    col = len(chunks) * cw
    z = proj(col, FL_PAD) + bf_ref[...]
    lf_ref[...] = jnp.minimum(z, 0.0) - jnp.log1p(jnp.exp(-jnp.abs(z)))


def _cumsum_kernel(lf_ref, tri_ref, pk_ref, onek_ref, pq_ref, oneq_ref, cqt_ref, ck_ref):
    tri = tri_ref[...]
    n_blocks = lf_ref.shape[0] // CUMSUM_BLOCK
    n_hp = ck_ref.shape[0]
    carry = jnp.zeros((1, LANES), F32)
    for blk in range(n_blocks):
        rows = slice(blk * CUMSUM_BLOCK, (blk + 1) * CUMSUM_BLOCK)
        hi, mid, lo = _split3(lf_ref[rows, :])
        c = _dot(tri, hi) + _dot(tri, mid) + _dot(tri, lo) + carry
        carry = c[CUMSUM_BLOCK - 1:CUMSUM_BLOCK, :]
        c = c * LOG2E
        pieces = jnp.concatenate(_split3(c), axis=1)
        ke = _dot(pieces, pk_ref[...]) + onek_ref[...]
        qe = _dot_nt(pq_ref[...], pieces) + oneq_ref[...]
        for hp in range(n_hp):
            ck_ref[hp, rows, :] = ke[:, hp * LANES:(hp + 1) * LANES].astype(BF16)
            for e in range(2):
                r0 = (2 * hp + e) * DECAY_ROWS
                cqt_ref[hp, e, :, rows] = qe[r0:r0 + DECAY_ROWS, :].astype(BF16)


def _fox_kernel(qt_ref, cqt_ref, k_ref, ck_ref, vt_ref, o_ref, qa_ref, st_ref, bm_ref, m_ref, acc_ref):
    blk = pl.program_id(2)
    tq = FOX_Q_TILE
    tk = FOX_K_TILE
    hd = HEAD_DIM
    cw = FOX_COLS
    zq = jnp.zeros((hd, tq), BF16)
    zd = jnp.zeros((DECAY_ROWS, tq), BF16)
    zpad = jnp.zeros((MXU_COLS - 2 * hd - 2 * DECAY_ROWS, tq), BF16)
    for e in range(2):
        parts = [zq, zq, zd, zd, zpad]
        parts[e] = qt_ref[e * hd:(e + 1) * hd, :]
        parts[2 + e] = cqt_ref[e]
        qa_ref[e] = jnp.concatenate(parts, axis=0)
    units = [(e, c0) for e in range(2) for c0 in range(0, tq, cw)]

    def key_block(key_start):
        rows = pl.ds(pl.multiple_of(key_start, tk), tk)
        return jnp.concatenate([k_ref[rows, :], ck_ref[rows, :]], axis=1)

    def value_block(key_start):
        return vt_ref[:, pl.ds(pl.multiple_of(key_start, tk), tk)]

    def issue_scores(buf, u, k_aug):
        e, c0 = units[u]
        st = _dot(k_aug, qa_ref[e, :, c0:c0 + cw])
        st_ref[buf, u] = st
        bm_ref[buf, u] = jnp.max(st, axis=0, keepdims=True)

    def update(buf, u, vt, key_off):
        e, c0 = units[u]
        st = st_ref[buf, u]
        if key_off is not None and key_off + tk - 1 > c0:
            key = lax.broadcasted_iota(jnp.int32, st.shape, 0) + key_off
            qry = lax.broadcasted_iota(jnp.int32, st.shape, 1) + c0
            st = jnp.where(key <= qry, st, -jnp.inf)
            block_max = jnp.max(st, axis=0, keepdims=True)
        else:
            block_max = bm_ref[buf, u]
        m = m_ref[u]
        m_new = jnp.maximum(m, block_max)
        alpha = jnp.exp2(m - m_new)
        p = jnp.exp2(st - m_new).astype(BF16)
        vt_aug = jnp.concatenate([vt[e * hd:(e + 1) * hd, :], ones_rows], axis=0)
        acc_ref[u] = alpha * acc_ref[u] + _dot(vt_aug, p)
        m_ref[u] = m_new

    ones_rows = jnp.ones((FOX_DEN_ROWS, tk), BF16)
    m_ref[...] = jnp.full(m_ref.shape, -jnp.inf, F32)
    acc_ref[...] = jnp.zeros(acc_ref.shape, F32)
    k_first = key_block(0)
    for u in range(len(units)):
        issue_scores(0, u, k_first)

    per_trip = FOX_BLOCKS_PER_TRIP
    assert per_trip % 2 == 0 and (tq // tk) % per_trip == 0

    def block_group(jg, carry):
        for step in range(per_trip):
            j = per_trip * jg + step
            k_next = key_block((j + 1) * tk)
            vt = value_block(j * tk)
            for u in range(min(FOX_LOOKAHEAD, len(units))):
                issue_scores(1 - step % 2, u, k_next)
            for u in range(len(units)):
                if u + FOX_LOOKAHEAD < len(units):
                    issue_scores(1 - step % 2, u + FOX_LOOKAHEAD, k_next)
                update(step % 2, u, vt, None)
        return carry

    lax.fori_loop(0, blk * (tq // tk // per_trip), block_group, 0)
    for jj in range(tq // tk):
        key_off = jj * tk
        active = [u for u, (e, c0) in enumerate(units) if c0 + cw > key_off]
        nxt_off = key_off + tk
        nxt_active = [u for u, (e, c0) in enumerate(units) if c0 + cw > nxt_off] if nxt_off < tq else []
        if nxt_active:
            k_next = key_block(blk * tq + nxt_off)
        vt = value_block(blk * tq + key_off)
        for u in active:
            if u in nxt_active:
                issue_scores((jj + 1) % 2, u, k_next)
            update(jj % 2, u, vt, key_off)
    per_head = []
    for e in range(2):
        cols = [acc_ref[u, :hd, :] * (1.0 / acc_ref[u, hd:hd + 1, :]) for u, unit in enumerate(units) if unit[0] == e]
        per_head.append(jnp.concatenate(cols, axis=1))
    ot = jnp.concatenate(per_head, axis=0)
    o_ref[...] = ot.T.astype(o_ref.dtype)


def _dilated_kernel(hm_ref, q0_ref, k0_ref, kp0_ref, v0_ref, vp0_ref, q1_ref, k1_ref, kp1_ref, v1_ref, vp1_ref,
                    q2_ref, k2_ref, kp2_ref, v2_ref, vp2_ref, ya_ref, o_scr, l_scr):
    jt = pl.program_id(1)
    c = DIL_BACK
    w = DIL_OUT
    nh = DIL_HEADS
    tile = ya_ref.shape[0]
    qi = lax.broadcasted_iota(jnp.int32, (nh * c, 2 * c), 0) & (c - 1)
    kj = lax.broadcasted_iota(jnp.int32, (nh * c, 2 * c), 1)
    dist = qi + c - kj
    in_band = (dist >= 0) & (dist <= DIL_BACK)
    in_band_first = in_band & (kj >= jnp.where(jt > 0, 0, c))
    lane = lax.broadcasted_iota(jnp.int32, (c, w), 1)
    in_head = [(lane >= hh * HEAD_DIM) & (lane < (hh + 1) * HEAD_DIM) for hh in range(nh)]

    def attend(gi, d, items, mask):
        def scores(item):
            q, kk, _, _ = item
            return _dot_nt(jnp.concatenate([q * hm_ref[hh] for hh in range(nh)], axis=0), kk)

        s_next = scores(items[0])
        for idx, (_, _, vv, tok0) in enumerate(items):
            s = s_next
            s_next = scores(items[idx + 1]) if idx + 1 < len(items) else None
            s = jnp.where(mask, s, -jnp.inf)
            m = jnp.max(s, axis=-1, keepdims=True)
            p = jnp.exp2(s - m)
            den = jnp.sum(p, axis=-1, keepdims=True)
            o4 = _dot(p.astype(BF16), vv) * (1.0 / den)
            lse4 = m + jnp.log2(den)
            o = jnp.zeros((c, w), F32)
            lse = jnp.zeros((c, w), F32)
            for hh in range(nh):
                rows = slice(hh * c, (hh + 1) * c)
                o = jnp.where(in_head[hh], o4[rows], o)
                lse = jnp.where(in_head[hh], lse4[rows], lse)
            for half in range(w // LANES):
                lanes = slice(half * LANES, (half + 1) * LANES)
                o_scr[gi, half, pl.ds(tok0, c, stride=d), :] = o[:, lanes]
                l_scr[gi, half, pl.ds(tok0, c, stride=d), :] = lse[:, lanes]

    groups = ((q0_ref, k0_ref, kp0_ref, v0_ref, vp0_ref), (q1_ref, k1_ref, kp1_ref, v1_ref, vp1_ref),
              (q2_ref, k2_ref, kp2_ref, v2_ref, vp2_ref))
    for gi, (q_ref, k_ref, kp_ref, v_ref, vp_ref) in enumerate(groups):
        d = q_ref.shape[0]
        n_sb = q_ref.shape[1] // c
        res_chunk = min(d, DIL_CHAINS)
        assert d % res_chunk == 0

        def first_item(r, q_ref=q_ref, k_ref=k_ref, kp_ref=kp_ref, v_ref=v_ref, vp_ref=vp_ref):
            kk = jnp.concatenate([kp_ref[r], k_ref[r, 0:c]], axis=0)
            vv = jnp.concatenate([vp_ref[r], v_ref[r, 0:c]], axis=0)
            return q_ref[r, 0:c], kk, vv, r

        def later_item(r, sb, d=d, q_ref=q_ref, k_ref=k_ref, v_ref=v_ref):
            start = pl.multiple_of(sb * c, c)
            window = pl.ds(start - c, 2 * c)
            return q_ref[r, pl.ds(start, c)], k_ref[r, window], v_ref[r, window], sb * (c * d) + r

        def first_chunk(it, carry, gi=gi, d=d, res_chunk=res_chunk, first_item=first_item):
            attend(gi, d, [first_item(it * res_chunk + i) for i in range(res_chunk)], in_band_first)
            return carry

        if d == res_chunk:
            first_chunk(0, 0)
        else:
            lax.fori_loop(0, d // res_chunk, first_chunk, 0)
        if n_sb > 1:
            sb_chunk = DIL_CHAINS // res_chunk if d == res_chunk else 1
            sb_chunk = max(k for k in range(1, sb_chunk + 1) if (n_sb - 1) % k == 0)
            assert d == res_chunk

            def later_chunk(it, carry, gi=gi, d=d, sb_chunk=sb_chunk, later_item=later_item):
                attend(gi, d, [later_item(r, 1 + it * sb_chunk + i) for i in range(sb_chunk) for r in range(d)],
                       in_band)
                return carry

            lax.fori_loop(0, (n_sb - 1) // sb_chunk, later_chunk, 0)

    def merge(ch, carry):
        rows = pl.ds(pl.multiple_of(ch * DIL_MERGE_ROWS, DIL_MERGE_ROWS), DIL_MERGE_ROWS)
        for half in range(w // LANES):
            l = [l_scr[g, half, rows, :] for g in range(N_DIL_GROUPS)]
            mx = jnp.maximum(jnp.maximum(l[0], l[1]), l[2])
            e = [jnp.exp2(lg - mx) for lg in l]
            num = e[0] * o_scr[0, half, rows, :] + e[1] * o_scr[1, half, rows, :] + e[2] * o_scr[2, half, rows, :]
            ya_ref[rows, half * LANES:(half + 1) * LANES] = (num / (e[0] + e[1] + e[2])).astype(ya_ref.dtype)
        return carry

    lax.fori_loop(0, tile // DIL_MERGE_ROWS, merge, 0)


def _memkv_kernel(mem_ref, g_ref, w_ref, gk_ref, km_ref, vm_ref):
    mn = _row_rmsnorm(mem_ref[...], g_ref[...]).astype(BF16)
    kv = _dot(mn, w_ref[...])
    gk = gk_ref[...]
    for hh in range(MEM_HEADS):
        cols = slice(hh * MEM_HEAD_DIM, (hh + 1) * MEM_HEAD_DIM)
        km_ref[:, cols] = _row_rmsnorm(kv[:, cols], gk).astype(km_ref.dtype)
    vm_ref[...] = kv[:, MEM_WIDTH:].astype(vm_ref.dtype)


def _merge_kernel(x_ref, h_ref, ya_ref, yb_ref, qm_ref, km_ref, vm_ref,
                  wg_ref, bg_ref, wa_ref, wb_ref, wm_ref, wo_ref, out_ref, merged_ref):
    ya = ya_ref[...]
    ym = []
    for hh in range(MEM_HEADS):
        cols = slice(hh * MEM_HEAD_DIM, (hh + 1) * MEM_HEAD_DIM)
        s = _dot_nt(qm_ref[:, cols], km_ref[:, cols])
        p = jnp.exp(s - jnp.max(s, axis=-1, keepdims=True))
        den = jnp.sum(p, axis=-1, keepdims=True)
        ym.append((_dot(p.astype(BF16), vm_ref[:, cols]) / den).astype(BF16))
    ym = jnp.concatenate(ym, axis=1)
    yb = yb_ref[...]
    h = h_ref[...]
    cw = 2 * MXU_COLS
    for c in range(0, D_MODEL, cw):
        merged = None
        for k, (y, w_ref) in enumerate(((ya, wa_ref), (yb, wb_ref), (ym, wm_ref))):
            gcol = k * D_MODEL + c
            gate = jax.nn.sigmoid(_dot(h, wg_ref[:, gcol:gcol + cw]) + bg_ref[:, gcol:gcol + cw])
            term = gate * _dot(y, w_ref[:, c:c + cw])
            merged = term if merged is None else merged + term
        merged_ref[:, c:c + cw] = merged.astype(BF16)
    out_ref[...] = x_ref[...] + _dot(merged_ref[...], wo_ref[...])


def _mlp_kernel(x_ref, g_ref, wu_ref, wd_ref, out_ref, h2_ref):
    x = x_ref[...]
    h2_ref[...] = _row_rmsnorm(x, g_ref[...]).astype(BF16)
    cw = D_MODEL
    acc = x
    for c in range(0, D_FF, cw):
        u = jnp.maximum(_dot(h2_ref[...], wu_ref[:, c:c + cw]), 0.0)
        acc = acc + _dot((u * u).astype(BF16), wd_ref[c:c + cw, :])
    out_ref[...] = acc


def _full(shape):
    return pl.BlockSpec(shape, lambda *_: (0,) * len(shape))


def _params(*sem):
    return pltpu.CompilerParams(dimension_semantics=sem)


def _blockdiag(width, head_dim):
    r = jnp.arange(width) // head_dim
    return (r[:, None] == r[None, :]).astype(BF16)


def _decay_placement(n_hp):
    pk = np.zeros((3 * LANES, n_hp * LANES), np.float32)
    onek = np.zeros((1, n_hp * LANES), np.float32)
    pq = np.zeros((n_hp * 2 * DECAY_ROWS, 3 * LANES), np.float32)
    oneq = np.zeros((n_hp * 2 * DECAY_ROWS, CUMSUM_BLOCK), np.float32)
    for hp in range(n_hp):
        for e in range(2):
            head = 2 * hp + e
            kcol = hp * LANES + DECAY_ROWS * e
            qrow = (2 * hp + e) * DECAY_ROWS
            for t in range(3):
                onek[0, kcol + t] = 1.0
                pk[t * LANES + head, kcol + 3 + t] = -1.0
                pq[qrow + t, t * LANES + head] = 1.0
                oneq[qrow + 3 + t, :] = 1.0
    return jnp.asarray(pk, BF16), jnp.asarray(onek), jnp.asarray(pq, BF16), jnp.asarray(oneq)


def _rope_tables(seq, width):
    half = ROT_DIM // 2
    inv_freq = ROPE_THETA ** (-jnp.arange(0, ROT_DIM, 2, dtype=F32) / ROT_DIM)
    ang = jnp.arange(seq, dtype=F32)[:, None] * inv_freq[None, :]
    cos, sin = jnp.cos(ang), jnp.sin(ang)
    ones = jnp.ones((seq, HEAD_DIM - ROT_DIM), F32)
    zeros = jnp.zeros((seq, HEAD_DIM - ROT_DIM), F32)
    zhalf = jnp.zeros((seq, half), F32)
    rc = jnp.concatenate([cos, cos, ones], axis=1)
    rs1 = jnp.concatenate([-sin, zhalf, zeros], axis=1)
    rs2 = jnp.concatenate([zhalf, sin, zeros], axis=1)
    reps = width // HEAD_DIM
    return tuple(jnp.tile(t, (1, reps)) for t in (rc, rs1, rs2))


def kernel(x, mem, g_mix, w_in, b_f, g_qA, g_kA, g_qB, g_kB, g_mem, w_mem_kv, g_qM, g_kM, w_gate, b_gate,
           w_br_a, w_br_b, w_br_m, w_out, g_mlp, w_up, w_down):
    B, T, D = x.shape
    assert D == D_MODEL and w_in.shape[0] == 1, "single-layer kernel"
    N = B * T
    mem_len = mem.shape[1]
    tm = ROW_TILE
    assert T % tm == 0 and T % FOX_Q_TILE == 0 and FOX_Q_TILE % FOX_K_TILE == 0
    x2d = x.reshape(N, D)

    wi = w_in[0]
    offs = [0]
    for wdt in (DIL_WIDTH, DIL_WIDTH, DIL_WIDTH, FOX_WIDTH, FOX_WIDTH, FOX_WIDTH, FOX_HEADS, MEM_WIDTH):
        offs.append(offs[-1] + wdt)
    w_fl = jnp.pad(wi[:, offs[6]:offs[7]], ((0, 0), (0, FL_PAD - FOX_HEADS)))
    w_all = jnp.concatenate([wi[:, :offs[6]], wi[:, offs[7]:], w_fl], axis=1).astype(BF16)
    n_cols = w_all.shape[1]
    bf_pad = jnp.pad(b_f[0], (0, FL_PAD - FOX_HEADS)).reshape(1, FL_PAD)
    cw = MXU_COLS
    q_scale = HEAD_DIM ** -0.5
    gqa = (jnp.tile(g_qA[0], cw // HEAD_DIM) * (q_scale * LOG2E)).reshape(1, cw)
    gka = jnp.tile(g_kA[0], cw // HEAD_DIM).reshape(1, cw)
    gqb = (jnp.tile(g_qB[0], cw // HEAD_DIM) * (q_scale * LOG2E)).reshape(1, cw)
    gkb = jnp.tile(g_kB[0], cw // HEAD_DIM).reshape(1, cw)
    gqm = (jnp.tile(g_qM[0], cw // MEM_HEAD_DIM) * MEM_HEAD_DIM ** -0.5).reshape(1, cw)
    rc, rs1, rs2 = _rope_tables(T, cw)
    bd64 = _blockdiag(cw, HEAD_DIM)
    bd128 = _blockdiag(cw, MEM_HEAD_DIM)

    row = lambda width: pl.BlockSpec((tm, width), lambda i: (i, 0))
    rope_spec = pl.BlockSpec((tm, cw), lambda i: (i % (T // tm), 0))
    tspec = pl.BlockSpec((None, FOX_WIDTH, tm), lambda i: (i // (T // tm), 0, i % (T // tm)))
    tshape = jax.ShapeDtypeStruct((B, FOX_WIDTH, T), BF16)
    rshape = lambda width: jax.ShapeDtypeStruct((N, width), BF16)
    dils = [d for _, d in DIL_GROUPS]
    assert all(win // d == DIL_BACK and tm % d == 0 and (tm // d) % 16 == 0 for win, d in DIL_GROUPS)
    dspecs = [pl.BlockSpec((None, d, tm // d, DIL_OUT), lambda i: (i // (T // tm), 0, i % (T // tm), 0))
              for d in dils]
    dshapes = [jax.ShapeDtypeStruct((B, d, T // d, DIL_OUT), BF16) for d in dils]
    outs = pl.pallas_call(
        _inproj_kernel,
        grid=(N // tm,),
        in_specs=[row(D), _full((1, D)), _full((D, n_cols)), _full((cw, cw)), _full((cw, cw)),
                  _full((1, cw)), _full((1, cw)), _full((1, cw)), _full((1, cw)), _full((1, cw)),
                  _full((1, FL_PAD)), rope_spec, rope_spec, rope_spec],
        out_specs=[row(D)] + dspecs * 3 + [tspec, row(FOX_WIDTH), tspec, row(MEM_WIDTH), row(FL_PAD)],
        out_shape=[rshape(D)] + dshapes * 3 + [tshape, rshape(FOX_WIDTH), tshape, rshape(MEM_WIDTH),
                                               jax.ShapeDtypeStruct((N, FL_PAD), F32)],
        scratch_shapes=[pltpu.VMEM((cw // LANES, tm, LANES), F32)],
        compiler_params=_params("parallel"),
        name="inproj",
    )(x2d, g_mix, w_all, bd64, bd128, gqa, gka, gqb, gkb, gqm, bf_pad, rc, rs1, rs2)
    h, qa_g, ka_g, va_g = outs[0], outs[1:4], outs[4:7], outs[7:10]
    qbt, kb, vbt, qm, logf = outs[10:]

    n_hp = FOX_HEADS // 2
    tri = (jnp.arange(CUMSUM_BLOCK)[:, None] >= jnp.arange(CUMSUM_BLOCK)[None, :]).astype(BF16)
    pk, onek, pq, oneq = _decay_placement(n_hp)
    cqt, ck_ext = pl.pallas_call(
        _cumsum_kernel,
        grid=(B,),
        in_specs=[pl.BlockSpec((T, FL_PAD), lambda b: (b, 0)), _full((CUMSUM_BLOCK, CUMSUM_BLOCK)),
                  _full(pk.shape), _full(onek.shape), _full(pq.shape), _full(oneq.shape)],
        out_specs=[pl.BlockSpec((None, n_hp, 2, DECAY_ROWS, T), lambda b: (b, 0, 0, 0, 0)),
                   pl.BlockSpec((None, n_hp, T, LANES), lambda b: (b, 0, 0, 0))],
        out_shape=[jax.ShapeDtypeStruct((B, n_hp, 2, DECAY_ROWS, T), BF16),
                   jax.ShapeDtypeStruct((B, n_hp, T, LANES), BF16)],
        compiler_params=_params("parallel"),
        name="decay_cumsum",
    )(logf, tri, pk, onek, pq, oneq)

    tq = FOX_Q_TILE
    nq = T // tq
    n_units = 2 * tq // FOX_COLS
    yb = pl.pallas_call(
        _fox_kernel,
        grid=(B, n_hp, nq),
        in_specs=[pl.BlockSpec((None, LANES, tq), lambda b, hp, i: (b, hp, i)),
                  pl.BlockSpec((None, None, 2, DECAY_ROWS, tq), lambda b, hp, i: (b, hp, 0, 0, i)),
                  pl.BlockSpec((T, LANES), lambda b, hp, i: (b, hp)),
                  pl.BlockSpec((None, None, T, LANES), lambda b, hp, i: (b, hp, 0, 0)),
                  pl.BlockSpec((None, LANES, T), lambda b, hp, i: (b, hp, 0))],
        out_specs=pl.BlockSpec((tq, LANES), lambda b, hp, i: (b * nq + i, hp)),
        out_shape=jax.ShapeDtypeStruct((N, FOX_WIDTH), BF16),
        scratch_shapes=[pltpu.VMEM((2, MXU_COLS, tq), BF16),
                        pltpu.VMEM((2, n_units, FOX_K_TILE, FOX_COLS), F32),
                        pltpu.VMEM((2, n_units, 1, FOX_COLS), F32),
                        pltpu.VMEM((n_units, 1, FOX_COLS), F32),
                        pltpu.VMEM((n_units, HEAD_DIM + FOX_DEN_ROWS, FOX_COLS), F32)],
        compiler_params=_params("parallel", "parallel", "arbitrary"),
        name="fox_attention",
    )(qbt, cqt, kb, ck_ext, vbt)

    tile = DIL_TILE
    assert T % tile == 0 and all((tile // d) % DIL_BACK == 0 for d in dils)
    lane_head = jnp.arange(DIL_OUT) // HEAD_DIM
    head_mask = jnp.broadcast_to((lane_head[None, :] == jnp.arange(DIL_HEADS)[:, None])[:, None, :],
                                 (DIL_HEADS, DIL_BACK, DIL_OUT)).astype(BF16)
    dil_specs, dil_args = [], []
    for gi, d in enumerate(dils):
        per_tile = tile // d // DIL_BACK
        cur = pl.BlockSpec((None, d, tile // d, DIL_OUT), lambda b, j: (b, 0, j, 0))
        prev = pl.BlockSpec((None, d, DIL_BACK, DIL_OUT),
                            lambda b, j, per_tile=per_tile: (b, 0, jnp.maximum(j * per_tile - 1, 0), 0))
        dil_specs += [cur, cur, prev, cur, prev]
        dil_args += [qa_g[gi], ka_g[gi], ka_g[gi], va_g[gi], va_g[gi]]
    ya = pl.pallas_call(
        _dilated_kernel,
        grid=(B, T // tile),
        in_specs=[_full((DIL_HEADS, DIL_BACK, DIL_OUT))] + dil_specs,
        out_specs=pl.BlockSpec((tile, DIL_OUT), lambda b, j: (b * (T // tile) + j, 0)),
        out_shape=jax.ShapeDtypeStruct((N, DIL_OUT), BF16),
        scratch_shapes=[pltpu.VMEM((N_DIL_GROUPS, DIL_OUT // LANES, tile, LANES), F32)] * 2,
        compiler_params=_params("parallel", "arbitrary"),
        name="dilated_attention",
    )(head_mask, *dil_args)

    gkm = g_kM[0].reshape(1, MEM_HEAD_DIM)
    km, vm = pl.pallas_call(
        _memkv_kernel,
        grid=(B,),
        in_specs=[pl.BlockSpec((None, mem_len, D), lambda b: (b, 0, 0)), _full((1, D)),
                  _full((D, 2 * MEM_WIDTH)), _full((1, MEM_HEAD_DIM))],
        out_specs=[pl.BlockSpec((None, mem_len, MEM_WIDTH), lambda b: (b, 0, 0))] * 2,
        out_shape=[jax.ShapeDtypeStruct((B, mem_len, MEM_WIDTH), BF16)] * 2,
        compiler_params=_params("parallel"),
        name="mem_kv",
    )(mem, g_mem, w_mem_kv[0].astype(BF16), gkm)

    mem_spec = pl.BlockSpec((None, mem_len, MEM_WIDTH), lambda i: (i // (T // tm), 0, 0))
    x_mid = pl.pallas_call(
        _merge_kernel,
        grid=(N // tm,),
        in_specs=[row(D), row(D), row(DIL_OUT), row(FOX_WIDTH), row(MEM_WIDTH), mem_spec, mem_spec,
                  _full((D, 3 * D)), _full((1, 3 * D)), _full((DIL_OUT, D)), _full((FOX_WIDTH, D)),
                  _full((MEM_WIDTH, D)), _full((D, D))],
        out_specs=row(D),
        out_shape=jax.ShapeDtypeStruct((N, D), F32),
        scratch_shapes=[pltpu.VMEM((tm, D), BF16)],
        compiler_params=_params("parallel"),
        name="merge_outproj",
    )(x2d, h, ya, yb, qm, km, vm, w_gate[0].astype(BF16), b_gate, w_br_a[0].astype(BF16),
      w_br_b[0].astype(BF16), w_br_m[0].astype(BF16), w_out[0].astype(BF16))

    out = pl.pallas_call(
        _mlp_kernel,
        grid=(N // tm,),
        in_specs=[row(D), _full((1, D)), _full((D, D_FF)), _full((D_FF, D))],
        out_specs=row(D),
        out_shape=jax.ShapeDtypeStruct((N, D), F32),
        scratch_shapes=[pltpu.VMEM((tm, D), BF16)],
        compiler_params=_params("parallel"),
        name="mlp",
    )(x_mid, g_mlp, w_up[0].astype(BF16), w_down[0].astype(BF16))
    return out.reshape(B, T, D)
```

```python
import functools

import jax
import jax.numpy as jnp
import numpy as np
from jax import lax
from jax.experimental import pallas as pl
from jax.experimental.pallas import tpu as pltpu

D_MODEL = 1024
HEAD_DIM = 64
DIL_GROUPS = ((128, 1), (512, 4), (2048, 16))
N_DIL_GROUPS = 3
DIL_HEADS = 4
DIL_WIDTH = N_DIL_GROUPS * DIL_HEADS * HEAD_DIM
DIL_OUT = DIL_HEADS * HEAD_DIM
DIL_BACK = 128
FOX_HEADS = 8
FOX_WIDTH = FOX_HEADS * HEAD_DIM
MEM_HEADS = 4
MEM_HEAD_DIM = 128
MEM_WIDTH = MEM_HEADS * MEM_HEAD_DIM
ROT_DIM = HEAD_DIM // 4
ROPE_THETA = 500000.0
D_FF = 4 * D_MODEL
EPS = 1e-6

LANES = 128
MXU_COLS = 256
FL_PAD = LANES

ROW_TILE = 512
INPROJ_ROW_TILE = 512
INPROJ_CHUNKS_PER_DOT = 1
INPROJ_ORDER = (("qa", 0), ("ka", 0), ("qa", 1), ("ka", 1), ("qa", 2), ("ka", 2), ("qb", 0), ("kb", 0),
                ("qb", 1), ("kb", 1), ("qm", 0), ("qm", 1), ("va", 2), ("va", 1), ("vb", 0), ("vb", 1), ("va", 0))
FOX_Q_TILE = 2048
FOX_K_TILE = 256
FOX_COLS = 512
FOX_DEN_ROWS = 16
FOX_LOOKAHEAD = 1
FOX_BLOCKS_PER_TRIP = 4
DECAY_ROWS = 16
CUMSUM_BLOCK = 256
DIL_TILE = 2048
DIL_MERGE_ROWS = 256
DIL_CHAINS = 4

LOG2E = 1.4426950408889634
BF16 = jnp.bfloat16
F32 = jnp.float32
NT_DIMS = (((1,), (1,)), ((), ()))


def _dot(a, b):
    return jnp.dot(a, b, preferred_element_type=F32)


def _dot_nt(a, b):
    return lax.dot_general(a, b, NT_DIMS, preferred_element_type=F32)


def _row_rmsnorm(x, g):
    ms = jnp.mean(x * x, axis=-1, keepdims=True)
    return x * lax.rsqrt(ms + EPS) * g


def _head_rmsnorm(y, blockdiag, gain, head_dim):
    y2 = (y * y).astype(BF16)
    half = y.shape[0] // 2
    ss = jnp.concatenate([_dot(y2[:half], blockdiag), _dot(y2[half:], blockdiag)], axis=0)
    return y * lax.rsqrt(ss * (1.0 / head_dim) + EPS) * gain


def _split3(c):
    hi = c.astype(BF16)
    r1 = c - hi.astype(F32)
    mid = r1.astype(BF16)
    lo = (r1 - mid.astype(F32)).astype(BF16)
    return hi, mid, lo


def _inproj_kernel(x_ref, gmix_ref, w_ref, bd64_ref, bd128_ref, gqa_ref, gka_ref, gqb_ref, gkb_ref,
                   gqm_ref, bf_ref, rc_ref, rs1_ref, rs2_ref,
                   h_ref, qa0_ref, qa1_ref, qa2_ref, ka0_ref, ka1_ref, ka2_ref, va0_ref, va1_ref, va2_ref,
                   qbt_ref, kb_ref, vbt_ref, qm_ref, lf_ref, perm_ref):
    h = _row_rmsnorm(x_ref[...], gmix_ref[...]).astype(BF16)
    h_ref[...] = h
    bd64 = bd64_ref[...]
    bd128 = bd128_ref[...]
    rc = rc_ref[...]
    rs1 = rs1_ref[...]
    rs2 = rs2_ref[...]
    cw = MXU_COLS
    tm = x_ref.shape[0]

    def proj(col, width=cw):
        return _dot(h, w_ref[:, col:col + width])

    def rope(y):
        return y * rc + pltpu.roll(y, cw - ROT_DIM // 2, 1) * rs1 + pltpu.roll(y, ROT_DIM // 2, 1) * rs2

    def store_rows(out_ref, c, y):
        out_ref[:, c:c + cw] = y.astype(out_ref.dtype)

    def store_transposed(out_ref, c, y):
        out_ref[c:c + cw, :] = y.T.astype(out_ref.dtype)

    def store_by_residue(out_ref, y):
        d = out_ref.shape[0]
        if d == 1:
            out_ref[0] = y.astype(out_ref.dtype)
            return
        for half in range(cw // LANES):
            perm_ref[half] = y[:, half * LANES:(half + 1) * LANES]
        for r in range(d):
            out_ref[r] = jnp.concatenate(
                [perm_ref[half, pl.ds(r, tm // d, stride=d), :] for half in range(cw // LANES)],
                axis=1).astype(out_ref.dtype)

    rows_of = lambda ref: [(functools.partial(store_rows, ref, c)) for c in range(0, ref.shape[1], cw)]
    cols_of = lambda ref: [(functools.partial(store_transposed, ref, c)) for c in range(0, ref.shape[0], cw)]
    residues_of = lambda refs: [functools.partial(store_by_residue, ref) for ref in refs]
    segments = {
        "qa": (residues_of((qa0_ref, qa1_ref, qa2_ref)), gqa_ref, bd64, HEAD_DIM, True),
        "ka": (residues_of((ka0_ref, ka1_ref, ka2_ref)), gka_ref, bd64, HEAD_DIM, True),
        "va": (residues_of((va0_ref, va1_ref, va2_ref)), None, None, None, False),
        "qb": (cols_of(qbt_ref), gqb_ref, bd64, HEAD_DIM, False),
        "kb": (rows_of(kb_ref), gkb_ref, bd64, HEAD_DIM, False),
        "vb": (cols_of(vbt_ref), None, None, None, False),
        "qm": (rows_of(qm_ref), gqm_ref, bd128, MEM_HEAD_DIM, False)}
    chunks = [(segments[name][0][piece],) + segments[name][1:] for name, piece in INPROJ_ORDER]
    per_dot = INPROJ_CHUNKS_PER_DOT
    groups = [chunks[i:i + per_dot] for i in range(0, len(chunks), per_dot)]
    starts = [i * cw for i in range(0, len(chunks), per_dot)]
    z = proj(len(chunks) * cw, FL_PAD) + bf_ref[...]
    y_next = proj(starts[0], len(groups[0]) * cw)
    lf_ref[...] = jnp.minimum(z, 0.0) - jnp.log1p(jnp.exp(-jnp.abs(z)))
    for gidx, group in enumerate(groups):
        y_wide = y_next
        y_next = proj(starts[gidx + 1], len(groups[gidx + 1]) * cw) if gidx + 1 < len(groups) else None
        for piece, (store, gain_ref, bd, hd, rot) in enumerate(group):
            y = y_wide[:, piece * cw:(piece + 1) * cw]
            if gain_ref is not None:
                y = _head_rmsnorm(y, bd, gain_ref[...], hd)
            if rot:
                y = rope(y)
            store(y)


def _cumsum_kernel(lf_ref, tri_ref, pk_ref, onek_ref, pq_ref, oneq_ref, cqt_ref, ck_ref):
    tri = tri_ref[...]
    n_blocks = lf_ref.shape[0] // CUMSUM_BLOCK
    n_hp = ck_ref.shape[0]
    carry = jnp.zeros((1, LANES), F32)
    for blk in range(n_blocks):
        rows = slice(blk * CUMSUM_BLOCK, (blk + 1) * CUMSUM_BLOCK)
        hi, mid, lo = _split3(lf_ref[rows, :])
        c = _dot(tri, hi) + _dot(tri, mid) + _dot(tri, lo) + carry
        carry = c[CUMSUM_BLOCK - 1:CUMSUM_BLOCK, :]
        c = c * LOG2E
        pieces = jnp.concatenate(_split3(c), axis=1)
        ke = _dot(pieces, pk_ref[...]) + onek_ref[...]
        qe = _dot_nt(pq_ref[...], pieces) + oneq_ref[...]
        for hp in range(n_hp):
            ck_ref[hp, rows, :] = ke[:, hp * LANES:(hp + 1) * LANES].astype(BF16)
            for e in range(2):
                r0 = (2 * hp + e) * DECAY_ROWS
                cqt_ref[hp, e, :, rows] = qe[r0:r0 + DECAY_ROWS, :].astype(BF16)


def _fox_kernel(qt_ref, cqt_ref, k_ref, ck_ref, vt_ref, o_ref, qa_ref, st_ref, bm_ref, m_ref, acc_ref):
    blk = pl.program_id(2)
    tq = FOX_Q_TILE
    tk = FOX_K_TILE
    hd = HEAD_DIM
    cw = FOX_COLS
    zq = jnp.zeros((hd, tq), BF16)
    zd = jnp.zeros((DECAY_ROWS, tq), BF16)
    zpad = jnp.zeros((MXU_COLS - 2 * hd - 2 * DECAY_ROWS, tq), BF16)
    for e in range(2):
        parts = [zq, zq, zd, zd, zpad]
        parts[e] = qt_ref[e * hd:(e + 1) * hd, :]
        parts[2 + e] = cqt_ref[e]
        qa_ref[e] = jnp.concatenate(parts, axis=0)
    units = [(e, c0) for e in range(2) for c0 in range(0, tq, cw)]

    def key_block(key_start):
        rows = pl.ds(pl.multiple_of(key_start, tk), tk)
        return jnp.concatenate([k_ref[rows, :], ck_ref[rows, :]], axis=1)

    def value_block(key_start):
        return vt_ref[:, pl.ds(pl.multiple_of(key_start, tk), tk)]

    def issue_scores(buf, u, k_aug):
        e, c0 = units[u]
        st = _dot(k_aug, qa_ref[e, :, c0:c0 + cw])
        st_ref[buf, u] = st
        bm_ref[buf, u] = jnp.max(st, axis=0, keepdims=True)

    def update(buf, u, vt, key_off):
        e, c0 = units[u]
        st = st_ref[buf, u]
        if key_off is not None and key_off + tk - 1 > c0:
            key = lax.broadcasted_iota(jnp.int32, st.shape, 0) + key_off
            qry = lax.broadcasted_iota(jnp.int32, st.shape, 1) + c0
            st = jnp.where(key <= qry, st, -jnp.inf)
            block_max = jnp.max(st, axis=0, keepdims=True)
        else:
            block_max = bm_ref[buf, u]
        m = m_ref[u]
        m_new = jnp.maximum(m, block_max)
        alpha = jnp.exp2(m - m_new)
        p = jnp.exp2(st - m_new).astype(BF16)
        vt_aug = jnp.concatenate([vt[e * hd:(e + 1) * hd, :], ones_rows], axis=0)
        acc_ref[u] = alpha * acc_ref[u] + _dot(vt_aug, p)
        m_ref[u] = m_new

    ones_rows = jnp.ones((FOX_DEN_ROWS, tk), BF16)
    m_ref[...] = jnp.full(m_ref.shape, -jnp.inf, F32)
    acc_ref[...] = jnp.zeros(acc_ref.shape, F32)
    k_first = key_block(0)
    for u in range(len(units)):
        issue_scores(0, u, k_first)

    per_trip = FOX_BLOCKS_PER_TRIP
    assert per_trip % 2 == 0 and (tq // tk) % per_trip == 0

    def block_group(jg, carry):
        for step in range(per_trip):
            j = per_trip * jg + step
            k_next = key_block((j + 1) * tk)
            vt = value_block(j * tk)
            for u in range(min(FOX_LOOKAHEAD, len(units))):
                issue_scores(1 - step % 2, u, k_next)
            for u in range(len(units)):
                if u + FOX_LOOKAHEAD < len(units):
                    issue_scores(1 - step % 2, u + FOX_LOOKAHEAD, k_next)
                update(step % 2, u, vt, None)
        return carry

    lax.fori_loop(0, blk * (tq // tk // per_trip), block_group, 0)
    for jj in range(tq // tk):
        key_off = jj * tk
        active = [u for u, (e, c0) in enumerate(units) if c0 + cw > key_off]
        nxt_off = key_off + tk
        nxt_active = [u for u, (e, c0) in enumerate(units) if c0 + cw > nxt_off] if nxt_off < tq else []
        if nxt_active:
            k_next = key_block(blk * tq + nxt_off)
        vt = value_block(blk * tq + key_off)
        for u in active:
            if u in nxt_active:
                issue_scores((jj + 1) % 2, u, k_next)
            update(jj % 2, u, vt, key_off)
    per_head = []
    for e in range(2):
        cols = [acc_ref[u, :hd, :] * (1.0 / acc_ref[u, hd:hd + 1, :]) for u, unit in enumerate(units) if unit[0] == e]
        per_head.append(jnp.concatenate(cols, axis=1))
    ot = jnp.concatenate(per_head, axis=0)
    o_ref[...] = ot.T.astype(o_ref.dtype)


def _dilated_kernel(hm_ref, q0_ref, k0_ref, kp0_ref, v0_ref, vp0_ref, q1_ref, k1_ref, kp1_ref, v1_ref, vp1_ref,
                    q2_ref, k2_ref, kp2_ref, v2_ref, vp2_ref, ya_ref, o_scr, l_scr):
    jt = pl.program_id(1)
    c = DIL_BACK
    w = DIL_OUT
    nh = DIL_HEADS
    tile = ya_ref.shape[0]
    qi = lax.broadcasted_iota(jnp.int32, (nh * c, 2 * c), 0) & (c - 1)
    kj = lax.broadcasted_iota(jnp.int32, (nh * c, 2 * c), 1)
    dist = qi + c - kj
    in_band = (dist >= 0) & (dist <= DIL_BACK)
    in_band_first = in_band & (kj >= jnp.where(jt > 0, 0, c))
    lane = lax.broadcasted_iota(jnp.int32, (c, w), 1)
    in_head = [(lane >= hh * HEAD_DIM) & (lane < (hh + 1) * HEAD_DIM) for hh in range(nh)]

    def attend(gi, d, items, mask):
        def scores(item):
            q, kk, _, _ = item
            return _dot_nt(jnp.concatenate([q * hm_ref[hh] for hh in range(nh)], axis=0), kk)

        s_next = scores(items[0])
        for idx, (_, _, vv, tok0) in enumerate(items):
            s = s_next
            s_next = scores(items[idx + 1]) if idx + 1 < len(items) else None
            s = jnp.where(mask, s, -jnp.inf)
            m = jnp.max(s, axis=-1, keepdims=True)
            p = jnp.exp2(s - m)
            den = jnp.sum(p, axis=-1, keepdims=True)
            o4 = _dot(p.astype(BF16), vv) * (1.0 / den)
            lse4 = m + jnp.log2(den)
            o = jnp.zeros((c, w), F32)
            lse = jnp.zeros((c, w), F32)
            for hh in range(nh):
                rows = slice(hh * c, (hh + 1) * c)
                o = jnp.where(in_head[hh], o4[rows], o)
                lse = jnp.where(in_head[hh], lse4[rows], lse)
            for half in range(w // LANES):
                lanes = slice(half * LANES, (half + 1) * LANES)
                o_scr[gi, half, pl.ds(tok0, c, stride=d), :] = o[:, lanes]
                l_scr[gi, half, pl.ds(tok0, c, stride=d), :] = lse[:, lanes]

    groups = ((q0_ref, k0_ref, kp0_ref, v0_ref, vp0_ref), (q1_ref, k1_ref, kp1_ref, v1_ref, vp1_ref),
              (q2_ref, k2_ref, kp2_ref, v2_ref, vp2_ref))
    for gi, (q_ref, k_ref, kp_ref, v_ref, vp_ref) in enumerate(groups):
        d = q_ref.shape[0]
        n_sb = q_ref.shape[1] // c
        res_chunk = min(d, DIL_CHAINS)
        assert d % res_chunk == 0

        def first_item(r, q_ref=q_ref, k_ref=k_ref, kp_ref=kp_ref, v_ref=v_ref, vp_ref=vp_ref):
            kk = jnp.concatenate([kp_ref[r], k_ref[r, 0:c]], axis=0)
            vv = jnp.concatenate([vp_ref[r], v_ref[r, 0:c]], axis=0)
            return q_ref[r, 0:c], kk, vv, r

        def later_item(r, sb, d=d, q_ref=q_ref, k_ref=k_ref, v_ref=v_ref):
            start = pl.multiple_of(sb * c, c)
            window = pl.ds(start - c, 2 * c)
            return q_ref[r, pl.ds(start, c)], k_ref[r, window], v_ref[r, window], sb * (c * d) + r

        def first_chunk(it, carry, gi=gi, d=d, res_chunk=res_chunk, first_item=first_item):
            attend(gi, d, [first_item(it * res_chunk + i) for i in range(res_chunk)], in_band_first)
            return carry

        if d == res_chunk:
            first_chunk(0, 0)
        else:
            lax.fori_loop(0, d // res_chunk, first_chunk, 0)
        if n_sb > 1:
            sb_chunk = DIL_CHAINS // res_chunk if d == res_chunk else 1
            sb_chunk = max(k for k in range(1, sb_chunk + 1) if (n_sb - 1) % k == 0)
            assert d == res_chunk

            def later_chunk(it, carry, gi=gi, d=d, sb_chunk=sb_chunk, later_item=later_item):
                attend(gi, d, [later_item(r, 1 + it * sb_chunk + i) for i in range(sb_chunk) for r in range(d)],
                       in_band)
                return carry

            lax.fori_loop(0, (n_sb - 1) // sb_chunk, later_chunk, 0)

    def merge(ch, carry):
        rows = pl.ds(pl.multiple_of(ch * DIL_MERGE_ROWS, DIL_MERGE_ROWS), DIL_MERGE_ROWS)
        for half in range(w // LANES):
            l = [l_scr[g, half, rows, :] for g in range(N_DIL_GROUPS)]
            mx = jnp.maximum(jnp.maximum(l[0], l[1]), l[2])
            e = [jnp.exp2(lg - mx) for lg in l]
            num = e[0] * o_scr[0, half, rows, :] + e[1] * o_scr[1, half, rows, :] + e[2] * o_scr[2, half, rows, :]
            ya_ref[rows, half * LANES:(half + 1) * LANES] = (num / (e[0] + e[1] + e[2])).astype(ya_ref.dtype)
        return carry

    lax.fori_loop(0, tile // DIL_MERGE_ROWS, merge, 0)


def _memkv_kernel(mem_ref, g_ref, w_ref, gk_ref, km_ref, vm_ref):
    mn = _row_rmsnorm(mem_ref[...], g_ref[...]).astype(BF16)
    kv = _dot(mn, w_ref[...])
    gk = gk_ref[...]
    for hh in range(MEM_HEADS):
        cols = slice(hh * MEM_HEAD_DIM, (hh + 1) * MEM_HEAD_DIM)
        km_ref[:, cols] = _row_rmsnorm(kv[:, cols], gk).astype(km_ref.dtype)
    vm_ref[...] = kv[:, MEM_WIDTH:].astype(vm_ref.dtype)


def _merge_kernel(x_ref, h_ref, ya_ref, yb_ref, qm_ref, km_ref, vm_ref,
                  wg_ref, bg_ref, wa_ref, wb_ref, wm_ref, wo_ref, out_ref, merged_ref):
    ya = ya_ref[...]
    ym = []
    for hh in range(MEM_HEADS):
        cols = slice(hh * MEM_HEAD_DIM, (hh + 1) * MEM_HEAD_DIM)
        s = _dot_nt(qm_ref[:, cols], km_ref[:, cols])
        p = jnp.exp(s - jnp.max(s, axis=-1, keepdims=True))
        den = jnp.sum(p, axis=-1, keepdims=True)
        ym.append((_dot(p.astype(BF16), vm_ref[:, cols]) / den).astype(BF16))
    ym = jnp.concatenate(ym, axis=1)
    yb = yb_ref[...]
    h = h_ref[...]
    cw = 2 * MXU_COLS
    for c in range(0, D_MODEL, cw):
        merged = None
        for k, (y, w_ref) in enumerate(((ya, wa_ref), (yb, wb_ref), (ym, wm_ref))):
            gcol = k * D_MODEL + c
            gate = jax.nn.sigmoid(_dot(h, wg_ref[:, gcol:gcol + cw]) + bg_ref[:, gcol:gcol + cw])
            term = gate * _dot(y, w_ref[:, c:c + cw])
            merged = term if merged is None else merged + term
        merged_ref[:, c:c + cw] = merged.astype(BF16)
    out_ref[...] = x_ref[...] + _dot(merged_ref[...], wo_ref[...])


def _mlp_kernel(x_ref, g_ref, wu_ref, wd_ref, out_ref, h2_ref):
    x = x_ref[...]
    h2_ref[...] = _row_rmsnorm(x, g_ref[...]).astype(BF16)
    cw = D_MODEL
    acc = x
    for c in range(0, D_FF, cw):
        u = jnp.maximum(_dot(h2_ref[...], wu_ref[:, c:c + cw]), 0.0)
        acc = acc + _dot((u * u).astype(BF16), wd_ref[c:c + cw, :])
    out_ref[...] = acc


def _full(shape):
    return pl.BlockSpec(shape, lambda *_: (0,) * len(shape), pipeline_mode=pl.Buffered(1))


def _params(*sem):
    return pltpu.CompilerParams(dimension_semantics=sem)


def _blockdiag(width, head_dim):
    r = jnp.arange(width) // head_dim
    return (r[:, None] == r[None, :]).astype(BF16)


def _decay_placement(n_hp):
    pk = np.zeros((3 * LANES, n_hp * LANES), np.float32)
    onek = np.zeros((1, n_hp * LANES), np.float32)
    pq = np.zeros((n_hp * 2 * DECAY_ROWS, 3 * LANES), np.float32)
    oneq = np.zeros((n_hp * 2 * DECAY_ROWS, CUMSUM_BLOCK), np.float32)
    for hp in range(n_hp):
        for e in range(2):
            head = 2 * hp + e
            kcol = hp * LANES + DECAY_ROWS * e
            qrow = (2 * hp + e) * DECAY_ROWS
            for t in range(3):
                onek[0, kcol + t] = 1.0
                pk[t * LANES + head, kcol + 3 + t] = -1.0
                pq[qrow + t, t * LANES + head] = 1.0
                oneq[qrow + 3 + t, :] = 1.0
    return jnp.asarray(pk, BF16), jnp.asarray(onek), jnp.asarray(pq, BF16), jnp.asarray(oneq)


def _rope_tables(seq, width):
    half = ROT_DIM // 2
    inv_freq = ROPE_THETA ** (-jnp.arange(0, ROT_DIM, 2, dtype=F32) / ROT_DIM)
    ang = jnp.arange(seq, dtype=F32)[:, None] * inv_freq[None, :]
    cos, sin = jnp.cos(ang), jnp.sin(ang)
    ones = jnp.ones((seq, HEAD_DIM - ROT_DIM), F32)
    zeros = jnp.zeros((seq, HEAD_DIM - ROT_DIM), F32)
    zhalf = jnp.zeros((seq, half), F32)
    rc = jnp.concatenate([cos, cos, ones], axis=1)
    rs1 = jnp.concatenate([-sin, zhalf, zeros], axis=1)
    rs2 = jnp.concatenate([zhalf, sin, zeros], axis=1)
    reps = width // HEAD_DIM
    return tuple(jnp.tile(t, (1, reps)) for t in (rc, rs1, rs2))


def kernel(x, mem, g_mix, w_in, b_f, g_qA, g_kA, g_qB, g_kB, g_mem, w_mem_kv, g_qM, g_kM, w_gate, b_gate,
           w_br_a, w_br_b, w_br_m, w_out, g_mlp, w_up, w_down):
    B, T, D = x.shape
    assert D == D_MODEL and w_in.shape[0] == 1, "single-layer kernel"
    N = B * T
    mem_len = mem.shape[1]
    tm = ROW_TILE
    assert T % tm == 0 and T % FOX_Q_TILE == 0 and FOX_Q_TILE % FOX_K_TILE == 0
    x2d = x.reshape(N, D)

    wi = w_in[0]
    offs = [0]
    for wdt in (DIL_WIDTH, DIL_WIDTH, DIL_WIDTH, FOX_WIDTH, FOX_WIDTH, FOX_WIDTH, FOX_HEADS, MEM_WIDTH):
        offs.append(offs[-1] + wdt)
    w_fl = jnp.pad(wi[:, offs[6]:offs[7]], ((0, 0), (0, FL_PAD - FOX_HEADS)))
    seg_start = dict(qa=offs[0], ka=offs[1], va=offs[2], qb=offs[3], kb=offs[4], vb=offs[5], qm=offs[7])
    pieces = [wi[:, seg_start[name] + piece * MXU_COLS:seg_start[name] + (piece + 1) * MXU_COLS]
              for name, piece in INPROJ_ORDER]
    w_all = jnp.concatenate(pieces + [w_fl], axis=1).astype(BF16)
    n_cols = w_all.shape[1]
    bf_pad = jnp.pad(b_f[0], (0, FL_PAD - FOX_HEADS)).reshape(1, FL_PAD)
    cw = MXU_COLS
    q_scale = HEAD_DIM ** -0.5
    gqa = (jnp.tile(g_qA[0], cw // HEAD_DIM) * (q_scale * LOG2E)).reshape(1, cw)
    gka = jnp.tile(g_kA[0], cw // HEAD_DIM).reshape(1, cw)
    gqb = (jnp.tile(g_qB[0], cw // HEAD_DIM) * (q_scale * LOG2E)).reshape(1, cw)
    gkb = jnp.tile(g_kB[0], cw // HEAD_DIM).reshape(1, cw)
    gqm = (jnp.tile(g_qM[0], cw // MEM_HEAD_DIM) * MEM_HEAD_DIM ** -0.5).reshape(1, cw)
    rc, rs1, rs2 = _rope_tables(T, cw)
    bd64 = _blockdiag(cw, HEAD_DIM)
    bd128 = _blockdiag(cw, MEM_HEAD_DIM)

    row = lambda width: pl.BlockSpec((tm, width), lambda i: (i, 0))
    tm1 = INPROJ_ROW_TILE
    per_seq = T // tm1
    assert T % tm1 == 0
    row1 = lambda width: pl.BlockSpec((tm1, width), lambda i: (i, 0))
    rope_spec = pl.BlockSpec((tm1, cw), lambda i: (i % per_seq, 0))
    tspec = pl.BlockSpec((None, FOX_WIDTH, tm1), lambda i: (i // per_seq, 0, i % per_seq))
    tshape = jax.ShapeDtypeStruct((B, FOX_WIDTH, T), BF16)
    rshape = lambda width: jax.ShapeDtypeStruct((N, width), BF16)
    dils = [d for _, d in DIL_GROUPS]
    assert all(win // d == DIL_BACK and tm1 % d == 0 and (tm1 // d) % 16 == 0 for win, d in DIL_GROUPS)
    dspecs = [pl.BlockSpec((None, d, tm1 // d, DIL_OUT), lambda i: (i // per_seq, 0, i % per_seq, 0))
              for d in dils]
    dshapes = [jax.ShapeDtypeStruct((B, d, T // d, DIL_OUT), BF16) for d in dils]
    outs = pl.pallas_call(
        _inproj_kernel,
        grid=(N // tm1,),
        in_specs=[row1(D), _full((1, D)), _full((D, n_cols)), _full((cw, cw)), _full((cw, cw)),
                  _full((1, cw)), _full((1, cw)), _full((1, cw)), _full((1, cw)), _full((1, cw)),
                  _full((1, FL_PAD)), rope_spec, rope_spec, rope_spec],
        out_specs=[row1(D)] + dspecs * 3 + [tspec, row1(FOX_WIDTH), tspec, row1(MEM_WIDTH), row1(FL_PAD)],
        out_shape=[rshape(D)] + dshapes * 3 + [tshape, rshape(FOX_WIDTH), tshape, rshape(MEM_WIDTH),
                                               jax.ShapeDtypeStruct((N, FL_PAD), F32)],
        scratch_shapes=[pltpu.VMEM((cw // LANES, tm1, LANES), F32)],
        compiler_params=_params("parallel"),
        name="inproj",
    )(x2d, g_mix, w_all, bd64, bd128, gqa, gka, gqb, gkb, gqm, bf_pad, rc, rs1, rs2)
    h, qa_g, ka_g, va_g = outs[0], outs[1:4], outs[4:7], outs[7:10]
    qbt, kb, vbt, qm, logf = outs[10:]

    n_hp = FOX_HEADS // 2
    tri = (jnp.arange(CUMSUM_BLOCK)[:, None] >= jnp.arange(CUMSUM_BLOCK)[None, :]).astype(BF16)
    pk, onek, pq, oneq = _decay_placement(n_hp)
    cqt, ck_ext = pl.pallas_call(
        _cumsum_kernel,
        grid=(B,),
        in_specs=[pl.BlockSpec((T, FL_PAD), lambda b: (b, 0)), _full((CUMSUM_BLOCK, CUMSUM_BLOCK)),
                  _full(pk.shape), _full(onek.shape), _full(pq.shape), _full(oneq.shape)],
        out_specs=[pl.BlockSpec((None, n_hp, 2, DECAY_ROWS, T), lambda b: (b, 0, 0, 0, 0)),
                   pl.BlockSpec((None, n_hp, T, LANES), lambda b: (b, 0, 0, 0))],
        out_shape=[jax.ShapeDtypeStruct((B, n_hp, 2, DECAY_ROWS, T), BF16),
                   jax.ShapeDtypeStruct((B, n_hp, T, LANES), BF16)],
        compiler_params=_params("parallel"),
        name="decay_cumsum",
    )(logf, tri, pk, onek, pq, oneq)

    tq = FOX_Q_TILE
    nq = T // tq
    n_units = 2 * tq // FOX_COLS
    yb = pl.pallas_call(
        _fox_kernel,
        grid=(B, n_hp, nq),
        in_specs=[pl.BlockSpec((None, LANES, tq), lambda b, hp, i: (b, hp, i)),
                  pl.BlockSpec((None, None, 2, DECAY_ROWS, tq), lambda b, hp, i: (b, hp, 0, 0, i)),
                  pl.BlockSpec((T, LANES), lambda b, hp, i: (b, hp)),
                  pl.BlockSpec((None, None, T, LANES), lambda b, hp, i: (b, hp, 0, 0)),
                  pl.BlockSpec((None, LANES, T), lambda b, hp, i: (b, hp, 0))],
        out_specs=pl.BlockSpec((tq, LANES), lambda b, hp, i: (b * nq + i, hp)),
        out_shape=jax.ShapeDtypeStruct((N, FOX_WIDTH), BF16),
        scratch_shapes=[pltpu.VMEM((2, MXU_COLS, tq), BF16),
                        pltpu.VMEM((2, n_units, FOX_K_TILE, FOX_COLS), F32),
                        pltpu.VMEM((2, n_units, 1, FOX_COLS), F32),
                        pltpu.VMEM((n_units, 1, FOX_COLS), F32),
                        pltpu.VMEM((n_units, HEAD_DIM + FOX_DEN_ROWS, FOX_COLS), F32)],
        compiler_params=_params("parallel", "parallel", "arbitrary"),
        name="fox_attention",
    )(qbt, cqt, kb, ck_ext, vbt)

    tile = DIL_TILE
    assert T % tile == 0 and all((tile // d) % DIL_BACK == 0 for d in dils)
    lane_head = jnp.arange(DIL_OUT) // HEAD_DIM
    head_mask = jnp.broadcast_to((lane_head[None, :] == jnp.arange(DIL_HEADS)[:, None])[:, None, :],
                                 (DIL_HEADS, DIL_BACK, DIL_OUT)).astype(BF16)
    dil_specs, dil_args = [], []
    for gi, d in enumerate(dils):
        per_tile = tile // d // DIL_BACK
        cur = pl.BlockSpec((None, d, tile // d, DIL_OUT), lambda b, j: (b, 0, j, 0))
        prev = pl.BlockSpec((None, d, DIL_BACK, DIL_OUT),
                            lambda b, j, per_tile=per_tile: (b, 0, jnp.maximum(j * per_tile - 1, 0), 0))
        dil_specs += [cur, cur, prev, cur, prev]
        dil_args += [qa_g[gi], ka_g[gi], ka_g[gi], va_g[gi], va_g[gi]]
    ya = pl.pallas_call(
        _dilated_kernel,
        grid=(B, T // tile),
        in_specs=[_full((DIL_HEADS, DIL_BACK, DIL_OUT))] + dil_specs,
        out_specs=pl.BlockSpec((tile, DIL_OUT), lambda b, j: (b * (T // tile) + j, 0)),
        out_shape=jax.ShapeDtypeStruct((N, DIL_OUT), BF16),
        scratch_shapes=[pltpu.VMEM((N_DIL_GROUPS, DIL_OUT // LANES, tile, LANES), F32)] * 2,
        compiler_params=_params("parallel", "arbitrary"),
        name="dilated_attention",
    )(head_mask, *dil_args)

    gkm = g_kM[0].reshape(1, MEM_HEAD_DIM)
    km, vm = pl.pallas_call(
        _memkv_kernel,
        grid=(B,),
        in_specs=[pl.BlockSpec((None, mem_len, D), lambda b: (b, 0, 0)), _full((1, D)),
                  _full((D, 2 * MEM_WIDTH)), _full((1, MEM_HEAD_DIM))],
        out_specs=[pl.BlockSpec((None, mem_len, MEM_WIDTH), lambda b: (b, 0, 0))] * 2,
        out_shape=[jax.ShapeDtypeStruct((B, mem_len, MEM_WIDTH), BF16)] * 2,
        compiler_params=_params("parallel"),
        name="mem_kv",
    )(mem, g_mem, w_mem_kv[0].astype(BF16), gkm)

    mem_spec = pl.BlockSpec((None, mem_len, MEM_WIDTH), lambda i: (i // (T // tm), 0, 0))
    x_mid = pl.pallas_call(
        _merge_kernel,
        grid=(N // tm,),
        in_specs=[row(D), row(D), row(DIL_OUT), row(FOX_WIDTH), row(MEM_WIDTH), mem_spec, mem_spec,
                  _full((D, 3 * D)), _full((1, 3 * D)), _full((DIL_OUT, D)), _full((FOX_WIDTH, D)),
                  _full((MEM_WIDTH, D)), _full((D, D))],
        out_specs=row(D),
        out_shape=jax.ShapeDtypeStruct((N, D), F32),
        scratch_shapes=[pltpu.VMEM((tm, D), BF16)],
        compiler_params=_params("parallel"),
        name="merge_outproj",
    )(x2d, h, ya, yb, qm, km, vm, w_gate[0].astype(BF16), b_gate, w_br_a[0].astype(BF16),
      w_br_b[0].astype(BF16), w_br_m[0].astype(BF16), w_out[0].astype(BF16))

    out = pl.pallas_call(
        _mlp_kernel,
        grid=(N // tm,),
        in_specs=[row(D), _full((1, D)), _full((D, D_FF)), _full((D_FF, D))],
        out_specs=row(D),
        out_shape=jax.ShapeDtypeStruct((N, D), F32),
        scratch_shapes=[pltpu.VMEM((tm, D), BF16)],
        compiler_params=_params("parallel"),
        name="mlp",
    )(x_mid, g_mlp, w_up[0].astype(BF16), w_down[0].astype(BF16))
    return out.reshape(B, T, D)
```

```python
import functools

import jax
import jax.numpy as jnp
import numpy as np
from jax import lax
from jax.experimental import pallas as pl
from jax.experimental.pallas import tpu as pltpu

D_MODEL = 1024
HEAD_DIM = 64
DIL_GROUPS = ((128, 1), (512, 4), (2048, 16))
N_DIL_GROUPS = 3
DIL_HEADS = 4
DIL_WIDTH = N_DIL_GROUPS * DIL_HEADS * HEAD_DIM
DIL_OUT = DIL_HEADS * HEAD_DIM
DIL_BACK = 128
FOX_HEADS = 8
FOX_WIDTH = FOX_HEADS * HEAD_DIM
MEM_HEADS = 4
MEM_HEAD_DIM = 128
MEM_WIDTH = MEM_HEADS * MEM_HEAD_DIM
ROT_DIM = HEAD_DIM // 4
ROPE_THETA = 500000.0
D_FF = 4 * D_MODEL
EPS = 1e-6

LANES = 128
MXU_COLS = 256
FL_PAD = LANES

ROW_TILE = 1024
INPROJ_ROW_TILE = 512
INPROJ_COL = dict(qa=0, ka=DIL_WIDTH, va=2 * DIL_WIDTH, qb=3 * DIL_WIDTH, kb=3 * DIL_WIDTH + FOX_WIDTH,
                  vb=3 * DIL_WIDTH + 2 * FOX_WIDTH, qm=3 * DIL_WIDTH + 3 * FOX_WIDTH,
                  fl=3 * DIL_WIDTH + 3 * FOX_WIDTH + MEM_WIDTH)
INPROJ_ORDER = (("qa", 0), ("ka", 0), ("qa", 1), ("ka", 1), ("qa", 2), ("ka", 2), ("qb", 0), ("kb", 0),
                ("qb", 1), ("kb", 1), ("qm", 0), ("qm", 1), ("va", 2), ("va", 1), ("vb", 0), ("vb", 1), ("va", 0))
FOX_Q_TILE = 2048
FOX_K_TILE = 256
FOX_COLS = 512
FOX_DEN_ROWS = 16
FOX_LOOKAHEAD = 1
FOX_BLOCKS_PER_TRIP = 4
DECAY_ROWS = 16
CUMSUM_BLOCK = 256
DIL_TILE = 2048
DIL_MERGE_ROWS = 256
DIL_CHAINS = 8
DIL_LATER_ITEMS = 12

LOG2E = 1.4426950408889634
BF16 = jnp.bfloat16
F32 = jnp.float32
NT_DIMS = (((1,), (1,)), ((), ()))


def _dot(a, b):
    return jnp.dot(a, b, preferred_element_type=F32)


def _dot_nt(a, b):
    return lax.dot_general(a, b, NT_DIMS, preferred_element_type=F32)


def _row_rmsnorm(x, g):
    ms = jnp.mean(x * x, axis=-1, keepdims=True)
    return x * lax.rsqrt(ms + EPS) * g


def _head_rmsnorm(y, blockdiag, gain, head_dim):
    y2 = (y * y).astype(BF16)
    half = y.shape[0] // 2
    ss = jnp.concatenate([_dot(y2[:half], blockdiag), _dot(y2[half:], blockdiag)], axis=0)
    return y * lax.rsqrt(ss * (1.0 / head_dim) + EPS) * gain


def _split3(c):
    hi = c.astype(BF16)
    r1 = c - hi.astype(F32)
    mid = r1.astype(BF16)
    lo = (r1 - mid.astype(F32)).astype(BF16)
    return hi, mid, lo


def _inproj_kernel(x_ref, gmix_ref, w_ref, bd64_ref, bd128_ref, gqa_ref, gka_ref, gqb_ref, gkb_ref,
                   gqm_ref, bf_ref, rc_ref, rs1_ref, rs2_ref,
                   h_ref, qa0_ref, qa1_ref, qa2_ref, ka0_ref, ka1_ref, ka2_ref, va0_ref, va1_ref, va2_ref,
                   qbt_ref, kb_ref, vbt_ref, qm_ref, lf_ref, perm_ref):
    h = _row_rmsnorm(x_ref[...], gmix_ref[...]).astype(BF16)
    h_ref[...] = h
    bd64 = bd64_ref[...]
    bd128 = bd128_ref[...]
    rc = rc_ref[...]
    rs1 = rs1_ref[...]
    rs2 = rs2_ref[...]
    cw = MXU_COLS
    tm = x_ref.shape[0]

    def proj(col, width=cw):
        return _dot(h, w_ref[:, col:col + width])

    def rope(y):
        return y * rc + pltpu.roll(y, cw - ROT_DIM // 2, 1) * rs1 + pltpu.roll(y, ROT_DIM // 2, 1) * rs2

    def store_rows(out_ref, c, y):
        out_ref[:, c:c + cw] = y.astype(out_ref.dtype)

    def store_transposed(out_ref, c, y):
        out_ref[c:c + cw, :] = y.T.astype(out_ref.dtype)

    def store_by_residue(out_ref, y):
        d = out_ref.shape[0]
        if d == 1:
            out_ref[0] = y.astype(out_ref.dtype)
            return
        for half in range(cw // LANES):
            perm_ref[half] = y[:, half * LANES:(half + 1) * LANES]
        for r in range(d):
            out_ref[r] = jnp.concatenate(
                [perm_ref[half, pl.ds(r, tm // d, stride=d), :] for half in range(cw // LANES)],
                axis=1).astype(out_ref.dtype)

    rows_of = lambda ref: [(functools.partial(store_rows, ref, c)) for c in range(0, ref.shape[1], cw)]
    cols_of = lambda ref: [(functools.partial(store_transposed, ref, c)) for c in range(0, ref.shape[0], cw)]
    residues_of = lambda refs: [functools.partial(store_by_residue, ref) for ref in refs]
    segments = {
        "qa": (residues_of((qa0_ref, qa1_ref, qa2_ref)), gqa_ref, bd64, HEAD_DIM, True),
        "ka": (residues_of((ka0_ref, ka1_ref, ka2_ref)), gka_ref, bd64, HEAD_DIM, True),
        "va": (residues_of((va0_ref, va1_ref, va2_ref)), None, None, None, False),
        "qb": (cols_of(qbt_ref), gqb_ref, bd64, HEAD_DIM, False),
        "kb": (rows_of(kb_ref), gkb_ref, bd64, HEAD_DIM, False),
        "vb": (cols_of(vbt_ref), None, None, None, False),
        "qm": (rows_of(qm_ref), gqm_ref, bd128, MEM_HEAD_DIM, False)}
    chunks = [(INPROJ_COL[name] + piece * cw, segments[name][0][piece]) + segments[name][1:]
              for name, piece in INPROJ_ORDER]
    z = proj(INPROJ_COL["fl"], FL_PAD) + bf_ref[...]
    y_next = proj(chunks[0][0])
    lf_ref[...] = jnp.minimum(z, 0.0) - jnp.log1p(jnp.exp(-jnp.abs(z)))
    for idx, (_, store, gain_ref, bd, hd, rot) in enumerate(chunks):
        y = y_next
        y_next = proj(chunks[idx + 1][0]) if idx + 1 < len(chunks) else None
        if gain_ref is not None:
            y = _head_rmsnorm(y, bd, gain_ref[...], hd)
        if rot:
            y = rope(y)
        store(y)


def _cumsum_kernel(lf_ref, tri_ref, pk_ref, onek_ref, pq_ref, oneq_ref, cqt_ref, ck_ref):
    tri = tri_ref[...]
    n_blocks = lf_ref.shape[0] // CUMSUM_BLOCK
    n_hp = ck_ref.shape[0]
    carry = jnp.zeros((1, LANES), F32)
    for blk in range(n_blocks):
        rows = slice(blk * CUMSUM_BLOCK, (blk + 1) * CUMSUM_BLOCK)
        hi, mid, lo = _split3(lf_ref[rows, :])
        c = _dot(tri, hi) + _dot(tri, mid) + _dot(tri, lo) + carry
        carry = c[CUMSUM_BLOCK - 1:CUMSUM_BLOCK, :]
        c = c * LOG2E
        pieces = jnp.concatenate(_split3(c), axis=1)
        ke = _dot(pieces, pk_ref[...]) + onek_ref[...]
        qe = _dot_nt(pq_ref[...], pieces) + oneq_ref[...]
        for hp in range(n_hp):
            ck_ref[hp, rows, :] = ke[:, hp * LANES:(hp + 1) * LANES].astype(BF16)
            for e in range(2):
                r0 = (2 * hp + e) * DECAY_ROWS
                cqt_ref[hp, e, :, rows] = qe[r0:r0 + DECAY_ROWS, :].astype(BF16)


def _fox_kernel(qt_ref, cqt_ref, k_ref, ck_ref, vt_ref, o_ref, qa_ref, st_ref, bm_ref, m_ref, acc_ref):
    blk = pl.program_id(2)
    tq = FOX_Q_TILE
    tk = FOX_K_TILE
    hd = HEAD_DIM
    cw = FOX_COLS
    zq = jnp.zeros((hd, tq), BF16)
    zd = jnp.zeros((DECAY_ROWS, tq), BF16)
    zpad = jnp.zeros((MXU_COLS - 2 * hd - 2 * DECAY_ROWS, tq), BF16)
    for e in range(2):
        parts = [zq, zq, zd, zd, zpad]
        parts[e] = qt_ref[e * hd:(e + 1) * hd, :]
        parts[2 + e] = cqt_ref[e]
        qa_ref[e] = jnp.concatenate(parts, axis=0)
    units = [(e, c0) for e in range(2) for c0 in range(0, tq, cw)]

    def key_block(key_start):
        rows = pl.ds(pl.multiple_of(key_start, tk), tk)
        return jnp.concatenate([k_ref[rows, :], ck_ref[rows, :]], axis=1)

    def value_block(key_start):
        return vt_ref[:, pl.ds(pl.multiple_of(key_start, tk), tk)]

    def issue_scores(buf, u, k_aug):
        e, c0 = units[u]
        st = _dot(k_aug, qa_ref[e, :, c0:c0 + cw])
        st_ref[buf, u] = st
        bm_ref[buf, u] = jnp.max(st, axis=0, keepdims=True)

    def update(buf, u, vt, key_off):
        e, c0 = units[u]
        st = st_ref[buf, u]
        if key_off is not None and key_off + tk - 1 > c0:
            key = lax.broadcasted_iota(jnp.int32, st.shape, 0) + key_off
            qry = lax.broadcasted_iota(jnp.int32, st.shape, 1) + c0
            st = jnp.where(key <= qry, st, -jnp.inf)
            block_max = jnp.max(st, axis=0, keepdims=True)
        else:
            block_max = bm_ref[buf, u]
        m = m_ref[u]
        m_new = jnp.maximum(m, block_max)
        alpha = jnp.exp2(m - m_new)
        p = jnp.exp2(st - m_new).astype(BF16)
        vt_aug = jnp.concatenate([vt[e * hd:(e + 1) * hd, :], ones_rows], axis=0)
        acc_ref[u] = alpha * acc_ref[u] + _dot(vt_aug, p)
        m_ref[u] = m_new

    ones_rows = jnp.ones((FOX_DEN_ROWS, tk), BF16)
    m_ref[...] = jnp.full(m_ref.shape, -jnp.inf, F32)
    acc_ref[...] = jnp.zeros(acc_ref.shape, F32)
    k_first = key_block(0)
    for u in range(len(units)):
        issue_scores(0, u, k_first)

    per_trip = FOX_BLOCKS_PER_TRIP
    assert per_trip % 2 == 0 and (tq // tk) % per_trip == 0

    def block_group(jg, carry):
        for step in range(per_trip):
            j = per_trip * jg + step
            k_next = key_block((j + 1) * tk)
            vt = value_block(j * tk)
            for u in range(min(FOX_LOOKAHEAD, len(units))):
                issue_scores(1 - step % 2, u, k_next)
            for u in range(len(units)):
                if u + FOX_LOOKAHEAD < len(units):
                    issue_scores(1 - step % 2, u + FOX_LOOKAHEAD, k_next)
                update(step % 2, u, vt, None)
        return carry

    lax.fori_loop(0, blk * (tq // tk // per_trip), block_group, 0)
    for jj in range(tq // tk):
        key_off = jj * tk
        active = [u for u, (e, c0) in enumerate(units) if c0 + cw > key_off]
        nxt_off = key_off + tk
        nxt_active = [u for u, (e, c0) in enumerate(units) if c0 + cw > nxt_off] if nxt_off < tq else []
        if nxt_active:
            k_next = key_block(blk * tq + nxt_off)
        vt = value_block(blk * tq + key_off)
        for u in active:
            if u in nxt_active:
                issue_scores((jj + 1) % 2, u, k_next)
            update(jj % 2, u, vt, key_off)
    per_head = []
    for e in range(2):
        cols = [acc_ref[u, :hd, :] * (1.0 / acc_ref[u, hd:hd + 1, :]) for u, unit in enumerate(units) if unit[0] == e]
        per_head.append(jnp.concatenate(cols, axis=1))
    ot = jnp.concatenate(per_head, axis=0)
    o_ref[...] = ot.T.astype(o_ref.dtype)


def _dilated_kernel(hm_ref, q0_ref, k0_ref, kp0_ref, v0_ref, vp0_ref, q1_ref, k1_ref, kp1_ref, v1_ref, vp1_ref,
                    q2_ref, k2_ref, kp2_ref, v2_ref, vp2_ref, ya_ref, o_scr, l_scr):
    jt = pl.program_id(1)
    c = DIL_BACK
    w = DIL_OUT
    nh = DIL_HEADS
    tile = ya_ref.shape[0]
    qi = lax.broadcasted_iota(jnp.int32, (nh * c, 2 * c), 0) & (c - 1)
    kj = lax.broadcasted_iota(jnp.int32, (nh * c, 2 * c), 1)
    dist = qi + c - kj
    in_band = (dist >= 0) & (dist <= DIL_BACK)
    in_band_first = in_band & (kj >= jnp.where(jt > 0, 0, c))
    lane = lax.broadcasted_iota(jnp.int32, (c, w), 1)
    in_head = [(lane >= hh * HEAD_DIM) & (lane < (hh + 1) * HEAD_DIM) for hh in range(nh)]

    def attend(gi, d, items, mask):
        def scores(item):
            q, kk, _, _ = item
            return _dot_nt(jnp.concatenate([q * hm_ref[hh] for hh in range(nh)], axis=0), kk)

        s_next = scores(items[0])
        for idx, (_, _, vv, tok0) in enumerate(items):
            s = s_next
            s_next = scores(items[idx + 1]) if idx + 1 < len(items) else None
            s = jnp.where(mask, s, -jnp.inf)
            m = jnp.max(s, axis=-1, keepdims=True)
            p = jnp.exp2(s - m)
            den = jnp.sum(p, axis=-1, keepdims=True)
            o4 = _dot(p.astype(BF16), vv) * (1.0 / den)
            lse4 = m + jnp.log2(den)
            o = jnp.zeros((c, w), F32)
            lse = jnp.zeros((c, w), F32)
            for hh in range(nh):
                rows = slice(hh * c, (hh + 1) * c)
                o = jnp.where(in_head[hh], o4[rows], o)
                lse = jnp.where(in_head[hh], lse4[rows], lse)
            for half in range(w // LANES):
                lanes = slice(half * LANES, (half + 1) * LANES)
                o_scr[gi, half, pl.ds(tok0, c, stride=d), :] = o[:, lanes]
                l_scr[gi, half, pl.ds(tok0, c, stride=d), :] = lse[:, lanes]

    groups = ((q0_ref, k0_ref, kp0_ref, v0_ref, vp0_ref), (q1_ref, k1_ref, kp1_ref, v1_ref, vp1_ref),
              (q2_ref, k2_ref, kp2_ref, v2_ref, vp2_ref))
    for gi, (q_ref, k_ref, kp_ref, v_ref, vp_ref) in enumerate(groups):
        d = q_ref.shape[0]
        n_sb = q_ref.shape[1] // c
        res_chunk = min(d, DIL_CHAINS)
        assert d % res_chunk == 0

        def first_item(r, q_ref=q_ref, k_ref=k_ref, kp_ref=kp_ref, v_ref=v_ref, vp_ref=vp_ref):
            kk = jnp.concatenate([kp_ref[r], k_ref[r, 0:c]], axis=0)
            vv = jnp.concatenate([vp_ref[r], v_ref[r, 0:c]], axis=0)
            return q_ref[r, 0:c], kk, vv, r

        def later_item(r, sb, d=d, q_ref=q_ref, k_ref=k_ref, v_ref=v_ref):
            start = pl.multiple_of(sb * c, c)
            window = pl.ds(start - c, 2 * c)
            return q_ref[r, pl.ds(start, c)], k_ref[r, window], v_ref[r, window], sb * (c * d) + r

        def first_chunk(it, carry, gi=gi, d=d, res_chunk=res_chunk, first_item=first_item):
            attend(gi, d, [first_item(it * res_chunk + i) for i in range(res_chunk)], in_band_first)
            return carry

        if d == res_chunk:
            first_chunk(0, 0)
        else:
            lax.fori_loop(0, d // res_chunk, first_chunk, 0)
        if n_sb > 1:
            sb_chunk = max(1, DIL_LATER_ITEMS // res_chunk)
            sb_chunk = max(k for k in range(1, sb_chunk + 1) if (n_sb - 1) % k == 0)
            assert d == res_chunk

            def later_chunk(it, carry, gi=gi, d=d, sb_chunk=sb_chunk, later_item=later_item):
                attend(gi, d, [later_item(r, 1 + it * sb_chunk + i) for i in range(sb_chunk) for r in range(d)],
                       in_band)
                return carry

            lax.fori_loop(0, (n_sb - 1) // sb_chunk, later_chunk, 0)

    def merge(ch, carry):
        rows = pl.ds(pl.multiple_of(ch * DIL_MERGE_ROWS, DIL_MERGE_ROWS), DIL_MERGE_ROWS)
        for half in range(w // LANES):
            l = [l_scr[g, half, rows, :] for g in range(N_DIL_GROUPS)]
            mx = jnp.maximum(jnp.maximum(l[0], l[1]), l[2])
            e = [jnp.exp2(lg - mx) for lg in l]
            num = e[0] * o_scr[0, half, rows, :] + e[1] * o_scr[1, half, rows, :] + e[2] * o_scr[2, half, rows, :]
            ya_ref[rows, half * LANES:(half + 1) * LANES] = (num / (e[0] + e[1] + e[2])).astype(ya_ref.dtype)
        return carry

    lax.fori_loop(0, tile // DIL_MERGE_ROWS, merge, 0)


def _memkv_kernel(mem_ref, g_ref, w_ref, gk_ref, km_ref, vm_ref):
    mn = _row_rmsnorm(mem_ref[...], g_ref[...]).astype(BF16)
    kv = _dot(mn, w_ref[...])
    gk = gk_ref[...]
    for hh in range(MEM_HEADS):
        cols = slice(hh * MEM_HEAD_DIM, (hh + 1) * MEM_HEAD_DIM)
        km_ref[:, cols] = _row_rmsnorm(kv[:, cols], gk).astype(km_ref.dtype)
    vm_ref[...] = kv[:, MEM_WIDTH:].astype(vm_ref.dtype)


def _merge_kernel(x_ref, h_ref, ya_ref, yb_ref, qm_ref, km_ref, vm_ref,
                  wg_ref, bg_ref, wa_ref, wb_ref, wm_ref, wo_ref, out_ref, merged_ref):
    ya = ya_ref[...]
    ym = []
    for hh in range(MEM_HEADS):
        cols = slice(hh * MEM_HEAD_DIM, (hh + 1) * MEM_HEAD_DIM)
        s = _dot_nt(qm_ref[:, cols], km_ref[:, cols])
        p = jnp.exp(s - jnp.max(s, axis=-1, keepdims=True))
        den = jnp.sum(p, axis=-1, keepdims=True)
        ym.append((_dot(p.astype(BF16), vm_ref[:, cols]) / den).astype(BF16))
    ym = jnp.concatenate(ym, axis=1)
    yb = yb_ref[...]
    h = h_ref[...]
    cw = 2 * MXU_COLS
    for c in range(0, D_MODEL, cw):
        merged = None
        for k, (y, w_ref) in enumerate(((ya, wa_ref), (yb, wb_ref), (ym, wm_ref))):
            gcol = k * D_MODEL + c
            gate = jax.nn.sigmoid(_dot(h, wg_ref[:, gcol:gcol + cw]) + bg_ref[:, gcol:gcol + cw])
            term = gate * _dot(y, w_ref[:, c:c + cw])
            merged = term if merged is None else merged + term
        merged_ref[:, c:c + cw] = merged.astype(BF16)
    out_ref[...] = x_ref[...] + _dot(merged_ref[...], wo_ref[...])


def _mlp_kernel(x_ref, g_ref, wu_ref, wd_ref, out_ref, h2_ref):
    x = x_ref[...]
    h2_ref[...] = _row_rmsnorm(x, g_ref[...]).astype(BF16)
    cw = D_MODEL
    acc = x
    for c in range(0, D_FF, cw):
        u = jnp.maximum(_dot(h2_ref[...], wu_ref[:, c:c + cw]), 0.0)
        acc = acc + _dot((u * u).astype(BF16), wd_ref[c:c + cw, :])
    out_ref[...] = acc


def _full(shape):
    return pl.BlockSpec(shape, lambda *_: (0,) * len(shape), pipeline_mode=pl.Buffered(1))


def _params(*sem):
    return pltpu.CompilerParams(dimension_semantics=sem)


def _blockdiag(width, head_dim):
    r = jnp.arange(width) // head_dim
    return (r[:, None] == r[None, :]).astype(BF16)


def _decay_placement(n_hp):
    pk = np.zeros((3 * LANES, n_hp * LANES), np.float32)
    onek = np.zeros((1, n_hp * LANES), np.float32)
    pq = np.zeros((n_hp * 2 * DECAY_ROWS, 3 * LANES), np.float32)
    oneq = np.zeros((n_hp * 2 * DECAY_ROWS, CUMSUM_BLOCK), np.float32)
    for hp in range(n_hp):
        for e in range(2):
            head = 2 * hp + e
            kcol = hp * LANES + DECAY_ROWS * e
            qrow = (2 * hp + e) * DECAY_ROWS
            for t in range(3):
                onek[0, kcol + t] = 1.0
                pk[t * LANES + head, kcol + 3 + t] = -1.0
                pq[qrow + t, t * LANES + head] = 1.0
                oneq[qrow + 3 + t, :] = 1.0
    return jnp.asarray(pk, BF16), jnp.asarray(onek), jnp.asarray(pq, BF16), jnp.asarray(oneq)


def _rope_tables(seq, width):
    half = ROT_DIM // 2
    inv_freq = ROPE_THETA ** (-jnp.arange(0, ROT_DIM, 2, dtype=F32) / ROT_DIM)
    ang = jnp.arange(seq, dtype=F32)[:, None] * inv_freq[None, :]
    cos, sin = jnp.cos(ang), jnp.sin(ang)
    ones = jnp.ones((seq, HEAD_DIM - ROT_DIM), F32)
    zeros = jnp.zeros((seq, HEAD_DIM - ROT_DIM), F32)
    zhalf = jnp.zeros((seq, half), F32)
    rc = jnp.concatenate([cos, cos, ones], axis=1)
    rs1 = jnp.concatenate([-sin, zhalf, zeros], axis=1)
    rs2 = jnp.concatenate([zhalf, sin, zeros], axis=1)
    reps = width // HEAD_DIM
    return tuple(jnp.tile(t, (1, reps)) for t in (rc, rs1, rs2))


def kernel(x, mem, g_mix, w_in, b_f, g_qA, g_kA, g_qB, g_kB, g_mem, w_mem_kv, g_qM, g_kM, w_gate, b_gate,
           w_br_a, w_br_b, w_br_m, w_out, g_mlp, w_up, w_down):
    B, T, D = x.shape
    assert D == D_MODEL and w_in.shape[0] == 1, "single-layer kernel"
    N = B * T
    mem_len = mem.shape[1]
    tm = ROW_TILE
    assert T % tm == 0 and T % FOX_Q_TILE == 0 and FOX_Q_TILE % FOX_K_TILE == 0
    x2d = x.reshape(N, D)

    wi = w_in[0]
    offs = [0]
    for wdt in (DIL_WIDTH, DIL_WIDTH, DIL_WIDTH, FOX_WIDTH, FOX_WIDTH, FOX_WIDTH, FOX_HEADS, MEM_WIDTH):
        offs.append(offs[-1] + wdt)
    w_fl = jnp.pad(wi[:, offs[6]:offs[7]], ((0, 0), (0, FL_PAD - FOX_HEADS)))
    w_all = jnp.concatenate([wi[:, :offs[6]], wi[:, offs[7]:], w_fl], axis=1).astype(BF16)
    assert offs[6] == INPROJ_COL["qm"] and w_all.shape[1] == INPROJ_COL["fl"] + FL_PAD
    n_cols = w_all.shape[1]
    bf_pad = jnp.pad(b_f[0], (0, FL_PAD - FOX_HEADS)).reshape(1, FL_PAD)
    cw = MXU_COLS
    q_scale = HEAD_DIM ** -0.5
    gqa = (jnp.tile(g_qA[0], cw // HEAD_DIM) * (q_scale * LOG2E)).reshape(1, cw)
    gka = jnp.tile(g_kA[0], cw // HEAD_DIM).reshape(1, cw)
    gqb = (jnp.tile(g_qB[0], cw // HEAD_DIM) * (q_scale * LOG2E)).reshape(1, cw)
    gkb = jnp.tile(g_kB[0], cw // HEAD_DIM).reshape(1, cw)
    gqm = (jnp.tile(g_qM[0], cw // MEM_HEAD_DIM) * MEM_HEAD_DIM ** -0.5).reshape(1, cw)
    rc, rs1, rs2 = _rope_tables(T, cw)
    bd64 = _blockdiag(cw, HEAD_DIM)
    bd128 = _blockdiag(cw, MEM_HEAD_DIM)

    row = lambda width: pl.BlockSpec((tm, width), lambda i: (i, 0))
    tm1 = INPROJ_ROW_TILE
    per_seq = T // tm1
    assert T % tm1 == 0
    row1 = lambda width: pl.BlockSpec((tm1, width), lambda i: (i, 0))
    rope_spec = pl.BlockSpec((tm1, cw), lambda i: (i % per_seq, 0))
    tspec = pl.BlockSpec((None, FOX_WIDTH, tm1), lambda i: (i // per_seq, 0, i % per_seq))
    tshape = jax.ShapeDtypeStruct((B, FOX_WIDTH, T), BF16)
    rshape = lambda width: jax.ShapeDtypeStruct((N, width), BF16)
    dils = [d for _, d in DIL_GROUPS]
    assert all(win // d == DIL_BACK and tm1 % d == 0 and (tm1 // d) % 16 == 0 for win, d in DIL_GROUPS)
    dspecs = [pl.BlockSpec((None, d, tm1 // d, DIL_OUT), lambda i: (i // per_seq, 0, i % per_seq, 0))
              for d in dils]
    dshapes = [jax.ShapeDtypeStruct((B, d, T // d, DIL_OUT), BF16) for d in dils]
    outs = pl.pallas_call(
        _inproj_kernel,
        grid=(N // tm1,),
        in_specs=[row1(D), _full((1, D)), _full((D, n_cols)), _full((cw, cw)), _full((cw, cw)),
                  _full((1, cw)), _full((1, cw)), _full((1, cw)), _full((1, cw)), _full((1, cw)),
                  _full((1, FL_PAD)), rope_spec, rope_spec, rope_spec],
        out_specs=[row1(D)] + dspecs * 3 + [tspec, row1(FOX_WIDTH), tspec, row1(MEM_WIDTH), row1(FL_PAD)],
        out_shape=[rshape(D)] + dshapes * 3 + [tshape, rshape(FOX_WIDTH), tshape, rshape(MEM_WIDTH),
                                               jax.ShapeDtypeStruct((N, FL_PAD), F32)],
        scratch_shapes=[pltpu.VMEM((cw // LANES, tm1, LANES), F32)],
        compiler_params=_params("parallel"),
        name="inproj",
    )(x2d, g_mix, w_all, bd64, bd128, gqa, gka, gqb, gkb, gqm, bf_pad, rc, rs1, rs2)
    h, qa_g, ka_g, va_g = outs[0], outs[1:4], outs[4:7], outs[7:10]
    qbt, kb, vbt, qm, logf = outs[10:]

    n_hp = FOX_HEADS // 2
    tri = (jnp.arange(CUMSUM_BLOCK)[:, None] >= jnp.arange(CUMSUM_BLOCK)[None, :]).astype(BF16)
    pk, onek, pq, oneq = _decay_placement(n_hp)
    cqt, ck_ext = pl.pallas_call(
        _cumsum_kernel,
        grid=(B,),
        in_specs=[pl.BlockSpec((T, FL_PAD), lambda b: (b, 0)), _full((CUMSUM_BLOCK, CUMSUM_BLOCK)),
                  _full(pk.shape), _full(onek.shape), _full(pq.shape), _full(oneq.shape)],
        out_specs=[pl.BlockSpec((None, n_hp, 2, DECAY_ROWS, T), lambda b: (b, 0, 0, 0, 0)),
                   pl.BlockSpec((None, n_hp, T, LANES), lambda b: (b, 0, 0, 0))],
        out_shape=[jax.ShapeDtypeStruct((B, n_hp, 2, DECAY_ROWS, T), BF16),
                   jax.ShapeDtypeStruct((B, n_hp, T, LANES), BF16)],
        compiler_params=_params("parallel"),
        name="decay_cumsum",
    )(logf, tri, pk, onek, pq, oneq)

    tq = FOX_Q_TILE
    nq = T // tq
    n_units = 2 * tq // FOX_COLS
    yb = pl.pallas_call(
        _fox_kernel,
        grid=(B, n_hp, nq),
        in_specs=[pl.BlockSpec((None, LANES, tq), lambda b, hp, i: (b, hp, i)),
                  pl.BlockSpec((None, None, 2, DECAY_ROWS, tq), lambda b, hp, i: (b, hp, 0, 0, i)),
                  pl.BlockSpec((T, LANES), lambda b, hp, i: (b, hp)),
                  pl.BlockSpec((None, None, T, LANES), lambda b, hp, i: (b, hp, 0, 0)),
                  pl.BlockSpec((None, LANES, T), lambda b, hp, i: (b, hp, 0))],
        out_specs=pl.BlockSpec((tq, LANES), lambda b, hp, i: (b * nq + i, hp)),
        out_shape=jax.ShapeDtypeStruct((N, FOX_WIDTH), BF16),
        scratch_shapes=[pltpu.VMEM((2, MXU_COLS, tq), BF16),
                        pltpu.VMEM((2, n_units, FOX_K_TILE, FOX_COLS), F32),
                        pltpu.VMEM((2, n_units, 1, FOX_COLS), F32),
                        pltpu.VMEM((n_units, 1, FOX_COLS), F32),
                        pltpu.VMEM((n_units, HEAD_DIM + FOX_DEN_ROWS, FOX_COLS), F32)],
        compiler_params=_params("parallel", "parallel", "arbitrary"),
        name="fox_attention",
    )(qbt, cqt, kb, ck_ext, vbt)

    tile = DIL_TILE
    assert T % tile == 0 and all((tile // d) % DIL_BACK == 0 for d in dils)
    lane_head = jnp.arange(DIL_OUT) // HEAD_DIM
    head_mask = jnp.broadcast_to((lane_head[None, :] == jnp.arange(DIL_HEADS)[:, None])[:, None, :],
                                 (DIL_HEADS, DIL_BACK, DIL_OUT)).astype(BF16)
    dil_specs, dil_args = [], []
    for gi, d in enumerate(dils):
        per_tile = tile // d // DIL_BACK
        cur = pl.BlockSpec((None, d, tile // d, DIL_OUT), lambda b, j: (b, 0, j, 0))
        prev = pl.BlockSpec((None, d, DIL_BACK, DIL_OUT),
                            lambda b, j, per_tile=per_tile: (b, 0, jnp.maximum(j * per_tile - 1, 0), 0))
        dil_specs += [cur, cur, prev, cur, prev]
        dil_args += [qa_g[gi], ka_g[gi], ka_g[gi], va_g[gi], va_g[gi]]
    ya = pl.pallas_call(
        _dilated_kernel,
        grid=(B, T // tile),
        in_specs=[_full((DIL_HEADS, DIL_BACK, DIL_OUT))] + dil_specs,
        out_specs=pl.BlockSpec((tile, DIL_OUT), lambda b, j: (b * (T // tile) + j, 0)),
        out_shape=jax.ShapeDtypeStruct((N, DIL_OUT), BF16),
        scratch_shapes=[pltpu.VMEM((N_DIL_GROUPS, DIL_OUT // LANES, tile, LANES), F32)] * 2,
        compiler_params=_params("parallel", "arbitrary"),
        name="dilated_attention",
    )(head_mask, *dil_args)

    gkm = g_kM[0].reshape(1, MEM_HEAD_DIM)
    km, vm = pl.pallas_call(
        _memkv_kernel,
        grid=(B,),
        in_specs=[pl.BlockSpec((None, mem_len, D), lambda b: (b, 0, 0)), _full((1, D)),
                  _full((D, 2 * MEM_WIDTH)), _full((1, MEM_HEAD_DIM))],
        out_specs=[pl.BlockSpec((None, mem_len, MEM_WIDTH), lambda b: (b, 0, 0))] * 2,
        out_shape=[jax.ShapeDtypeStruct((B, mem_len, MEM_WIDTH), BF16)] * 2,
        compiler_params=_params("parallel"),
        name="mem_kv",
    )(mem, g_mem, w_mem_kv[0].astype(BF16), gkm)

    mem_spec = pl.BlockSpec((None, mem_len, MEM_WIDTH), lambda i: (i // (T // tm), 0, 0))
    x_mid = pl.pallas_call(
        _merge_kernel,
        grid=(N // tm,),
        in_specs=[row(D), row(D), row(DIL_OUT), row(FOX_WIDTH), row(MEM_WIDTH), mem_spec, mem_spec,
                  _full((D, 3 * D)), _full((1, 3 * D)), _full((DIL_OUT, D)), _full((FOX_WIDTH, D)),
                  _full((MEM_WIDTH, D)), _full((D, D))],
        out_specs=row(D),
        out_shape=jax.ShapeDtypeStruct((N, D), F32),
        scratch_shapes=[pltpu.VMEM((tm, D), BF16)],
        compiler_params=_params("parallel"),
        name="merge_outproj",
    )(x2d, h, ya, yb, qm, km, vm, w_gate[0].astype(BF16), b_gate, w_br_a[0].astype(BF16),
      w_br_b[0].astype(BF16), w_br_m[0].astype(BF16), w_out[0].astype(BF16))

    out = pl.pallas_call(
        _mlp_kernel,
        grid=(N // tm,),
        in_specs=[row(D), _full((1, D)), _full((D, D_FF)), _full((D_FF, D))],
        out_specs=row(D),
        out_shape=jax.ShapeDtypeStruct((N, D), F32),
        scratch_shapes=[pltpu.VMEM((tm, D), BF16)],
        compiler_params=_params("parallel"),
        name="mlp",
    )(x_mid, g_mlp, w_up[0].astype(BF16), w_down[0].astype(BF16))
    return out.reshape(B, T, D)
```

```python
import functools

import jax
import jax.numpy as jnp
import numpy as np
from jax import lax
from jax.experimental import pallas as pl
from jax.experimental.pallas import tpu as pltpu

D_MODEL = 1024
HEAD_DIM = 64
DIL_GROUPS = ((128, 1), (512, 4), (2048, 16))
N_DIL_GROUPS = 3
DIL_HEADS = 4
DIL_WIDTH = N_DIL_GROUPS * DIL_HEADS * HEAD_DIM
DIL_OUT = DIL_HEADS * HEAD_DIM
DIL_BACK = 128
FOX_HEADS = 8
FOX_WIDTH = FOX_HEADS * HEAD_DIM
MEM_HEADS = 4
MEM_HEAD_DIM = 128
MEM_WIDTH = MEM_HEADS * MEM_HEAD_DIM
ROT_DIM = HEAD_DIM // 4
ROPE_THETA = 500000.0
D_FF = 4 * D_MODEL
EPS = 1e-6

LANES = 128
MXU_COLS = 256
FL_PAD = LANES

ROW_TILE = 1024
INPROJ_ROW_TILE = 512
INPROJ_COL = dict(qa=0, ka=DIL_WIDTH, va=2 * DIL_WIDTH, qb=3 * DIL_WIDTH, kb=3 * DIL_WIDTH + FOX_WIDTH,
                  vb=3 * DIL_WIDTH + 2 * FOX_WIDTH, qm=3 * DIL_WIDTH + 3 * FOX_WIDTH,
                  fl=3 * DIL_WIDTH + 3 * FOX_WIDTH + MEM_WIDTH)
INPROJ_ORDER = (("qa", 0), ("ka", 0), ("qa", 1), ("ka", 1), ("qa", 2), ("ka", 2), ("qb", 0), ("kb", 0),
                ("qb", 1), ("kb", 1), ("qm", 0), ("qm", 1), ("va", 2), ("va", 1), ("vb", 0), ("vb", 1), ("va", 0))
FOX_Q_TILE = 2048
FOX_K_TILE = 256
FOX_COLS = 512
FOX_DEN_ROWS = 16
FOX_LOOKAHEAD = 1
FOX_BLOCKS_PER_TRIP = 4
DECAY_ROWS = 16
CUMSUM_BLOCK = 256
DIL_TILE = 2048
DIL_MERGE_ROWS = 256
DIL_CHAINS = 8
DIL_LATER_ITEMS = 12

LOG2E = 1.4426950408889634
BF16 = jnp.bfloat16
F32 = jnp.float32
NT_DIMS = (((1,), (1,)), ((), ()))


def _dot(a, b):
    return jnp.dot(a, b, preferred_element_type=F32)


def _dot_nt(a, b):
    return lax.dot_general(a, b, NT_DIMS, preferred_element_type=F32)


def _row_rmsnorm(x, g):
    ms = jnp.mean(x * x, axis=-1, keepdims=True)
    return x * lax.rsqrt(ms + EPS) * g


def _head_rmsnorm(y, blockdiag, gain, head_dim):
    y2 = (y * y).astype(BF16)
    half = y.shape[0] // 2
    ss = jnp.concatenate([_dot(y2[:half], blockdiag), _dot(y2[half:], blockdiag)], axis=0)
    return y * lax.rsqrt(ss * (1.0 / head_dim) + EPS) * gain


def _split3(c):
    hi = c.astype(BF16)
    r1 = c - hi.astype(F32)
    mid = r1.astype(BF16)
    lo = (r1 - mid.astype(F32)).astype(BF16)
    return hi, mid, lo


def _inproj_kernel(x_ref, gmix_ref, w_ref, bd64_ref, bd128_ref, gqa_ref, gka_ref, gqb_ref, gkb_ref,
                   gqm_ref, bf_ref, rc_ref, rs1_ref, rs2_ref,
                   h_ref, qa0_ref, qa1_ref, qa2_ref, ka0_ref, ka1_ref, ka2_ref, va0_ref, va1_ref, va2_ref,
                   qbt_ref, kb_ref, vbt_ref, qm_ref, lf_ref, perm_ref):
    h = _row_rmsnorm(x_ref[...], gmix_ref[...]).astype(BF16)
    h_ref[...] = h
    bd64 = bd64_ref[...]
    bd128 = bd128_ref[...]
    rc = rc_ref[...]
    rs1 = rs1_ref[...]
    rs2 = rs2_ref[...]
    cw = MXU_COLS
    tm = x_ref.shape[0]

    def proj(col, width=cw):
        return _dot(h, w_ref[:, col:col + width])

    def rope(y):
        return y * rc + pltpu.roll(y, cw - ROT_DIM // 2, 1) * rs1 + pltpu.roll(y, ROT_DIM // 2, 1) * rs2

    def store_rows(out_ref, c, y):
        out_ref[:, c:c + cw] = y.astype(out_ref.dtype)

    def store_transposed(out_ref, c, y):
        out_ref[c:c + cw, :] = y.T.astype(out_ref.dtype)

    def store_by_residue(out_ref, y):
        d = out_ref.shape[0]
        if d == 1:
            out_ref[0] = y.astype(out_ref.dtype)
            return
        for half in range(cw // LANES):
            perm_ref[half] = y[:, half * LANES:(half + 1) * LANES]
        for r in range(d):
            out_ref[r] = jnp.concatenate(
                [perm_ref[half, pl.ds(r, tm // d, stride=d), :] for half in range(cw // LANES)],
                axis=1).astype(out_ref.dtype)

    rows_of = lambda ref: [(functools.partial(store_rows, ref, c)) for c in range(0, ref.shape[1], cw)]
    cols_of = lambda ref: [(functools.partial(store_transposed, ref, c)) for c in range(0, ref.shape[0], cw)]
    residues_of = lambda refs: [functools.partial(store_by_residue, ref) for ref in refs]
    segments = {
        "qa": (residues_of((qa0_ref, qa1_ref, qa2_ref)), gqa_ref, bd64, HEAD_DIM, True),
        "ka": (residues_of((ka0_ref, ka1_ref, ka2_ref)), gka_ref, bd64, HEAD_DIM, True),
        "va": (residues_of((va0_ref, va1_ref, va2_ref)), None, None, None, False),
        "qb": (cols_of(qbt_ref), gqb_ref, bd64, HEAD_DIM, False),
        "kb": (rows_of(kb_ref), gkb_ref, bd64, HEAD_DIM, False),
        "vb": (cols_of(vbt_ref), None, None, None, False),
        "qm": (rows_of(qm_ref), gqm_ref, bd128, MEM_HEAD_DIM, False)}
    chunks = [(INPROJ_COL[name] + piece * cw, segments[name][0][piece]) + segments[name][1:]
              for name, piece in INPROJ_ORDER]
    z = proj(INPROJ_COL["fl"], FL_PAD) + bf_ref[...]
    y_next = proj(chunks[0][0])
    lf_ref[...] = jnp.minimum(z, 0.0) - jnp.log1p(jnp.exp(-jnp.abs(z)))
    for idx, (_, store, gain_ref, bd, hd, rot) in enumerate(chunks):
        y = y_next
        y_next = proj(chunks[idx + 1][0]) if idx + 1 < len(chunks) else None
        if gain_ref is not None:
            y = _head_rmsnorm(y, bd, gain_ref[...], hd)
        if rot:
            y = rope(y)
        store(y)


def _cumsum_kernel(lf_ref, tri_ref, pk_ref, onek_ref, pq_ref, oneq_ref, cqt_ref, ck_ref):
    tri = tri_ref[...]
    n_blocks = lf_ref.shape[0] // CUMSUM_BLOCK
    n_hp = ck_ref.shape[0]
    carry = jnp.zeros((1, LANES), F32)
    for blk in range(n_blocks):
        rows = slice(blk * CUMSUM_BLOCK, (blk + 1) * CUMSUM_BLOCK)
        local = _dot(tri, jnp.concatenate(_split3(lf_ref[rows, :]), axis=1))
        c = local[:, :LANES] + local[:, LANES:2 * LANES] + local[:, 2 * LANES:] + carry
        carry = c[CUMSUM_BLOCK - 1:CUMSUM_BLOCK, :]
        c = c * LOG2E
        pieces = jnp.concatenate(_split3(c), axis=1)
        ke = _dot(pieces, pk_ref[...]) + onek_ref[...]
        qe = _dot_nt(pq_ref[...], pieces) + oneq_ref[...]
        for hp in range(n_hp):
            ck_ref[hp, rows, :] = ke[:, hp * LANES:(hp + 1) * LANES].astype(BF16)
            for e in range(2):
                r0 = (2 * hp + e) * DECAY_ROWS
                cqt_ref[hp, e, :, rows] = qe[r0:r0 + DECAY_ROWS, :].astype(BF16)


def _fox_kernel(qt_ref, cqt_ref, k_ref, ck_ref, vt_ref, o_ref, qa_ref, st_ref, bm_ref, m_ref, acc_ref):
    blk = pl.program_id(2)
    tq = FOX_Q_TILE
    tk = FOX_K_TILE
    hd = HEAD_DIM
    cw = FOX_COLS
    zq = jnp.zeros((hd, tq), BF16)
    zd = jnp.zeros((DECAY_ROWS, tq), BF16)
    zpad = jnp.zeros((MXU_COLS - 2 * hd - 2 * DECAY_ROWS, tq), BF16)
    for e in range(2):
        parts = [zq, zq, zd, zd, zpad]
        parts[e] = qt_ref[e * hd:(e + 1) * hd, :]
        parts[2 + e] = cqt_ref[e]
        qa_ref[e] = jnp.concatenate(parts, axis=0)
    units = [(e, c0) for e in range(2) for c0 in range(0, tq, cw)]

    def key_block(key_start):
        rows = pl.ds(pl.multiple_of(key_start, tk), tk)
        return jnp.concatenate([k_ref[rows, :], ck_ref[rows, :]], axis=1)

    def value_block(key_start):
        return vt_ref[:, pl.ds(pl.multiple_of(key_start, tk), tk)]

    def first_visible(u, key_off):
        c0 = units[u][1]
        return cw // 2 if key_off is not None and key_off >= c0 + cw // 2 else 0

    def issue_scores(buf, u, k_aug, key_off=None):
        e, c0 = units[u]
        lo = first_visible(u, key_off)
        st = _dot(k_aug, qa_ref[e, :, c0 + lo:c0 + cw])
        st_ref[buf, u, :, lo:] = st
        bm_ref[buf, u, :, lo:] = jnp.max(st, axis=0, keepdims=True)

    def update(buf, u, vt, key_off):
        e, c0 = units[u]
        lo = first_visible(u, key_off)
        st = st_ref[buf, u, :, lo:]
        if key_off is not None and key_off + tk - 1 > c0 + lo:
            key = lax.broadcasted_iota(jnp.int32, st.shape, 0) + key_off
            qry = lax.broadcasted_iota(jnp.int32, st.shape, 1) + (c0 + lo)
            st = jnp.where(key <= qry, st, -jnp.inf)
            block_max = jnp.max(st, axis=0, keepdims=True)
        else:
            block_max = bm_ref[buf, u, :, lo:]
        m = m_ref[u, :, lo:]
        m_new = jnp.maximum(m, block_max)
        alpha = jnp.exp2(m - m_new)
        p = jnp.exp2(st - m_new).astype(BF16)
        vt_aug = jnp.concatenate([vt[e * hd:(e + 1) * hd, :], ones_rows], axis=0)
        acc_ref[u, :, lo:] = alpha * acc_ref[u, :, lo:] + _dot(vt_aug, p)
        m_ref[u, :, lo:] = m_new

    ones_rows = jnp.ones((FOX_DEN_ROWS, tk), BF16)
    m_ref[...] = jnp.full(m_ref.shape, -jnp.inf, F32)
    acc_ref[...] = jnp.zeros(acc_ref.shape, F32)
    k_first = key_block(0)
    for u in range(len(units)):
        issue_scores(0, u, k_first)

    per_trip = FOX_BLOCKS_PER_TRIP
    assert per_trip % 2 == 0 and (tq // tk) % per_trip == 0

    def block_group(jg, carry):
        for step in range(per_trip):
            j = per_trip * jg + step
            k_next = key_block((j + 1) * tk)
            vt = value_block(j * tk)
            for u in range(min(FOX_LOOKAHEAD, len(units))):
                issue_scores(1 - step % 2, u, k_next)
            for u in range(len(units)):
                if u + FOX_LOOKAHEAD < len(units):
                    issue_scores(1 - step % 2, u + FOX_LOOKAHEAD, k_next)
                update(step % 2, u, vt, None)
        return carry

    lax.fori_loop(0, blk * (tq // tk // per_trip), block_group, 0)
    for jj in range(tq // tk):
        key_off = jj * tk
        active = [u for u, (e, c0) in enumerate(units) if c0 + cw > key_off]
        nxt_off = key_off + tk
        nxt_active = [u for u, (e, c0) in enumerate(units) if c0 + cw > nxt_off] if nxt_off < tq else []
        if nxt_active:
            k_next = key_block(blk * tq + nxt_off)
        vt = value_block(blk * tq + key_off)
        for u in active:
            if u in nxt_active:
                issue_scores((jj + 1) % 2, u, k_next, nxt_off)
            update(jj % 2, u, vt, key_off)
    per_head = []
    for e in range(2):
        cols = [acc_ref[u, :hd, :] * (1.0 / acc_ref[u, hd:hd + 1, :]) for u, unit in enumerate(units) if unit[0] == e]
        per_head.append(jnp.concatenate(cols, axis=1))
    ot = jnp.concatenate(per_head, axis=0)
    o_ref[...] = ot.T.astype(o_ref.dtype)


def _dilated_kernel(hm_ref, q0_ref, k0_ref, kp0_ref, v0_ref, vp0_ref, q1_ref, k1_ref, kp1_ref, v1_ref, vp1_ref,
                    q2_ref, k2_ref, kp2_ref, v2_ref, vp2_ref, ya_ref, o_scr, l_scr):
    jt = pl.program_id(1)
    c = DIL_BACK
    w = DIL_OUT
    nh = DIL_HEADS
    tile = ya_ref.shape[0]
    qi = lax.broadcasted_iota(jnp.int32, (nh * c, 2 * c), 0) & (c - 1)
    kj = lax.broadcasted_iota(jnp.int32, (nh * c, 2 * c), 1)
    dist = qi + c - kj
    in_band = (dist >= 0) & (dist <= DIL_BACK)
    in_band_first = in_band & (kj >= jnp.where(jt > 0, 0, c))
    lane = lax.broadcasted_iota(jnp.int32, (c, w), 1)
    in_head = [(lane >= hh * HEAD_DIM) & (lane < (hh + 1) * HEAD_DIM) for hh in range(nh)]

    def attend(gi, d, items, mask):
        def scores(item):
            q, kk, _, _ = item
            return _dot_nt(jnp.concatenate([q * hm_ref[hh] for hh in range(nh)], axis=0), kk)

        s_next = scores(items[0])
        for idx, (_, _, vv, tok0) in enumerate(items):
            s = s_next
            s_next = scores(items[idx + 1]) if idx + 1 < len(items) else None
            s = jnp.where(mask, s, -jnp.inf)
            m = jnp.max(s, axis=-1, keepdims=True)
            p = jnp.exp2(s - m)
            den = jnp.sum(p, axis=-1, keepdims=True)
            o4 = _dot(p.astype(BF16), vv) * (1.0 / den)
            lse4 = m + jnp.log2(den)
            o = jnp.zeros((c, w), F32)
            lse = jnp.zeros((c, w), F32)
            for hh in range(nh):
                rows = slice(hh * c, (hh + 1) * c)
                o = jnp.where(in_head[hh], o4[rows], o)
                lse = jnp.where(in_head[hh], lse4[rows], lse)
            for half in range(w // LANES):
                lanes = slice(half * LANES, (half + 1) * LANES)
                o_scr[gi, half, pl.ds(tok0, c, stride=d), :] = o[:, lanes]
                l_scr[gi, half, pl.ds(tok0, c, stride=d), :] = lse[:, lanes]

    groups = ((q0_ref, k0_ref, kp0_ref, v0_ref, vp0_ref), (q1_ref, k1_ref, kp1_ref, v1_ref, vp1_ref),
              (q2_ref, k2_ref, kp2_ref, v2_ref, vp2_ref))
    for gi, (q_ref, k_ref, kp_ref, v_ref, vp_ref) in enumerate(groups):
        d = q_ref.shape[0]
        n_sb = q_ref.shape[1] // c
        res_chunk = min(d, DIL_CHAINS)
        assert d % res_chunk == 0

        def first_item(r, q_ref=q_ref, k_ref=k_ref, kp_ref=kp_ref, v_ref=v_ref, vp_ref=vp_ref):
            kk = jnp.concatenate([kp_ref[r], k_ref[r, 0:c]], axis=0)
            vv = jnp.concatenate([vp_ref[r], v_ref[r, 0:c]], axis=0)
            return q_ref[r, 0:c], kk, vv, r

        def later_item(r, sb, d=d, q_ref=q_ref, k_ref=k_ref, v_ref=v_ref):
            start = pl.multiple_of(sb * c, c)
            window = pl.ds(start - c, 2 * c)
            return q_ref[r, pl.ds(start, c)], k_ref[r, window], v_ref[r, window], sb * (c * d) + r

        def first_chunk(it, carry, gi=gi, d=d, res_chunk=res_chunk, first_item=first_item):
            attend(gi, d, [first_item(it * res_chunk + i) for i in range(res_chunk)], in_band_first)
            return carry

        if d == res_chunk:
            first_chunk(0, 0)
        else:
            lax.fori_loop(0, d // res_chunk, first_chunk, 0)
        if n_sb > 1:
            sb_chunk = max(1, DIL_LATER_ITEMS // res_chunk)
            sb_chunk = max(k for k in range(1, sb_chunk + 1) if (n_sb - 1) % k == 0)
            assert d == res_chunk

            def later_chunk(it, carry, gi=gi, d=d, sb_chunk=sb_chunk, later_item=later_item):
                attend(gi, d, [later_item(r, 1 + it * sb_chunk + i) for i in range(sb_chunk) for r in range(d)],
                       in_band)
                return carry

            lax.fori_loop(0, (n_sb - 1) // sb_chunk, later_chunk, 0)

    def merge(ch, carry):
        rows = pl.ds(pl.multiple_of(ch * DIL_MERGE_ROWS, DIL_MERGE_ROWS), DIL_MERGE_ROWS)
        for half in range(w // LANES):
            l = [l_scr[g, half, rows, :] for g in range(N_DIL_GROUPS)]
            mx = jnp.maximum(jnp.maximum(l[0], l[1]), l[2])
            e = [jnp.exp2(lg - mx) for lg in l]
            num = e[0] * o_scr[0, half, rows, :] + e[1] * o_scr[1, half, rows, :] + e[2] * o_scr[2, half, rows, :]
            ya_ref[rows, half * LANES:(half + 1) * LANES] = (num / (e[0] + e[1] + e[2])).astype(ya_ref.dtype)
        return carry

    lax.fori_loop(0, tile // DIL_MERGE_ROWS, merge, 0)


def _memkv_kernel(mem_ref, g_ref, w_ref, gk_ref, km_ref, vm_ref):
    mn = _row_rmsnorm(mem_ref[...], g_ref[...]).astype(BF16)
    kv = _dot(mn, w_ref[...])
    gk = gk_ref[...]
    for hh in range(MEM_HEADS):
        cols = slice(hh * MEM_HEAD_DIM, (hh + 1) * MEM_HEAD_DIM)
        km_ref[:, cols] = _row_rmsnorm(kv[:, cols], gk).astype(km_ref.dtype)
    vm_ref[...] = kv[:, MEM_WIDTH:].astype(vm_ref.dtype)


def _merge_kernel(x_ref, h_ref, ya_ref, yb_ref, qm_ref, km_ref, vm_ref,
                  wg_ref, bg_ref, wa_ref, wb_ref, wm_ref, wo_ref, out_ref, merged_ref):
    ya = ya_ref[...]
    ym = []
    for hh in range(MEM_HEADS):
        cols = slice(hh * MEM_HEAD_DIM, (hh + 1) * MEM_HEAD_DIM)
        s = _dot_nt(qm_ref[:, cols], km_ref[:, cols])
        p = jnp.exp(s - jnp.max(s, axis=-1, keepdims=True))
        den = jnp.sum(p, axis=-1, keepdims=True)
        ym.append((_dot(p.astype(BF16), vm_ref[:, cols]) / den).astype(BF16))
    ym = jnp.concatenate(ym, axis=1)
    yb = yb_ref[...]
    h = h_ref[...]
    cw = 2 * MXU_COLS
    for c in range(0, D_MODEL, cw):
        merged = None
        for k, (y, w_ref) in enumerate(((ya, wa_ref), (yb, wb_ref), (ym, wm_ref))):
            gcol = k * D_MODEL + c
            gate = jax.nn.sigmoid(_dot(h, wg_ref[:, gcol:gcol + cw]) + bg_ref[:, gcol:gcol + cw])
            term = gate * _dot(y, w_ref[:, c:c + cw])
            merged = term if merged is None else merged + term
        merged_ref[:, c:c + cw] = merged.astype(BF16)
    out_ref[...] = x_ref[...] + _dot(merged_ref[...], wo_ref[...])


def _mlp_kernel(x_ref, g_ref, wu_ref, wd_ref, out_ref, h2_ref):
    x = x_ref[...]
    h2_ref[...] = _row_rmsnorm(x, g_ref[...]).astype(BF16)
    cw = D_MODEL
    acc = x
    for c in range(0, D_FF, cw):
        u = jnp.maximum(_dot(h2_ref[...], wu_ref[:, c:c + cw]), 0.0)
        acc = acc + _dot((u * u).astype(BF16), wd_ref[c:c + cw, :])
    out_ref[...] = acc


def _full(shape):
    return pl.BlockSpec(shape, lambda *_: (0,) * len(shape), pipeline_mode=pl.Buffered(1))


def _params(*sem):
    return pltpu.CompilerParams(dimension_semantics=sem)


def _blockdiag(width, head_dim):
    r = jnp.arange(width) // head_dim
    return (r[:, None] == r[None, :]).astype(BF16)


def _decay_placement(n_hp):
    pk = np.zeros((3 * LANES, n_hp * LANES), np.float32)
    onek = np.zeros((1, n_hp * LANES), np.float32)
    pq = np.zeros((n_hp * 2 * DECAY_ROWS, 3 * LANES), np.float32)
    oneq = np.zeros((n_hp * 2 * DECAY_ROWS, CUMSUM_BLOCK), np.float32)
    for hp in range(n_hp):
        for e in range(2):
            head = 2 * hp + e
            kcol = hp * LANES + DECAY_ROWS * e
            qrow = (2 * hp + e) * DECAY_ROWS
            for t in range(3):
                onek[0, kcol + t] = 1.0
                pk[t * LANES + head, kcol + 3 + t] = -1.0
                pq[qrow + t, t * LANES + head] = 1.0
                oneq[qrow + 3 + t, :] = 1.0
    return jnp.asarray(pk, BF16), jnp.asarray(onek), jnp.asarray(pq, BF16), jnp.asarray(oneq)


def _rope_tables(seq, width):
    half = ROT_DIM // 2
    inv_freq = ROPE_THETA ** (-jnp.arange(0, ROT_DIM, 2, dtype=F32) / ROT_DIM)
    ang = jnp.arange(seq, dtype=F32)[:, None] * inv_freq[None, :]
    cos, sin = jnp.cos(ang), jnp.sin(ang)
    ones = jnp.ones((seq, HEAD_DIM - ROT_DIM), F32)
    zeros = jnp.zeros((seq, HEAD_DIM - ROT_DIM), F32)
    zhalf = jnp.zeros((seq, half), F32)
    rc = jnp.concatenate([cos, cos, ones], axis=1)
    rs1 = jnp.concatenate([-sin, zhalf, zeros], axis=1)
    rs2 = jnp.concatenate([zhalf, sin, zeros], axis=1)
    reps = width // HEAD_DIM
    return tuple(jnp.tile(t, (1, reps)) for t in (rc, rs1, rs2))


def kernel(x, mem, g_mix, w_in, b_f, g_qA, g_kA, g_qB, g_kB, g_mem, w_mem_kv, g_qM, g_kM, w_gate, b_gate,
           w_br_a, w_br_b, w_br_m, w_out, g_mlp, w_up, w_down):
    B, T, D = x.shape
    assert D == D_MODEL and w_in.shape[0] == 1, "single-layer kernel"
    N = B * T
    mem_len = mem.shape[1]
    tm = ROW_TILE
    assert T % tm == 0 and T % FOX_Q_TILE == 0 and FOX_Q_TILE % FOX_K_TILE == 0
    x2d = x.reshape(N, D)

    wi = w_in[0]
    offs = [0]
    for wdt in (DIL_WIDTH, DIL_WIDTH, DIL_WIDTH, FOX_WIDTH, FOX_WIDTH, FOX_WIDTH, FOX_HEADS, MEM_WIDTH):
        offs.append(offs[-1] + wdt)
    w_fl = jnp.pad(wi[:, offs[6]:offs[7]], ((0, 0), (0, FL_PAD - FOX_HEADS)))
    w_all = jnp.concatenate([wi[:, :offs[6]], wi[:, offs[7]:], w_fl], axis=1).astype(BF16)
    assert offs[6] == INPROJ_COL["qm"] and w_all.shape[1] == INPROJ_COL["fl"] + FL_PAD
    n_cols = w_all.shape[1]
    bf_pad = jnp.pad(b_f[0], (0, FL_PAD - FOX_HEADS)).reshape(1, FL_PAD)
    cw = MXU_COLS
    q_scale = HEAD_DIM ** -0.5
    gqa = (jnp.tile(g_qA[0], cw // HEAD_DIM) * (q_scale * LOG2E)).reshape(1, cw)
    gka = jnp.tile(g_kA[0], cw // HEAD_DIM).reshape(1, cw)
    gqb = (jnp.tile(g_qB[0], cw // HEAD_DIM) * (q_scale * LOG2E)).reshape(1, cw)
    gkb = jnp.tile(g_kB[0], cw // HEAD_DIM).reshape(1, cw)
    gqm = (jnp.tile(g_qM[0], cw // MEM_HEAD_DIM) * MEM_HEAD_DIM ** -0.5).reshape(1, cw)
    rc, rs1, rs2 = _rope_tables(T, cw)
    bd64 = _blockdiag(cw, HEAD_DIM)
    bd128 = _blockdiag(cw, MEM_HEAD_DIM)

    row = lambda width: pl.BlockSpec((tm, width), lambda i: (i, 0))
    tm1 = INPROJ_ROW_TILE
    per_seq = T // tm1
    assert T % tm1 == 0
    row1 = lambda width: pl.BlockSpec((tm1, width), lambda i: (i, 0))
    rope_spec = pl.BlockSpec((tm1, cw), lambda i: (i % per_seq, 0))
    tspec = pl.BlockSpec((None, FOX_WIDTH, tm1), lambda i: (i // per_seq, 0, i % per_seq))
    tshape = jax.ShapeDtypeStruct((B, FOX_WIDTH, T), BF16)
    rshape = lambda width: jax.ShapeDtypeStruct((N, width), BF16)
    dils = [d for _, d in DIL_GROUPS]
    assert all(win // d == DIL_BACK and tm1 % d == 0 and (tm1 // d) % 16 == 0 for win, d in DIL_GROUPS)
    dspecs = [pl.BlockSpec((None, d, tm1 // d, DIL_OUT), lambda i: (i // per_seq, 0, i % per_seq, 0))
              for d in dils]
    dshapes = [jax.ShapeDtypeStruct((B, d, T // d, DIL_OUT), BF16) for d in dils]
    outs = pl.pallas_call(
        _inproj_kernel,
        grid=(N // tm1,),
        in_specs=[row1(D), _full((1, D)), _full((D, n_cols)), _full((cw, cw)), _full((cw, cw)),
                  _full((1, cw)), _full((1, cw)), _full((1, cw)), _full((1, cw)), _full((1, cw)),
                  _full((1, FL_PAD)), rope_spec, rope_spec, rope_spec],
        out_specs=[row1(D)] + dspecs * 3 + [tspec, row1(FOX_WIDTH), tspec, row1(MEM_WIDTH), row1(FL_PAD)],
        out_shape=[rshape(D)] + dshapes * 3 + [tshape, rshape(FOX_WIDTH), tshape, rshape(MEM_WIDTH),
                                               jax.ShapeDtypeStruct((N, FL_PAD), F32)],
        scratch_shapes=[pltpu.VMEM((cw // LANES, tm1, LANES), F32)],
        compiler_params=_params("parallel"),
        name="inproj",
    )(x2d, g_mix, w_all, bd64, bd128, gqa, gka, gqb, gkb, gqm, bf_pad, rc, rs1, rs2)
    h, qa_g, ka_g, va_g = outs[0], outs[1:4], outs[4:7], outs[7:10]
    qbt, kb, vbt, qm, logf = outs[10:]

    n_hp = FOX_HEADS // 2
    tri = (jnp.arange(CUMSUM_BLOCK)[:, None] >= jnp.arange(CUMSUM_BLOCK)[None, :]).astype(BF16)
    pk, onek, pq, oneq = _decay_placement(n_hp)
    cqt, ck_ext = pl.pallas_call(
        _cumsum_kernel,
        grid=(B,),
        in_specs=[pl.BlockSpec((T, FL_PAD), lambda b: (b, 0)), _full((CUMSUM_BLOCK, CUMSUM_BLOCK)),
                  _full(pk.shape), _full(onek.shape), _full(pq.shape), _full(oneq.shape)],
        out_specs=[pl.BlockSpec((None, n_hp, 2, DECAY_ROWS, T), lambda b: (b, 0, 0, 0, 0)),
                   pl.BlockSpec((None, n_hp, T, LANES), lambda b: (b, 0, 0, 0))],
        out_shape=[jax.ShapeDtypeStruct((B, n_hp, 2, DECAY_ROWS, T), BF16),
                   jax.ShapeDtypeStruct((B, n_hp, T, LANES), BF16)],
        compiler_params=_params("parallel"),
        name="decay_cumsum",
    )(logf, tri, pk, onek, pq, oneq)

    tq = FOX_Q_TILE
    nq = T // tq
    n_units = 2 * tq // FOX_COLS
    yb = pl.pallas_call(
        _fox_kernel,
        grid=(B, n_hp, nq),
        in_specs=[pl.BlockSpec((None, LANES, tq), lambda b, hp, i: (b, hp, i)),
                  pl.BlockSpec((None, None, 2, DECAY_ROWS, tq), lambda b, hp, i: (b, hp, 0, 0, i)),
                  pl.BlockSpec((T, LANES), lambda b, hp, i: (b, hp)),
                  pl.BlockSpec((None, None, T, LANES), lambda b, hp, i: (b, hp, 0, 0)),
                  pl.BlockSpec((None, LANES, T), lambda b, hp, i: (b, hp, 0))],
        out_specs=pl.BlockSpec((tq, LANES), lambda b, hp, i: (b * nq + i, hp)),
        out_shape=jax.ShapeDtypeStruct((N, FOX_WIDTH), BF16),
        scratch_shapes=[pltpu.VMEM((2, MXU_COLS, tq), BF16),
                        pltpu.VMEM((2, n_units, FOX_K_TILE, FOX_COLS), F32),
                        pltpu.VMEM((2, n_units, 1, FOX_COLS), F32),
                        pltpu.VMEM((n_units, 1, FOX_COLS), F32),
                        pltpu.VMEM((n_units, HEAD_DIM + FOX_DEN_ROWS, FOX_COLS), F32)],
        compiler_params=_params("parallel", "parallel", "arbitrary"),
        name="fox_attention",
    )(qbt, cqt, kb, ck_ext, vbt)

    tile = DIL_TILE
    assert T % tile == 0 and all((tile // d) % DIL_BACK == 0 for d in dils)
    lane_head = jnp.arange(DIL_OUT) // HEAD_DIM
    head_mask = jnp.broadcast_to((lane_head[None, :] == jnp.arange(DIL_HEADS)[:, None])[:, None, :],
                                 (DIL_HEADS, DIL_BACK, DIL_OUT)).astype(BF16)
    dil_specs, dil_args = [], []
    for gi, d in enumerate(dils):
        per_tile = tile // d // DIL_BACK
        cur = pl.BlockSpec((None, d, tile // d, DIL_OUT), lambda b, j: (b, 0, j, 0))
        prev = pl.BlockSpec((None, d, DIL_BACK, DIL_OUT),
                            lambda b, j, per_tile=per_tile: (b, 0, jnp.maximum(j * per_tile - 1, 0), 0))
        dil_specs += [cur, cur, prev, cur, prev]
        dil_args += [qa_g[gi], ka_g[gi], ka_g[gi], va_g[gi], va_g[gi]]
    ya = pl.pallas_call(
        _dilated_kernel,
        grid=(B, T // tile),
        in_specs=[_full((DIL_HEADS, DIL_BACK, DIL_OUT))] + dil_specs,
        out_specs=pl.BlockSpec((tile, DIL_OUT), lambda b, j: (b * (T // tile) + j, 0)),
        out_shape=jax.ShapeDtypeStruct((N, DIL_OUT), BF16),
        scratch_shapes=[pltpu.VMEM((N_DIL_GROUPS, DIL_OUT // LANES, tile, LANES), F32)] * 2,
        compiler_params=_params("parallel", "arbitrary"),
        name="dilated_attention",
    )(head_mask, *dil_args)

    gkm = g_kM[0].reshape(1, MEM_HEAD_DIM)
    km, vm = pl.pallas_call(
        _memkv_kernel,
        grid=(B,),
        in_specs=[pl.BlockSpec((None, mem_len, D), lambda b: (b, 0, 0)), _full((1, D)),
                  _full((D, 2 * MEM_WIDTH)), _full((1, MEM_HEAD_DIM))],
        out_specs=[pl.BlockSpec((None, mem_len, MEM_WIDTH), lambda b: (b, 0, 0))] * 2,
        out_shape=[jax.ShapeDtypeStruct((B, mem_len, MEM_WIDTH), BF16)] * 2,
        compiler_params=_params("parallel"),
        name="mem_kv",
    )(mem, g_mem, w_mem_kv[0].astype(BF16), gkm)

    mem_spec = pl.BlockSpec((None, mem_len, MEM_WIDTH), lambda i: (i // (T // tm), 0, 0))
    x_mid = pl.pallas_call(
        _merge_kernel,
        grid=(N // tm,),
        in_specs=[row(D), row(D), row(DIL_OUT), row(FOX_WIDTH), row(MEM_WIDTH), mem_spec, mem_spec,
                  _full((D, 3 * D)), _full((1, 3 * D)), _full((DIL_OUT, D)), _full((FOX_WIDTH, D)),
                  _full((MEM_WIDTH, D)), _full((D, D))],
        out_specs=row(D),
        out_shape=jax.ShapeDtypeStruct((N, D), F32),
        scratch_shapes=[pltpu.VMEM((tm, D), BF16)],
        compiler_params=_params("parallel"),
        name="merge_outproj",
    )(x2d, h, ya, yb, qm, km, vm, w_gate[0].astype(BF16), b_gate, w_br_a[0].astype(BF16),
      w_br_b[0].astype(BF16), w_br_m[0].astype(BF16), w_out[0].astype(BF16))

    out = pl.pallas_call(
        _mlp_kernel,
        grid=(N // tm,),
        in_specs=[row(D), _full((1, D)), _full((D, D_FF)), _full((D_FF, D))],
        out_specs=row(D),
        out_shape=jax.ShapeDtypeStruct((N, D), F32),
        scratch_shapes=[pltpu.VMEM((tm, D), BF16)],
        compiler_params=_params("parallel"),
        name="mlp",
    )(x_mid, g_mlp, w_up[0].astype(BF16), w_down[0].astype(BF16))
    return out.reshape(B, T, D)
```

```python
import functools

import jax
import jax.numpy as jnp
import numpy as np
from jax import lax
from jax.experimental import pallas as pl
from jax.experimental.pallas import tpu as pltpu

D_MODEL = 1024
HEAD_DIM = 64
DIL_GROUPS = ((128, 1), (512, 4), (2048, 16))
N_DIL_GROUPS = 3
DIL_HEADS = 4
DIL_WIDTH = N_DIL_GROUPS * DIL_HEADS * HEAD_DIM
DIL_OUT = DIL_HEADS * HEAD_DIM
DIL_BACK = 128
FOX_HEADS = 8
FOX_WIDTH = FOX_HEADS * HEAD_DIM
MEM_HEADS = 4
MEM_HEAD_DIM = 128
MEM_WIDTH = MEM_HEADS * MEM_HEAD_DIM
ROT_DIM = HEAD_DIM // 4
ROPE_THETA = 500000.0
D_FF = 4 * D_MODEL
EPS = 1e-6

LANES = 128
MXU_COLS = 256
FL_PAD = LANES

ROW_TILE = 1024
INPROJ_ROW_TILE = 512
INPROJ_COL = dict(qa=0, ka=DIL_WIDTH, va=2 * DIL_WIDTH, qb=3 * DIL_WIDTH, kb=3 * DIL_WIDTH + FOX_WIDTH,
                  vb=3 * DIL_WIDTH + 2 * FOX_WIDTH, qm=3 * DIL_WIDTH + 3 * FOX_WIDTH,
                  fl=3 * DIL_WIDTH + 3 * FOX_WIDTH + MEM_WIDTH)
INPROJ_ORDER = (("qa", 0), ("ka", 0), ("qa", 1), ("ka", 1), ("qa", 2), ("ka", 2), ("qb", 0), ("kb", 0),
                ("qb", 1), ("kb", 1), ("qm", 0), ("qm", 1), ("va", 2), ("va", 1), ("vb", 0), ("vb", 1), ("va", 0))
FOX_Q_TILE = 2048
FOX_K_TILE = 256
FOX_COLS = 512
FOX_DEN_ROWS = 16
FOX_LOOKAHEAD = 1
FOX_BLOCKS_PER_TRIP = 4
DECAY_ROWS = 16
CUMSUM_BLOCK = 256
DIL_TILE = 2048
DIL_MERGE_ROWS = 256
DIL_CHAINS = 8
DIL_LATER_ITEMS = 12

LOG2E = 1.4426950408889634
MASKED = -1e30
BF16 = jnp.bfloat16
F32 = jnp.float32
NT_DIMS = (((1,), (1,)), ((), ()))


def _dot(a, b):
    return jnp.dot(a, b, preferred_element_type=F32)


def _dot_nt(a, b):
    return lax.dot_general(a, b, NT_DIMS, preferred_element_type=F32)


def _row_rmsnorm(x, g):
    ms = jnp.mean(x * x, axis=-1, keepdims=True)
    return x * lax.rsqrt(ms + EPS) * g


def _head_rmsnorm(y, blockdiag, gain, head_dim):
    y2 = (y * y).astype(BF16)
    half = y.shape[0] // 2
    ss = jnp.concatenate([_dot(y2[:half], blockdiag), _dot(y2[half:], blockdiag)], axis=0)
    return y * lax.rsqrt(ss * (1.0 / head_dim) + EPS) * gain


def _split3(c):
    hi = c.astype(BF16)
    r1 = c - hi.astype(F32)
    mid = r1.astype(BF16)
    lo = (r1 - mid.astype(F32)).astype(BF16)
    return hi, mid, lo


def _inproj_kernel(x_ref, gmix_ref, w_ref, bd64_ref, gqa_ref, gka_ref, gqb_ref, gkb_ref,
                   gqm_ref, bf_ref, rc_ref, rs1_ref, rs2_ref,
                   h_ref, qa0_ref, qa1_ref, qa2_ref, ka0_ref, ka1_ref, ka2_ref, va0_ref, va1_ref, va2_ref,
                   qbt_ref, kb_ref, vbt_ref, qm_ref, lf_ref, perm_ref):
    h = _row_rmsnorm(x_ref[...], gmix_ref[...]).astype(BF16)
    h_ref[...] = h
    bd64 = bd64_ref[...]
    rc = rc_ref[...]
    rs1 = rs1_ref[...]
    rs2 = rs2_ref[...]
    cw = MXU_COLS
    tm = x_ref.shape[0]

    def proj(col, width=cw):
        return _dot(h, w_ref[:, col:col + width])

    def rope(y):
        return y * rc + pltpu.roll(y, cw - ROT_DIM // 2, 1) * rs1 + pltpu.roll(y, ROT_DIM // 2, 1) * rs2

    def store_rows(out_ref, c, y):
        out_ref[:, c:c + cw] = y.astype(out_ref.dtype)

    def store_transposed(out_ref, c, y):
        out_ref[c:c + cw, :] = y.T.astype(out_ref.dtype)

    def store_by_residue(out_ref, y):
        d = out_ref.shape[0]
        if d == 1:
            out_ref[0] = y.astype(out_ref.dtype)
            return
        for half in range(cw // LANES):
            perm_ref[half] = y[:, half * LANES:(half + 1) * LANES]
        for r in range(d):
            out_ref[r] = jnp.concatenate(
                [perm_ref[half, pl.ds(r, tm // d, stride=d), :] for half in range(cw // LANES)],
                axis=1).astype(out_ref.dtype)

    rows_of = lambda ref: [(functools.partial(store_rows, ref, c)) for c in range(0, ref.shape[1], cw)]
    cols_of = lambda ref: [(functools.partial(store_transposed, ref, c)) for c in range(0, ref.shape[0], cw)]
    residues_of = lambda refs: [functools.partial(store_by_residue, ref) for ref in refs]
    segments = {
        "qa": (residues_of((qa0_ref, qa1_ref, qa2_ref)), gqa_ref, bd64, HEAD_DIM, True),
        "ka": (residues_of((ka0_ref, ka1_ref, ka2_ref)), gka_ref, bd64, HEAD_DIM, True),
        "va": (residues_of((va0_ref, va1_ref, va2_ref)), None, None, None, False),
        "qb": (cols_of(qbt_ref), gqb_ref, bd64, HEAD_DIM, False),
        "kb": (rows_of(kb_ref), gkb_ref, bd64, HEAD_DIM, False),
        "vb": (cols_of(vbt_ref), None, None, None, False),
        "qm": (rows_of(qm_ref), gqm_ref, None, MEM_HEAD_DIM, False)}
    chunks = [(INPROJ_COL[name] + piece * cw, segments[name][0][piece]) + segments[name][1:]
              for name, piece in INPROJ_ORDER]
    z = proj(INPROJ_COL["fl"], FL_PAD) + bf_ref[...]
    y_next = proj(chunks[0][0])
    lf_ref[...] = jnp.minimum(z, 0.0) - jnp.log1p(jnp.exp(-jnp.abs(z)))
    for idx, (_, store, gain_ref, bd, hd, rot) in enumerate(chunks):
        y = y_next
        y_next = proj(chunks[idx + 1][0]) if idx + 1 < len(chunks) else None
        if gain_ref is not None and hd == LANES:
            gain = gain_ref[...]
            y = jnp.concatenate([_row_rmsnorm(y[:, c:c + hd], gain[:, c:c + hd]) for c in range(0, cw, hd)], axis=1)
        elif gain_ref is not None:
            y = _head_rmsnorm(y, bd, gain_ref[...], hd)
        if rot:
            y = rope(y)
        store(y)


def _cumsum_kernel(lf_ref, tri_ref, pk_ref, onek_ref, pq_ref, oneq_ref, cqt_ref, ck_ref):
    tri = tri_ref[...]
    n_blocks = lf_ref.shape[0] // CUMSUM_BLOCK
    n_hp = ck_ref.shape[0]
    carry = jnp.zeros((1, LANES), F32)
    for blk in range(n_blocks):
        rows = slice(blk * CUMSUM_BLOCK, (blk + 1) * CUMSUM_BLOCK)
        local = _dot(tri, jnp.concatenate(_split3(lf_ref[rows, :]), axis=1))
        c = local[:, :LANES] + local[:, LANES:2 * LANES] + local[:, 2 * LANES:] + carry
        carry = c[CUMSUM_BLOCK - 1:CUMSUM_BLOCK, :]
        c = c * LOG2E
        pieces = jnp.concatenate(_split3(c), axis=1)
        ke = _dot(pieces, pk_ref[...]) + onek_ref[...]
        qe = _dot_nt(pq_ref[...], pieces) + oneq_ref[...]
        for hp in range(n_hp):
            ck_ref[hp, rows, :] = ke[:, hp * LANES:(hp + 1) * LANES].astype(BF16)
            for e in range(2):
                r0 = (2 * hp + e) * DECAY_ROWS
                cqt_ref[hp, e, :, rows] = qe[r0:r0 + DECAY_ROWS, :].astype(BF16)


def _fox_kernel(qt_ref, cqt_ref, k_ref, ck_ref, vt_ref, o_ref, qa_ref, st_ref, bm_ref, m_ref, acc_ref):
    blk = pl.program_id(2)
    tq = FOX_Q_TILE
    tk = FOX_K_TILE
    hd = HEAD_DIM
    cw = FOX_COLS
    zq = jnp.zeros((hd, tq), BF16)
    zd = jnp.zeros((DECAY_ROWS, tq), BF16)
    zpad = jnp.zeros((MXU_COLS - 2 * hd - 2 * DECAY_ROWS, tq), BF16)
    for e in range(2):
        parts = [zq, zq, zd, zd, zpad]
        parts[e] = qt_ref[e * hd:(e + 1) * hd, :]
        parts[2 + e] = cqt_ref[e]
        qa_ref[e] = jnp.concatenate(parts, axis=0)
    units = [(e, c0) for e in range(2) for c0 in range(0, tq, cw)]

    def key_block(key_start):
        rows = pl.ds(pl.multiple_of(key_start, tk), tk)
        return jnp.concatenate([k_ref[rows, :], ck_ref[rows, :]], axis=1)

    def value_block(key_start):
        return vt_ref[:, pl.ds(pl.multiple_of(key_start, tk), tk)]

    def first_visible(u, key_off):
        c0 = units[u][1]
        return cw // 2 if key_off is not None and key_off >= c0 + cw // 2 else 0

    def issue_scores(buf, u, k_aug, key_off=None):
        e, c0 = units[u]
        lo = first_visible(u, key_off)
        st = _dot(k_aug, qa_ref[e, :, c0 + lo:c0 + cw])
        st_ref[buf, u, :, lo:] = st
        bm_ref[buf, u, :, lo:] = jnp.max(st, axis=0, keepdims=True)

    def update(buf, u, vt, key_off):
        e, c0 = units[u]
        lo = first_visible(u, key_off)
        st = st_ref[buf, u, :, lo:]
        if key_off is not None and key_off + tk - 1 > c0 + lo:
            key = lax.broadcasted_iota(jnp.int32, st.shape, 0) + key_off
            qry = lax.broadcasted_iota(jnp.int32, st.shape, 1) + (c0 + lo)
            st = jnp.where(key <= qry, st, MASKED)
            block_max = jnp.max(st, axis=0, keepdims=True)
        else:
            block_max = bm_ref[buf, u, :, lo:]
        m = m_ref[u, :, lo:]
        m_new = jnp.maximum(m, block_max)
        alpha = jnp.exp2(m - m_new)
        p = jnp.exp2(st - m_new).astype(BF16)
        vt_aug = jnp.concatenate([vt[e * hd:(e + 1) * hd, :], ones_rows], axis=0)
        acc_ref[u, :, lo:] = alpha * acc_ref[u, :, lo:] + _dot(vt_aug, p)
        m_ref[u, :, lo:] = m_new

    ones_rows = jnp.ones((FOX_DEN_ROWS, tk), BF16)
    m_ref[...] = jnp.full(m_ref.shape, MASKED, F32)
    acc_ref[...] = jnp.zeros(acc_ref.shape, F32)
    k_first = key_block(0)
    for u in range(len(units)):
        issue_scores(0, u, k_first)

    per_trip = FOX_BLOCKS_PER_TRIP
    assert per_trip % 2 == 0 and (tq // tk) % per_trip == 0

    def block_group(jg, carry):
        for step in range(per_trip):
            j = per_trip * jg + step
            k_next = key_block((j + 1) * tk)
            vt = value_block(j * tk)
            for u in range(min(FOX_LOOKAHEAD, len(units))):
                issue_scores(1 - step % 2, u, k_next)
            for u in range(len(units)):
                if u + FOX_LOOKAHEAD < len(units):
                    issue_scores(1 - step % 2, u + FOX_LOOKAHEAD, k_next)
                update(step % 2, u, vt, None)
        return carry

    lax.fori_loop(0, blk * (tq // tk // per_trip), block_group, 0)
    for jj in range(tq // tk):
        key_off = jj * tk
        active = [u for u, (e, c0) in enumerate(units) if c0 + cw > key_off]
        nxt_off = key_off + tk
        nxt_active = [u for u, (e, c0) in enumerate(units) if c0 + cw > nxt_off] if nxt_off < tq else []
        if nxt_active:
            k_next = key_block(blk * tq + nxt_off)
        vt = value_block(blk * tq + key_off)
        for u in active:
            if u in nxt_active:
                issue_scores((jj + 1) % 2, u, k_next, nxt_off)
            update(jj % 2, u, vt, key_off)
    per_head = []
    for e in range(2):
        cols = [acc_ref[u, :hd, :] * (1.0 / acc_ref[u, hd:hd + 1, :]) for u, unit in enumerate(units) if unit[0] == e]
        per_head.append(jnp.concatenate(cols, axis=1))
    ot = jnp.concatenate(per_head, axis=0)
    o_ref[...] = ot.T.astype(o_ref.dtype)


def _dilated_kernel(hm_ref, q0_ref, k0_ref, kp0_ref, v0_ref, vp0_ref, q1_ref, k1_ref, kp1_ref, v1_ref, vp1_ref,
                    q2_ref, k2_ref, kp2_ref, v2_ref, vp2_ref, ya_ref, o_scr, l_scr):
    jt = pl.program_id(1)
    c = DIL_BACK
    w = DIL_OUT
    nh = DIL_HEADS
    tile = ya_ref.shape[0]
    qi = lax.broadcasted_iota(jnp.int32, (nh * c, 2 * c), 0) & (c - 1)
    kj = lax.broadcasted_iota(jnp.int32, (nh * c, 2 * c), 1)
    dist = qi + c - kj
    in_band = (dist >= 0) & (dist <= DIL_BACK)
    in_band_first = in_band & (kj >= jnp.where(jt > 0, 0, c))
    lane = lax.broadcasted_iota(jnp.int32, (c, w), 1)
    in_head = [(lane >= hh * HEAD_DIM) & (lane < (hh + 1) * HEAD_DIM) for hh in range(nh)]

    def attend(gi, d, items, mask):
        def scores(item):
            q, kk, _, _ = item
            return _dot_nt(jnp.concatenate([q * hm_ref[hh] for hh in range(nh)], axis=0), kk)

        s_next = scores(items[0])
        for idx, (_, _, vv, tok0) in enumerate(items):
            s = s_next
            s_next = scores(items[idx + 1]) if idx + 1 < len(items) else None
            s = jnp.where(mask, s, MASKED)
            m = jnp.max(s, axis=-1, keepdims=True)
            p = jnp.exp2(s - m)
            den = jnp.sum(p, axis=-1, keepdims=True)
            o4 = _dot(p.astype(BF16), vv) * (1.0 / den)
            lse4 = m + jnp.log2(den)
            o = jnp.zeros((c, w), F32)
            lse = jnp.zeros((c, w), F32)
            for hh in range(nh):
                rows = slice(hh * c, (hh + 1) * c)
                o = jnp.where(in_head[hh], o4[rows], o)
                lse = jnp.where(in_head[hh], lse4[rows], lse)
            for half in range(w // LANES):
                lanes = slice(half * LANES, (half + 1) * LANES)
                o_scr[gi, half, pl.ds(tok0, c, stride=d), :] = o[:, lanes]
                l_scr[gi, half, pl.ds(tok0, c, stride=d), :] = lse[:, lanes]

    groups = ((q0_ref, k0_ref, kp0_ref, v0_ref, vp0_ref), (q1_ref, k1_ref, kp1_ref, v1_ref, vp1_ref),
              (q2_ref, k2_ref, kp2_ref, v2_ref, vp2_ref))
    for gi, (q_ref, k_ref, kp_ref, v_ref, vp_ref) in enumerate(groups):
        d = q_ref.shape[0]
        n_sb = q_ref.shape[1] // c
        res_chunk = min(d, DIL_CHAINS)
        assert d % res_chunk == 0

        def first_item(r, q_ref=q_ref, k_ref=k_ref, kp_ref=kp_ref, v_ref=v_ref, vp_ref=vp_ref):
            kk = jnp.concatenate([kp_ref[r], k_ref[r, 0:c]], axis=0)
            vv = jnp.concatenate([vp_ref[r], v_ref[r, 0:c]], axis=0)
            return q_ref[r, 0:c], kk, vv, r

        def later_item(r, sb, d=d, q_ref=q_ref, k_ref=k_ref, v_ref=v_ref):
            start = pl.multiple_of(sb * c, c)
            window = pl.ds(start - c, 2 * c)
            return q_ref[r, pl.ds(start, c)], k_ref[r, window], v_ref[r, window], sb * (c * d) + r

        def first_chunk(it, carry, gi=gi, d=d, res_chunk=res_chunk, first_item=first_item):
            attend(gi, d, [first_item(it * res_chunk + i) for i in range(res_chunk)], in_band_first)
            return carry

        if d == res_chunk:
            first_chunk(0, 0)
        else:
            lax.fori_loop(0, d // res_chunk, first_chunk, 0)
        if n_sb > 1:
            sb_chunk = max(1, DIL_LATER_ITEMS // res_chunk)
            sb_chunk = max(k for k in range(1, sb_chunk + 1) if (n_sb - 1) % k == 0)
            assert d == res_chunk

            def later_chunk(it, carry, gi=gi, d=d, sb_chunk=sb_chunk, later_item=later_item):
                attend(gi, d, [later_item(r, 1 + it * sb_chunk + i) for i in range(sb_chunk) for r in range(d)],
                       in_band)
                return carry

            lax.fori_loop(0, (n_sb - 1) // sb_chunk, later_chunk, 0)

    def merge(ch, carry):
        rows = pl.ds(pl.multiple_of(ch * DIL_MERGE_ROWS, DIL_MERGE_ROWS), DIL_MERGE_ROWS)
        for half in range(w // LANES):
            l = [l_scr[g, half, rows, :] for g in range(N_DIL_GROUPS)]
            mx = jnp.maximum(jnp.maximum(l[0], l[1]), l[2])
            e = [jnp.exp2(lg - mx) for lg in l]
            num = e[0] * o_scr[0, half, rows, :] + e[1] * o_scr[1, half, rows, :] + e[2] * o_scr[2, half, rows, :]
            ya_ref[rows, half * LANES:(half + 1) * LANES] = (num / (e[0] + e[1] + e[2])).astype(ya_ref.dtype)
        return carry

    lax.fori_loop(0, tile // DIL_MERGE_ROWS, merge, 0)


def _memkv_kernel(mem_ref, g_ref, w_ref, gk_ref, km_ref, vm_ref):
    mn = _row_rmsnorm(mem_ref[...], g_ref[...]).astype(BF16)
    kv = _dot(mn, w_ref[...])
    gk = gk_ref[...]
    for hh in range(MEM_HEADS):
        cols = slice(hh * MEM_HEAD_DIM, (hh + 1) * MEM_HEAD_DIM)
        km_ref[:, cols] = _row_rmsnorm(kv[:, cols], gk).astype(km_ref.dtype)
    vm_ref[...] = kv[:, MEM_WIDTH:].astype(vm_ref.dtype)


def _merge_kernel(x_ref, h_ref, ya_ref, yb_ref, qm_ref, km_ref, vm_ref,
                  wg_ref, bg_ref, wa_ref, wb_ref, wm_ref, wo_ref, out_ref, merged_ref):
    ya = ya_ref[...]
    ym = []
    for hh in range(MEM_HEADS):
        cols = slice(hh * MEM_HEAD_DIM, (hh + 1) * MEM_HEAD_DIM)
        s = _dot_nt(qm_ref[:, cols], km_ref[:, cols])
        p = jnp.exp(s - jnp.max(s, axis=-1, keepdims=True))
        den = jnp.sum(p, axis=-1, keepdims=True)
        ym.append((_dot(p.astype(BF16), vm_ref[:, cols]) / den).astype(BF16))
    ym = jnp.concatenate(ym, axis=1)
    yb = yb_ref[...]
    h = h_ref[...]
    cw = 2 * MXU_COLS
    for c in range(0, D_MODEL, cw):
        merged = None
        for k, (y, w_ref) in enumerate(((ya, wa_ref), (yb, wb_ref), (ym, wm_ref))):
            gcol = k * D_MODEL + c
            gate = jax.nn.sigmoid(_dot(h, wg_ref[:, gcol:gcol + cw]) + bg_ref[:, gcol:gcol + cw])
            term = gate * _dot(y, w_ref[:, c:c + cw])
            merged = term if merged is None else merged + term
        merged_ref[:, c:c + cw] = merged.astype(BF16)
    out_ref[...] = x_ref[...] + _dot(merged_ref[...], wo_ref[...])


def _mlp_kernel(x_ref, g_ref, wu_ref, wd_ref, out_ref, h2_ref):
    x = x_ref[...]
    h2_ref[...] = _row_rmsnorm(x, g_ref[...]).astype(BF16)
    cw = D_MODEL
    acc = x
    for c in range(0, D_FF, cw):
        u = jnp.maximum(_dot(h2_ref[...], wu_ref[:, c:c + cw]), 0.0)
        acc = acc + _dot((u * u).astype(BF16), wd_ref[c:c + cw, :])
    out_ref[...] = acc


def _full(shape):
    return pl.BlockSpec(shape, lambda *_: (0,) * len(shape), pipeline_mode=pl.Buffered(1))


def _params(*sem):
    return pltpu.CompilerParams(dimension_semantics=sem)


def _blockdiag(width, head_dim):
    r = jnp.arange(width) // head_dim
    return (r[:, None] == r[None, :]).astype(BF16)


def _decay_placement(n_hp):
    pk = np.zeros((3 * LANES, n_hp * LANES), np.float32)
    onek = np.zeros((1, n_hp * LANES), np.float32)
    pq = np.zeros((n_hp * 2 * DECAY_ROWS, 3 * LANES), np.float32)
    oneq = np.zeros((n_hp * 2 * DECAY_ROWS, CUMSUM_BLOCK), np.float32)
    for hp in range(n_hp):
        for e in range(2):
            head = 2 * hp + e
            kcol = hp * LANES + DECAY_ROWS * e
            qrow = (2 * hp + e) * DECAY_ROWS
            for t in range(3):
                onek[0, kcol + t] = 1.0
                pk[t * LANES + head, kcol + 3 + t] = -1.0
                pq[qrow + t, t * LANES + head] = 1.0
                oneq[qrow + 3 + t, :] = 1.0
    return jnp.asarray(pk, BF16), jnp.asarray(onek), jnp.asarray(pq, BF16), jnp.asarray(oneq)


def _rope_tables(seq, width):
    half = ROT_DIM // 2
    inv_freq = ROPE_THETA ** (-jnp.arange(0, ROT_DIM, 2, dtype=F32) / ROT_DIM)
    ang = jnp.arange(seq, dtype=F32)[:, None] * inv_freq[None, :]
    cos, sin = jnp.cos(ang), jnp.sin(ang)
    ones = jnp.ones((seq, HEAD_DIM - ROT_DIM), F32)
    zeros = jnp.zeros((seq, HEAD_DIM - ROT_DIM), F32)
    zhalf = jnp.zeros((seq, half), F32)
    rc = jnp.concatenate([cos, cos, ones], axis=1)
    rs1 = jnp.concatenate([-sin, zhalf, zeros], axis=1)
    rs2 = jnp.concatenate([zhalf, sin, zeros], axis=1)
    reps = width // HEAD_DIM
    return tuple(jnp.tile(t, (1, reps)) for t in (rc, rs1, rs2))


def kernel(x, mem, g_mix, w_in, b_f, g_qA, g_kA, g_qB, g_kB, g_mem, w_mem_kv, g_qM, g_kM, w_gate, b_gate,
           w_br_a, w_br_b, w_br_m, w_out, g_mlp, w_up, w_down):
    B, T, D = x.shape
    assert D == D_MODEL and w_in.shape[0] == 1, "single-layer kernel"
    N = B * T
    mem_len = mem.shape[1]
    tm = ROW_TILE
    assert T % tm == 0 and T % FOX_Q_TILE == 0 and FOX_Q_TILE % FOX_K_TILE == 0
    x2d = x.reshape(N, D)

    wi = w_in[0]
    offs = [0]
    for wdt in (DIL_WIDTH, DIL_WIDTH, DIL_WIDTH, FOX_WIDTH, FOX_WIDTH, FOX_WIDTH, FOX_HEADS, MEM_WIDTH):
        offs.append(offs[-1] + wdt)
    w_fl = jnp.pad(wi[:, offs[6]:offs[7]], ((0, 0), (0, FL_PAD - FOX_HEADS)))
    w_all = jnp.concatenate([wi[:, :offs[6]], wi[:, offs[7]:], w_fl], axis=1).astype(BF16)
    assert offs[6] == INPROJ_COL["qm"] and w_all.shape[1] == INPROJ_COL["fl"] + FL_PAD
    n_cols = w_all.shape[1]
    bf_pad = jnp.pad(b_f[0], (0, FL_PAD - FOX_HEADS)).reshape(1, FL_PAD)
    cw = MXU_COLS
    q_scale = HEAD_DIM ** -0.5
    gqa = (jnp.tile(g_qA[0], cw // HEAD_DIM) * (q_scale * LOG2E)).reshape(1, cw)
    gka = jnp.tile(g_kA[0], cw // HEAD_DIM).reshape(1, cw)
    gqb = (jnp.tile(g_qB[0], cw // HEAD_DIM) * (q_scale * LOG2E)).reshape(1, cw)
    gkb = jnp.tile(g_kB[0], cw // HEAD_DIM).reshape(1, cw)
    gqm = (jnp.tile(g_qM[0], cw // MEM_HEAD_DIM) * MEM_HEAD_DIM ** -0.5).reshape(1, cw)
    rc, rs1, rs2 = _rope_tables(T, cw)
    bd64 = _blockdiag(cw, HEAD_DIM)

    row = lambda width: pl.BlockSpec((tm, width), lambda i: (i, 0))
    tm1 = INPROJ_ROW_TILE
    per_seq = T // tm1
    assert T % tm1 == 0
    row1 = lambda width: pl.BlockSpec((tm1, width), lambda i: (i, 0))
    rope_spec = pl.BlockSpec((tm1, cw), lambda i: (i % per_seq, 0))
    tspec = pl.BlockSpec((None, FOX_WIDTH, tm1), lambda i: (i // per_seq, 0, i % per_seq))
    tshape = jax.ShapeDtypeStruct((B, FOX_WIDTH, T), BF16)
    rshape = lambda width: jax.ShapeDtypeStruct((N, width), BF16)
    dils = [d for _, d in DIL_GROUPS]
    assert all(win // d == DIL_BACK and tm1 % d == 0 and (tm1 // d) % 16 == 0 for win, d in DIL_GROUPS)
    dspecs = [pl.BlockSpec((None, d, tm1 // d, DIL_OUT), lambda i: (i // per_seq, 0, i % per_seq, 0))
              for d in dils]
    dshapes = [jax.ShapeDtypeStruct((B, d, T // d, DIL_OUT), BF16) for d in dils]
    outs = pl.pallas_call(
        _inproj_kernel,
        grid=(N // tm1,),
        in_specs=[row1(D), _full((1, D)), _full((D, n_cols)), _full((cw, cw)),
                  _full((1, cw)), _full((1, cw)), _full((1, cw)), _full((1, cw)), _full((1, cw)),
                  _full((1, FL_PAD)), rope_spec, rope_spec, rope_spec],
        out_specs=[row1(D)] + dspecs * 3 + [tspec, row1(FOX_WIDTH), tspec, row1(MEM_WIDTH), row1(FL_PAD)],
        out_shape=[rshape(D)] + dshapes * 3 + [tshape, rshape(FOX_WIDTH), tshape, rshape(MEM_WIDTH),
                                               jax.ShapeDtypeStruct((N, FL_PAD), F32)],
        scratch_shapes=[pltpu.VMEM((cw // LANES, tm1, LANES), F32)],
        compiler_params=_params("parallel"),
        name="inproj",
    )(x2d, g_mix, w_all, bd64, gqa, gka, gqb, gkb, gqm, bf_pad, rc, rs1, rs2)
    h, qa_g, ka_g, va_g = outs[0], outs[1:4], outs[4:7], outs[7:10]
    qbt, kb, vbt, qm, logf = outs[10:]

    n_hp = FOX_HEADS // 2
    tri = (jnp.arange(CUMSUM_BLOCK)[:, None] >= jnp.arange(CUMSUM_BLOCK)[None, :]).astype(BF16)
    pk, onek, pq, oneq = _decay_placement(n_hp)
    cqt, ck_ext = pl.pallas_call(
        _cumsum_kernel,
        grid=(B,),
        in_specs=[pl.BlockSpec((T, FL_PAD), lambda b: (b, 0)), _full((CUMSUM_BLOCK, CUMSUM_BLOCK)),
                  _full(pk.shape), _full(onek.shape), _full(pq.shape), _full(oneq.shape)],
        out_specs=[pl.BlockSpec((None, n_hp, 2, DECAY_ROWS, T), lambda b: (b, 0, 0, 0, 0)),
                   pl.BlockSpec((None, n_hp, T, LANES), lambda b: (b, 0, 0, 0))],
        out_shape=[jax.ShapeDtypeStruct((B, n_hp, 2, DECAY_ROWS, T), BF16),
                   jax.ShapeDtypeStruct((B, n_hp, T, LANES), BF16)],
        compiler_params=_params("parallel"),
        name="decay_cumsum",
    )(logf, tri, pk, onek, pq, oneq)

    tq = FOX_Q_TILE
    nq = T // tq
    n_units = 2 * tq // FOX_COLS
    yb = pl.pallas_call(
        _fox_kernel,
        grid=(B, n_hp, nq),
        in_specs=[pl.BlockSpec((None, LANES, tq), lambda b, hp, i: (b, hp, i)),
                  pl.BlockSpec((None, None, 2, DECAY_ROWS, tq), lambda b, hp, i: (b, hp, 0, 0, i)),
                  pl.BlockSpec((T, LANES), lambda b, hp, i: (b, hp)),
                  pl.BlockSpec((None, None, T, LANES), lambda b, hp, i: (b, hp, 0, 0)),
                  pl.BlockSpec((None, LANES, T), lambda b, hp, i: (b, hp, 0))],
        out_specs=pl.BlockSpec((tq, LANES), lambda b, hp, i: (b * nq + i, hp)),
        out_shape=jax.ShapeDtypeStruct((N, FOX_WIDTH), BF16),
        scratch_shapes=[pltpu.VMEM((2, MXU_COLS, tq), BF16),
                        pltpu.VMEM((2, n_units, FOX_K_TILE, FOX_COLS), F32),
                        pltpu.VMEM((2, n_units, 1, FOX_COLS), F32),
                        pltpu.VMEM((n_units, 1, FOX_COLS), F32),
                        pltpu.VMEM((n_units, HEAD_DIM + FOX_DEN_ROWS, FOX_COLS), F32)],
        compiler_params=_params("parallel", "parallel", "arbitrary"),
        name="fox_attention",
    )(qbt, cqt, kb, ck_ext, vbt)

    tile = DIL_TILE
    assert T % tile == 0 and all((tile // d) % DIL_BACK == 0 for d in dils)
    lane_head = jnp.arange(DIL_OUT) // HEAD_DIM
    head_mask = jnp.broadcast_to((lane_head[None, :] == jnp.arange(DIL_HEADS)[:, None])[:, None, :],
                                 (DIL_HEADS, DIL_BACK, DIL_OUT)).astype(BF16)
    dil_specs, dil_args = [], []
    for gi, d in enumerate(dils):
        per_tile = tile // d // DIL_BACK
        cur = pl.BlockSpec((None, d, tile // d, DIL_OUT), lambda b, j: (b, 0, j, 0))
        prev = pl.BlockSpec((None, d, DIL_BACK, DIL_OUT),
                            lambda b, j, per_tile=per_tile: (b, 0, jnp.maximum(j * per_tile - 1, 0), 0))
        dil_specs += [cur, cur, prev, cur, prev]
        dil_args += [qa_g[gi], ka_g[gi], ka_g[gi], va_g[gi], va_g[gi]]
    ya = pl.pallas_call(
        _dilated_kernel,
        grid=(B, T // tile),
        in_specs=[_full((DIL_HEADS, DIL_BACK, DIL_OUT))] + dil_specs,
        out_specs=pl.BlockSpec((tile, DIL_OUT), lambda b, j: (b * (T // tile) + j, 0)),
        out_shape=jax.ShapeDtypeStruct((N, DIL_OUT), BF16),
        scratch_shapes=[pltpu.VMEM((N_DIL_GROUPS, DIL_OUT // LANES, tile, LANES), F32)] * 2,
        compiler_params=_params("parallel", "arbitrary"),
        name="dilated_attention",
    )(head_mask, *dil_args)

    gkm = g_kM[0].reshape(1, MEM_HEAD_DIM)
    km, vm = pl.pallas_call(
        _memkv_kernel,
        grid=(B,),
        in_specs=[pl.BlockSpec((None, mem_len, D), lambda b: (b, 0, 0)), _full((1, D)),
                  _full((D, 2 * MEM_WIDTH)), _full((1, MEM_HEAD_DIM))],
        out_specs=[pl.BlockSpec((None, mem_len, MEM_WIDTH), lambda b: (b, 0, 0))] * 2,
        out_shape=[jax.ShapeDtypeStruct((B, mem_len, MEM_WIDTH), BF16)] * 2,
        compiler_params=_params("parallel"),
        name="mem_kv",
    )(mem, g_mem, w_mem_kv[0].astype(BF16), gkm)

    mem_spec = pl.BlockSpec((None, mem_len, MEM_WIDTH), lambda i: (i // (T // tm), 0, 0))
    x_mid = pl.pallas_call(
        _merge_kernel,
        grid=(N // tm,),
        in_specs=[row(D), row(D), row(DIL_OUT), row(FOX_WIDTH), row(MEM_WIDTH), mem_spec, mem_spec,
                  _full((D, 3 * D)), _full((1, 3 * D)), _full((DIL_OUT, D)), _full((FOX_WIDTH, D)),
                  _full((MEM_WIDTH, D)), _full((D, D))],
        out_specs=row(D),
        out_shape=jax.ShapeDtypeStruct((N, D), F32),
        scratch_shapes=[pltpu.VMEM((tm, D), BF16)],
        compiler_params=_params("parallel"),
        name="merge_outproj",
    )(x2d, h, ya, yb, qm, km, vm, w_gate[0].astype(BF16), b_gate, w_br_a[0].astype(BF16),
      w_br_b[0].astype(BF16), w_br_m[0].astype(BF16), w_out[0].astype(BF16))

    out = pl.pallas_call(
        _mlp_kernel,
        grid=(N // tm,),
        in_specs=[row(D), _full((1, D)), _full((D, D_FF)), _full((D_FF, D))],
        out_specs=row(D),
        out_shape=jax.ShapeDtypeStruct((N, D), F32),
        scratch_shapes=[pltpu.VMEM((tm, D), BF16)],
        compiler_params=_params("parallel"),
        name="mlp",
    )(x_mid, g_mlp, w_up[0].astype(BF16), w_down[0].astype(BF16))
    return out.reshape(B, T, D)
```

```python
import functools

import jax
import jax.numpy as jnp
import numpy as np
from jax import lax
from jax.experimental import pallas as pl
from jax.experimental.pallas import tpu as pltpu

D_MODEL = 1024
HEAD_DIM = 64
DIL_GROUPS = ((128, 1), (512, 4), (2048, 16))
N_DIL_GROUPS = 3
DIL_HEADS = 4
DIL_WIDTH = N_DIL_GROUPS * DIL_HEADS * HEAD_DIM
DIL_OUT = DIL_HEADS * HEAD_DIM
DIL_BACK = 128
FOX_HEADS = 8
FOX_WIDTH = FOX_HEADS * HEAD_DIM
MEM_HEADS = 4
MEM_HEAD_DIM = 128
MEM_WIDTH = MEM_HEADS * MEM_HEAD_DIM
ROT_DIM = HEAD_DIM // 4
ROPE_THETA = 500000.0
D_FF = 4 * D_MODEL
EPS = 1e-6

LANES = 128
MXU_COLS = 256
FL_PAD = LANES

ROW_TILE = 1024
INPROJ_ROW_TILE = 512
INPROJ_COL = dict(qa=0, ka=DIL_WIDTH, va=2 * DIL_WIDTH, qb=3 * DIL_WIDTH, kb=3 * DIL_WIDTH + FOX_WIDTH,
                  vb=3 * DIL_WIDTH + 2 * FOX_WIDTH, qm=3 * DIL_WIDTH + 3 * FOX_WIDTH,
                  fl=3 * DIL_WIDTH + 3 * FOX_WIDTH + MEM_WIDTH)
INPROJ_ORDER = (("qa", 0), ("ka", 0), ("qa", 1), ("ka", 1), ("qa", 2), ("ka", 2), ("qb", 0), ("kb", 0),
                ("qb", 1), ("kb", 1), ("qm", 0), ("qm", 1), ("va", 2), ("va", 1), ("vb", 0), ("vb", 1), ("va", 0))
FOX_Q_TILE = 2048
FOX_K_TILE = 256
FOX_COLS = 512
FOX_DEN_ROWS = 16
FOX_LOOKAHEAD = 1
FOX_BLOCKS_PER_TRIP = 4
DECAY_ROWS = 16
CUMSUM_BLOCK = 256
DIL_TILE = 2048
DIL_MERGE_ROWS = 256
DIL_CHAINS = 8
DIL_LATER_ITEMS = 12

LOG2E = 1.4426950408889634
MASKED = -jnp.inf
BF16 = jnp.bfloat16
F32 = jnp.float32
NT_DIMS = (((1,), (1,)), ((), ()))


def _dot(a, b):
    return jnp.dot(a, b, preferred_element_type=F32)


def _dot_nt(a, b):
    return lax.dot_general(a, b, NT_DIMS, preferred_element_type=F32)


def _row_rmsnorm(x, g):
    ms = jnp.mean(x * x, axis=-1, keepdims=True)
    return x * lax.rsqrt(ms + EPS) * g


def _head_rmsnorm(y, blockdiag, gain, head_dim):
    y2 = (y * y).astype(BF16)
    half = y.shape[0] // 2
    ss = jnp.concatenate([_dot(y2[:half], blockdiag), _dot(y2[half:], blockdiag)], axis=0)
    return y * lax.rsqrt(ss * (1.0 / head_dim) + EPS) * gain


def _split3(c):
    hi = c.astype(BF16)
    r1 = c - hi.astype(F32)
    mid = r1.astype(BF16)
    lo = (r1 - mid.astype(F32)).astype(BF16)
    return hi, mid, lo


def _inproj_kernel(x_ref, gmix_ref, w_ref, bd64_ref, gqa_ref, gka_ref, gqbt_ref, gkb_ref,
                   gqm_ref, bf_ref, rc_ref, rs1_ref, rs2_ref,
                   h_ref, qa0_ref, qa1_ref, qa2_ref, ka0_ref, ka1_ref, ka2_ref, va0_ref, va1_ref, va2_ref,
                   qbt_ref, kb_ref, vbt_ref, qm_ref, lf_ref, perm_ref):
    h = _row_rmsnorm(x_ref[...], gmix_ref[...]).astype(BF16)
    h_ref[...] = h
    bd64 = bd64_ref[...]
    rc = rc_ref[...]
    rs1 = rs1_ref[...]
    rs2 = rs2_ref[...]
    cw = MXU_COLS
    tm = x_ref.shape[0]

    def proj(col, width=cw):
        return _dot(h, w_ref[:, col:col + width])

    def rope(y):
        return y * rc + pltpu.roll(y, cw - ROT_DIM // 2, 1) * rs1 + pltpu.roll(y, ROT_DIM // 2, 1) * rs2

    def store_rows(out_ref, c, y):
        out_ref[:, c:c + cw] = y.astype(out_ref.dtype)

    def store_transposed(out_ref, c, y):
        out_ref[c:c + cw, :] = y.T.astype(out_ref.dtype)

    def store_transposed_normed(out_ref, c, y):
        yt = y.T
        gain_col = gqbt_ref[...]
        parts = []
        for r0 in range(0, cw, HEAD_DIM):
            blk = yt[r0:r0 + HEAD_DIM]
            ss = jnp.sum(blk * blk, axis=0, keepdims=True)
            gain = jnp.concatenate([gain_col[r0:r0 + HEAD_DIM]] * (tm // LANES), axis=1)
            parts.append(blk * lax.rsqrt(ss * (1.0 / HEAD_DIM) + EPS) * gain)
        out_ref[c:c + cw, :] = jnp.concatenate(parts, axis=0).astype(out_ref.dtype)

    def store_by_residue(out_ref, y):
        d = out_ref.shape[0]
        if d == 1:
            out_ref[0] = y.astype(out_ref.dtype)
            return
        for half in range(cw // LANES):
            perm_ref[half] = y[:, half * LANES:(half + 1) * LANES]
        for r in range(d):
            out_ref[r] = jnp.concatenate(
                [perm_ref[half, pl.ds(r, tm // d, stride=d), :] for half in range(cw // LANES)],
                axis=1).astype(out_ref.dtype)

    rows_of = lambda ref: [(functools.partial(store_rows, ref, c)) for c in range(0, ref.shape[1], cw)]
    cols_of = lambda ref: [(functools.partial(store_transposed, ref, c)) for c in range(0, ref.shape[0], cw)]
    residues_of = lambda refs: [functools.partial(store_by_residue, ref) for ref in refs]
    segments = {
        "qa": (residues_of((qa0_ref, qa1_ref, qa2_ref)), gqa_ref, bd64, HEAD_DIM, True),
        "ka": (residues_of((ka0_ref, ka1_ref, ka2_ref)), gka_ref, bd64, HEAD_DIM, True),
        "va": (residues_of((va0_ref, va1_ref, va2_ref)), None, None, None, False),
        "qb": ([functools.partial(store_transposed_normed, qbt_ref, c) for c in range(0, qbt_ref.shape[0], cw)],
               None, None, None, False),
        "kb": (rows_of(kb_ref), gkb_ref, bd64, HEAD_DIM, False),
        "vb": (cols_of(vbt_ref), None, None, None, False),
        "qm": (rows_of(qm_ref), gqm_ref, None, MEM_HEAD_DIM, False)}
    chunks = [(INPROJ_COL[name] + piece * cw, segments[name][0][piece]) + segments[name][1:]
              for name, piece in INPROJ_ORDER]
    z = proj(INPROJ_COL["fl"], FL_PAD) + bf_ref[...]
    y_next = proj(chunks[0][0])
    lf_ref[...] = jnp.minimum(z, 0.0) - jnp.log1p(jnp.exp(-jnp.abs(z)))
    for idx, (_, store, gain_ref, bd, hd, rot) in enumerate(chunks):
        y = y_next
        y_next = proj(chunks[idx + 1][0]) if idx + 1 < len(chunks) else None
        if gain_ref is not None and hd == LANES:
            gain = gain_ref[...]
            y = jnp.concatenate([_row_rmsnorm(y[:, c:c + hd], gain[:, c:c + hd]) for c in range(0, cw, hd)], axis=1)
        elif gain_ref is not None:
            y = _head_rmsnorm(y, bd, gain_ref[...], hd)
        if rot:
            y = rope(y)
        store(y)


def _cumsum_kernel(lf_ref, tri_ref, pk_ref, onek_ref, pq_ref, oneq_ref, cqt_ref, ck_ref):
    tri = tri_ref[...]
    n_blocks = lf_ref.shape[0] // CUMSUM_BLOCK
    n_hp = ck_ref.shape[0]
    carry = jnp.zeros((1, LANES), F32)
    for blk in range(n_blocks):
        rows = slice(blk * CUMSUM_BLOCK, (blk + 1) * CUMSUM_BLOCK)
        local = _dot(tri, jnp.concatenate(_split3(lf_ref[rows, :]), axis=1))
        c = local[:, :LANES] + local[:, LANES:2 * LANES] + local[:, 2 * LANES:] + carry
        carry = c[CUMSUM_BLOCK - 1:CUMSUM_BLOCK, :]
        c = c * LOG2E
        pieces = jnp.concatenate(_split3(c), axis=1)
        ke = _dot(pieces, pk_ref[...]) + onek_ref[...]
        qe = _dot_nt(pq_ref[...], pieces) + oneq_ref[...]
        for hp in range(n_hp):
            ck_ref[hp, rows, :] = ke[:, hp * LANES:(hp + 1) * LANES].astype(BF16)
            for e in range(2):
                r0 = (2 * hp + e) * DECAY_ROWS
                cqt_ref[hp, e, :, rows] = qe[r0:r0 + DECAY_ROWS, :].astype(BF16)


def _fox_kernel(qt_ref, cqt_ref, k_ref, ck_ref, vt_ref, o_ref, qa_ref, st_ref, bm_ref, m_ref, acc_ref):
    blk = pl.program_id(2)
    tq = FOX_Q_TILE
    tk = FOX_K_TILE
    hd = HEAD_DIM
    cw = FOX_COLS
    zq = jnp.zeros((hd, tq), BF16)
    zd = jnp.zeros((DECAY_ROWS, tq), BF16)
    zpad = jnp.zeros((MXU_COLS - 2 * hd - 2 * DECAY_ROWS, tq), BF16)
    for e in range(2):
        parts = [zq, zq, zd, zd, zpad]
        parts[e] = qt_ref[e * hd:(e + 1) * hd, :]
        parts[2 + e] = cqt_ref[e]
        qa_ref[e] = jnp.concatenate(parts, axis=0)
    units = [(e, c0) for e in range(2) for c0 in range(0, tq, cw)]

    def key_block(key_start):
        rows = pl.ds(pl.multiple_of(key_start, tk), tk)
        return jnp.concatenate([k_ref[rows, :], ck_ref[rows, :]], axis=1)

    def value_block(key_start):
        return vt_ref[:, pl.ds(pl.multiple_of(key_start, tk), tk)]

    def first_visible(u, key_off):
        c0 = units[u][1]
        return cw // 2 if key_off is not None and key_off >= c0 + cw // 2 else 0

    def issue_scores(buf, u, k_aug, key_off=None):
        e, c0 = units[u]
        lo = first_visible(u, key_off)
        st = _dot(k_aug, qa_ref[e, :, c0 + lo:c0 + cw])
        st_ref[buf, u, :, lo:] = st
        bm_ref[buf, u, :, lo:] = jnp.max(st, axis=0, keepdims=True)

    def update(buf, u, vt, key_off):
        e, c0 = units[u]
        lo = first_visible(u, key_off)
        st = st_ref[buf, u, :, lo:]
        if key_off is not None and key_off + tk - 1 > c0 + lo:
            assert key_off == c0 + lo
            masked = jnp.where(causal, st[:, :tk], MASKED)
            st = masked if st.shape[1] == tk else jnp.concatenate([masked, st[:, tk:]], axis=1)
            block_max = jnp.max(st, axis=0, keepdims=True)
        else:
            block_max = bm_ref[buf, u, :, lo:]
        m = m_ref[u, :, lo:]
        m_new = jnp.maximum(m, block_max)
        alpha = jnp.exp2(m - m_new)
        p = jnp.exp2(st - m_new).astype(BF16)
        vt_aug = jnp.concatenate([vt[e * hd:(e + 1) * hd, :], ones_rows], axis=0)
        acc_ref[u, :, lo:] = alpha * acc_ref[u, :, lo:] + _dot(vt_aug, p)
        m_ref[u, :, lo:] = m_new

    ones_rows = jnp.ones((FOX_DEN_ROWS, tk), BF16)
    causal = lax.broadcasted_iota(jnp.int32, (tk, tk), 0) <= lax.broadcasted_iota(jnp.int32, (tk, tk), 1)
    m_ref[...] = jnp.full(m_ref.shape, MASKED, F32)
    acc_ref[...] = jnp.zeros(acc_ref.shape, F32)
    k_first = key_block(0)
    for u in range(len(units)):
        issue_scores(0, u, k_first)

    per_trip = FOX_BLOCKS_PER_TRIP
    assert per_trip % 2 == 0 and (tq // tk) % per_trip == 0

    def block_group(jg, carry):
        for step in range(per_trip):
            j = per_trip * jg + step
            k_next = key_block((j + 1) * tk)
            vt = value_block(j * tk)
            for u in range(min(FOX_LOOKAHEAD, len(units))):
                issue_scores(1 - step % 2, u, k_next)
            for u in range(len(units)):
                if u + FOX_LOOKAHEAD < len(units):
                    issue_scores(1 - step % 2, u + FOX_LOOKAHEAD, k_next)
                update(step % 2, u, vt, None)
        return carry

    lax.fori_loop(0, blk * (tq // tk // per_trip), block_group, 0)
    for jj in range(tq // tk):
        key_off = jj * tk
        active = [u for u, (e, c0) in enumerate(units) if c0 + cw > key_off]
        nxt_off = key_off + tk
        nxt_active = [u for u, (e, c0) in enumerate(units) if c0 + cw > nxt_off] if nxt_off < tq else []
        if nxt_active:
            k_next = key_block(blk * tq + nxt_off)
        vt = value_block(blk * tq + key_off)
        for u in active:
            if u in nxt_active:
                issue_scores((jj + 1) % 2, u, k_next, nxt_off)
            update(jj % 2, u, vt, key_off)
    per_head = []
    for e in range(2):
        cols = [acc_ref[u, :hd, :] * (1.0 / acc_ref[u, hd:hd + 1, :]) for u, unit in enumerate(units) if unit[0] == e]
        per_head.append(jnp.concatenate(cols, axis=1))
    ot = jnp.concatenate(per_head, axis=0)
    o_ref[...] = ot.T.astype(o_ref.dtype)


def _dilated_kernel(hm_ref, q0_ref, k0_ref, kp0_ref, v0_ref, vp0_ref, q1_ref, k1_ref, kp1_ref, v1_ref, vp1_ref,
                    q2_ref, k2_ref, kp2_ref, v2_ref, vp2_ref, ya_ref, o_scr, l_scr):
    jt = pl.program_id(1)
    c = DIL_BACK
    w = DIL_OUT
    nh = DIL_HEADS
    tile = ya_ref.shape[0]
    qi = lax.broadcasted_iota(jnp.int32, (nh * c, 2 * c), 0) & (c - 1)
    kj = lax.broadcasted_iota(jnp.int32, (nh * c, 2 * c), 1)
    dist = qi + c - kj
    in_band = (dist >= 0) & (dist <= DIL_BACK)
    in_band_first = in_band & (kj >= jnp.where(jt > 0, 0, c))
    lane = lax.broadcasted_iota(jnp.int32, (c, w), 1)
    in_head = [(lane >= hh * HEAD_DIM) & (lane < (hh + 1) * HEAD_DIM) for hh in range(nh)]

    def attend(gi, d, items, mask):
        def scores(item):
            q, kk, _, _ = item
            return _dot_nt(jnp.concatenate([q * hm_ref[hh] for hh in range(nh)], axis=0), kk)

        s_next = scores(items[0])
        for idx, (_, _, vv, tok0) in enumerate(items):
            s = s_next
            s_next = scores(items[idx + 1]) if idx + 1 < len(items) else None
            s = jnp.where(mask, s, MASKED)
            m = jnp.max(s, axis=-1, keepdims=True)
            p = jnp.exp2(s - m)
            den = jnp.sum(p, axis=-1, keepdims=True)
            o4 = _dot(p.astype(BF16), vv) * (1.0 / den)
            lse4 = m + jnp.log2(den)
            o = jnp.zeros((c, w), F32)
            lse = jnp.zeros((c, w), F32)
            for hh in range(nh):
                rows = slice(hh * c, (hh + 1) * c)
                o = jnp.where(in_head[hh], o4[rows], o)
                lse = jnp.where(in_head[hh], lse4[rows], lse)
            for half in range(w // LANES):
                lanes = slice(half * LANES, (half + 1) * LANES)
                o_scr[gi, half, pl.ds(tok0, c, stride=d), :] = o[:, lanes]
                l_scr[gi, half, pl.ds(tok0, c, stride=d), :] = lse[:, lanes]

    groups = ((q0_ref, k0_ref, kp0_ref, v0_ref, vp0_ref), (q1_ref, k1_ref, kp1_ref, v1_ref, vp1_ref),
              (q2_ref, k2_ref, kp2_ref, v2_ref, vp2_ref))
    for gi, (q_ref, k_ref, kp_ref, v_ref, vp_ref) in enumerate(groups):
        d = q_ref.shape[0]
        n_sb = q_ref.shape[1] // c
        res_chunk = min(d, DIL_CHAINS)
        assert d % res_chunk == 0

        def first_item(r, q_ref=q_ref, k_ref=k_ref, kp_ref=kp_ref, v_ref=v_ref, vp_ref=vp_ref):
            kk = jnp.concatenate([kp_ref[r], k_ref[r, 0:c]], axis=0)
            vv = jnp.concatenate([vp_ref[r], v_ref[r, 0:c]], axis=0)
            return q_ref[r, 0:c], kk, vv, r

        def later_item(r, sb, d=d, q_ref=q_ref, k_ref=k_ref, v_ref=v_ref):
            start = pl.multiple_of(sb * c, c)
            window = pl.ds(start - c, 2 * c)
            return q_ref[r, pl.ds(start, c)], k_ref[r, window], v_ref[r, window], sb * (c * d) + r

        def first_chunk(it, carry, gi=gi, d=d, res_chunk=res_chunk, first_item=first_item):
            attend(gi, d, [first_item(it * res_chunk + i) for i in range(res_chunk)], in_band_first)
            return carry

        if d == res_chunk:
            first_chunk(0, 0)
        else:
            lax.fori_loop(0, d // res_chunk, first_chunk, 0)
        if n_sb > 1:
            sb_chunk = max(1, DIL_LATER_ITEMS // res_chunk)
            sb_chunk = max(k for k in range(1, sb_chunk + 1) if (n_sb - 1) % k == 0)
            assert d == res_chunk

            def later_chunk(it, carry, gi=gi, d=d, sb_chunk=sb_chunk, later_item=later_item):
                attend(gi, d, [later_item(r, 1 + it * sb_chunk + i) for i in range(sb_chunk) for r in range(d)],
                       in_band)
                return carry

            lax.fori_loop(0, (n_sb - 1) // sb_chunk, later_chunk, 0)

    def merge(ch, carry):
        rows = pl.ds(pl.multiple_of(ch * DIL_MERGE_ROWS, DIL_MERGE_ROWS), DIL_MERGE_ROWS)
        for half in range(w // LANES):
            l = [l_scr[g, half, rows, :] for g in range(N_DIL_GROUPS)]
            mx = jnp.maximum(jnp.maximum(l[0], l[1]), l[2])
            e = [jnp.exp2(lg - mx) for lg in l]
            num = e[0] * o_scr[0, half, rows, :] + e[1] * o_scr[1, half, rows, :] + e[2] * o_scr[2, half, rows, :]
            ya_ref[rows, half * LANES:(half + 1) * LANES] = (num / (e[0] + e[1] + e[2])).astype(ya_ref.dtype)
        return carry

    lax.fori_loop(0, tile // DIL_MERGE_ROWS, merge, 0)


def _memkv_kernel(mem_ref, g_ref, w_ref, gk_ref, km_ref, vm_ref):
    mn = _row_rmsnorm(mem_ref[...], g_ref[...]).astype(BF16)
    kv = _dot(mn, w_ref[...])
    gk = gk_ref[...]
    for hh in range(MEM_HEADS):
        cols = slice(hh * MEM_HEAD_DIM, (hh + 1) * MEM_HEAD_DIM)
        km_ref[:, cols] = _row_rmsnorm(kv[:, cols], gk).astype(km_ref.dtype)
    vm_ref[...] = kv[:, MEM_WIDTH:].astype(vm_ref.dtype)


def _merge_kernel(x_ref, h_ref, ya_ref, yb_ref, qm_ref, km_ref, vm_ref,
                  wg_ref, bg_ref, wa_ref, wb_ref, wm_ref, wo_ref, out_ref, merged_ref):
    ya = ya_ref[...]
    ym = []
    for hh in range(MEM_HEADS):
        cols = slice(hh * MEM_HEAD_DIM, (hh + 1) * MEM_HEAD_DIM)
        s = _dot_nt(qm_ref[:, cols], km_ref[:, cols])
        p = jnp.exp(s - jnp.max(s, axis=-1, keepdims=True))
        den = jnp.sum(p, axis=-1, keepdims=True)
        ym.append((_dot(p.astype(BF16), vm_ref[:, cols]) / den).astype(BF16))
    ym = jnp.concatenate(ym, axis=1)
    yb = yb_ref[...]
    h = h_ref[...]
    cw = 2 * MXU_COLS
    for c in range(0, D_MODEL, cw):
        merged = None
        for k, (y, w_ref) in enumerate(((ya, wa_ref), (yb, wb_ref), (ym, wm_ref))):
            gcol = k * D_MODEL + c
            gate = jax.nn.sigmoid(_dot(h, wg_ref[:, gcol:gcol + cw]) + bg_ref[:, gcol:gcol + cw])
            term = gate * _dot(y, w_ref[:, c:c + cw])
            merged = term if merged is None else merged + term
        merged_ref[:, c:c + cw] = merged.astype(BF16)
    out_ref[...] = x_ref[...] + _dot(merged_ref[...], wo_ref[...])


def _mlp_kernel(x_ref, g_ref, wu_ref, wd_ref, out_ref, h2_ref):
    x = x_ref[...]
    h2_ref[...] = _row_rmsnorm(x, g_ref[...]).astype(BF16)
    cw = D_MODEL
    acc = x
    for c in range(0, D_FF, cw):
        u = jnp.maximum(_dot(h2_ref[...], wu_ref[:, c:c + cw]), 0.0)
        acc = acc + _dot((u * u).astype(BF16), wd_ref[c:c + cw, :])
    out_ref[...] = acc


def _full(shape):
    return pl.BlockSpec(shape, lambda *_: (0,) * len(shape), pipeline_mode=pl.Buffered(1))


def _params(*sem):
    return pltpu.CompilerParams(dimension_semantics=sem)


def _blockdiag(width, head_dim):
    r = jnp.arange(width) // head_dim
    return (r[:, None] == r[None, :]).astype(BF16)


def _decay_placement(n_hp):
    pk = np.zeros((3 * LANES, n_hp * LANES), np.float32)
    onek = np.zeros((1, n_hp * LANES), np.float32)
    pq = np.zeros((n_hp * 2 * DECAY_ROWS, 3 * LANES), np.float32)
    oneq = np.zeros((n_hp * 2 * DECAY_ROWS, CUMSUM_BLOCK), np.float32)
    for hp in range(n_hp):
        for e in range(2):
            head = 2 * hp + e
            kcol = hp * LANES + DECAY_ROWS * e
            qrow = (2 * hp + e) * DECAY_ROWS
            for t in range(3):
                onek[0, kcol + t] = 1.0
                pk[t * LANES + head, kcol + 3 + t] = -1.0
                pq[qrow + t, t * LANES + head] = 1.0
                oneq[qrow + 3 + t, :] = 1.0
    return jnp.asarray(pk, BF16), jnp.asarray(onek), jnp.asarray(pq, BF16), jnp.asarray(oneq)


def _rope_tables(seq, width):
    half = ROT_DIM // 2
    inv_freq = ROPE_THETA ** (-jnp.arange(0, ROT_DIM, 2, dtype=F32) / ROT_DIM)
    ang = jnp.arange(seq, dtype=F32)[:, None] * inv_freq[None, :]
    cos, sin = jnp.cos(ang), jnp.sin(ang)
    ones = jnp.ones((seq, HEAD_DIM - ROT_DIM), F32)
    zeros = jnp.zeros((seq, HEAD_DIM - ROT_DIM), F32)
    zhalf = jnp.zeros((seq, half), F32)
    rc = jnp.concatenate([cos, cos, ones], axis=1)
    rs1 = jnp.concatenate([-sin, zhalf, zeros], axis=1)
    rs2 = jnp.concatenate([zhalf, sin, zeros], axis=1)
    reps = width // HEAD_DIM
    return tuple(jnp.tile(t, (1, reps)) for t in (rc, rs1, rs2))


def kernel(x, mem, g_mix, w_in, b_f, g_qA, g_kA, g_qB, g_kB, g_mem, w_mem_kv, g_qM, g_kM, w_gate, b_gate,
           w_br_a, w_br_b, w_br_m, w_out, g_mlp, w_up, w_down):
    B, T, D = x.shape
    assert D == D_MODEL and w_in.shape[0] == 1, "single-layer kernel"
    N = B * T
    mem_len = mem.shape[1]
    tm = ROW_TILE
    assert T % tm == 0 and T % FOX_Q_TILE == 0 and FOX_Q_TILE % FOX_K_TILE == 0
    x2d = x.reshape(N, D)

    wi = w_in[0]
    offs = [0]
    for wdt in (DIL_WIDTH, DIL_WIDTH, DIL_WIDTH, FOX_WIDTH, FOX_WIDTH, FOX_WIDTH, FOX_HEADS, MEM_WIDTH):
        offs.append(offs[-1] + wdt)
    w_fl = jnp.pad(wi[:, offs[6]:offs[7]], ((0, 0), (0, FL_PAD - FOX_HEADS)))
    w_all = jnp.concatenate([wi[:, :offs[6]], wi[:, offs[7]:], w_fl], axis=1).astype(BF16)
    assert offs[6] == INPROJ_COL["qm"] and w_all.shape[1] == INPROJ_COL["fl"] + FL_PAD
    n_cols = w_all.shape[1]
    bf_pad = jnp.pad(b_f[0], (0, FL_PAD - FOX_HEADS)).reshape(1, FL_PAD)
    cw = MXU_COLS
    q_scale = HEAD_DIM ** -0.5
    gqa = (jnp.tile(g_qA[0], cw // HEAD_DIM) * (q_scale * LOG2E)).reshape(1, cw)
    gka = jnp.tile(g_kA[0], cw // HEAD_DIM).reshape(1, cw)
    gqbt = jnp.broadcast_to((jnp.tile(g_qB[0], cw // HEAD_DIM) * (q_scale * LOG2E))[:, None], (cw, LANES))
    gkb = jnp.tile(g_kB[0], cw // HEAD_DIM).reshape(1, cw)
    gqm = (jnp.tile(g_qM[0], cw // MEM_HEAD_DIM) * MEM_HEAD_DIM ** -0.5).reshape(1, cw)
    rc, rs1, rs2 = _rope_tables(T, cw)
    bd64 = _blockdiag(cw, HEAD_DIM)

    row = lambda width: pl.BlockSpec((tm, width), lambda i: (i, 0))
    tm1 = INPROJ_ROW_TILE
    per_seq = T // tm1
    assert T % tm1 == 0
    row1 = lambda width: pl.BlockSpec((tm1, width), lambda i: (i, 0))
    rope_spec = pl.BlockSpec((tm1, cw), lambda i: (i % per_seq, 0))
    tspec = pl.BlockSpec((None, FOX_WIDTH, tm1), lambda i: (i // per_seq, 0, i % per_seq))
    tshape = jax.ShapeDtypeStruct((B, FOX_WIDTH, T), BF16)
    rshape = lambda width: jax.ShapeDtypeStruct((N, width), BF16)
    dils = [d for _, d in DIL_GROUPS]
    assert all(win // d == DIL_BACK and tm1 % d == 0 and (tm1 // d) % 16 == 0 for win, d in DIL_GROUPS)
    dspecs = [pl.BlockSpec((None, d, tm1 // d, DIL_OUT), lambda i: (i // per_seq, 0, i % per_seq, 0))
              for d in dils]
    dshapes = [jax.ShapeDtypeStruct((B, d, T // d, DIL_OUT), BF16) for d in dils]
    outs = pl.pallas_call(
        _inproj_kernel,
        grid=(N // tm1,),
        in_specs=[row1(D), _full((1, D)), _full((D, n_cols)), _full((cw, cw)),
                  _full((1, cw)), _full((1, cw)), _full((cw, LANES)), _full((1, cw)), _full((1, cw)),
                  _full((1, FL_PAD)), rope_spec, rope_spec, rope_spec],
        out_specs=[row1(D)] + dspecs * 3 + [tspec, row1(FOX_WIDTH), tspec, row1(MEM_WIDTH), row1(FL_PAD)],
        out_shape=[rshape(D)] + dshapes * 3 + [tshape, rshape(FOX_WIDTH), tshape, rshape(MEM_WIDTH),
                                               jax.ShapeDtypeStruct((N, FL_PAD), F32)],
        scratch_shapes=[pltpu.VMEM((cw // LANES, tm1, LANES), F32)],
        compiler_params=_params("parallel"),
        name="inproj",
    )(x2d, g_mix, w_all, bd64, gqa, gka, gqbt, gkb, gqm, bf_pad, rc, rs1, rs2)
    h, qa_g, ka_g, va_g = outs[0], outs[1:4], outs[4:7], outs[7:10]
    qbt, kb, vbt, qm, logf = outs[10:]

    n_hp = FOX_HEADS // 2
    tri = (jnp.arange(CUMSUM_BLOCK)[:, None] >= jnp.arange(CUMSUM_BLOCK)[None, :]).astype(BF16)
    pk, onek, pq, oneq = _decay_placement(n_hp)
    cqt, ck_ext = pl.pallas_call(
        _cumsum_kernel,
        grid=(B,),
        in_specs=[pl.BlockSpec((T, FL_PAD), lambda b: (b, 0)), _full((CUMSUM_BLOCK, CUMSUM_BLOCK)),
                  _full(pk.shape), _full(onek.shape), _full(pq.shape), _full(oneq.shape)],
        out_specs=[pl.BlockSpec((None, n_hp, 2, DECAY_ROWS, T), lambda b: (b, 0, 0, 0, 0)),
                   pl.BlockSpec((None, n_hp, T, LANES), lambda b: (b, 0, 0, 0))],
        out_shape=[jax.ShapeDtypeStruct((B, n_hp, 2, DECAY_ROWS, T), BF16),
                   jax.ShapeDtypeStruct((B, n_hp, T, LANES), BF16)],
        compiler_params=_params("parallel"),
        name="decay_cumsum",
    )(logf, tri, pk, onek, pq, oneq)

    tq = FOX_Q_TILE
    nq = T // tq
    n_units = 2 * tq // FOX_COLS
    yb = pl.pallas_call(
        _fox_kernel,
        grid=(B, n_hp, nq),
        in_specs=[pl.BlockSpec((None, LANES, tq), lambda b, hp, i: (b, hp, i)),
                  pl.BlockSpec((None, None, 2, DECAY_ROWS, tq), lambda b, hp, i: (b, hp, 0, 0, i)),
                  pl.BlockSpec((T, LANES), lambda b, hp, i: (b, hp)),
                  pl.BlockSpec((None, None, T, LANES), lambda b, hp, i: (b, hp, 0, 0)),
                  pl.BlockSpec((None, LANES, T), lambda b, hp, i: (b, hp, 0))],
        out_specs=pl.BlockSpec((tq, LANES), lambda b, hp, i: (b * nq + i, hp)),
        out_shape=jax.ShapeDtypeStruct((N, FOX_WIDTH), BF16),
        scratch_shapes=[pltpu.VMEM((2, MXU_COLS, tq), BF16),
                        pltpu.VMEM((2, n_units, FOX_K_TILE, FOX_COLS), F32),
                        pltpu.VMEM((2, n_units, 1, FOX_COLS), F32),
                        pltpu.VMEM((n_units, 1, FOX_COLS), F32),
                        pltpu.VMEM((n_units, HEAD_DIM + FOX_DEN_ROWS, FOX_COLS), F32)],
        compiler_params=_params("parallel", "parallel", "arbitrary"),
        name="fox_attention",
    )(qbt, cqt, kb, ck_ext, vbt)

    tile = DIL_TILE
    assert T % tile == 0 and all((tile // d) % DIL_BACK == 0 for d in dils)
    lane_head = jnp.arange(DIL_OUT) // HEAD_DIM
    head_mask = jnp.broadcast_to((lane_head[None, :] == jnp.arange(DIL_HEADS)[:, None])[:, None, :],
                                 (DIL_HEADS, DIL_BACK, DIL_OUT)).astype(BF16)
    dil_specs, dil_args = [], []
    for gi, d in enumerate(dils):
        per_tile = tile // d // DIL_BACK
        cur = pl.BlockSpec((None, d, tile // d, DIL_OUT), lambda b, j: (b, 0, j, 0))
        prev = pl.BlockSpec((None, d, DIL_BACK, DIL_OUT),
                            lambda b, j, per_tile=per_tile: (b, 0, jnp.maximum(j * per_tile - 1, 0), 0))
        dil_specs += [cur, cur, prev, cur, prev]
        dil_args += [qa_g[gi], ka_g[gi], ka_g[gi], va_g[gi], va_g[gi]]
    ya = pl.pallas_call(
        _dilated_kernel,
        grid=(B, T // tile),
        in_specs=[_full((DIL_HEADS, DIL_BACK, DIL_OUT))] + dil_specs,
        out_specs=pl.BlockSpec((tile, DIL_OUT), lambda b, j: (b * (T // tile) + j, 0)),
        out_shape=jax.ShapeDtypeStruct((N, DIL_OUT), BF16),
        scratch_shapes=[pltpu.VMEM((N_DIL_GROUPS, DIL_OUT // LANES, tile, LANES), F32)] * 2,
        compiler_params=_params("parallel", "arbitrary"),
        name="dilated_attention",
    )(head_mask, *dil_args)

    gkm = g_kM[0].reshape(1, MEM_HEAD_DIM)
    km, vm = pl.pallas_call(
        _memkv_kernel,
        grid=(B,),
        in_specs=[pl.BlockSpec((None, mem_len, D), lambda b: (b, 0, 0)), _full((1, D)),
                  _full((D, 2 * MEM_WIDTH)), _full((1, MEM_HEAD_DIM))],
        out_specs=[pl.BlockSpec((None, mem_len, MEM_WIDTH), lambda b: (b, 0, 0))] * 2,
        out_shape=[jax.ShapeDtypeStruct((B, mem_len, MEM_WIDTH), BF16)] * 2,
        compiler_params=_params("parallel"),
        name="mem_kv",
    )(mem, g_mem, w_mem_kv[0].astype(BF16), gkm)

    mem_spec = pl.BlockSpec((None, mem_len, MEM_WIDTH), lambda i: (i // (T // tm), 0, 0))
    x_mid = pl.pallas_call(
        _merge_kernel,
        grid=(N // tm,),
        in_specs=[row(D), row(D), row(DIL_OUT), row(FOX_WIDTH), row(MEM_WIDTH), mem_spec, mem_spec,
                  _full((D, 3 * D)), _full((1, 3 * D)), _full((DIL_OUT, D)), _full((FOX_WIDTH, D)),
                  _full((MEM_WIDTH, D)), _full((D, D))],
        out_specs=row(D),
        out_shape=jax.ShapeDtypeStruct((N, D), F32),
        scratch_shapes=[pltpu.VMEM((tm, D), BF16)],
        compiler_params=_params("parallel"),
        name="merge_outproj",
    )(x2d, h, ya, yb, qm, km, vm, w_gate[0].astype(BF16), b_gate, w_br_a[0].astype(BF16),
      w_br_b[0].astype(BF16), w_br_m[0].astype(BF16), w_out[0].astype(BF16))

    out = pl.pallas_call(
        _mlp_kernel,
        grid=(N // tm,),
        in_specs=[row(D), _full((1, D)), _full((D, D_FF)), _full((D_FF, D))],
        out_specs=row(D),
        out_shape=jax.ShapeDtypeStruct((N, D), F32),
        scratch_shapes=[pltpu.VMEM((tm, D), BF16)],
        compiler_params=_params("parallel"),
        name="mlp",
    )(x_mid, g_mlp, w_up[0].astype(BF16), w_down[0].astype(BF16))
    return out.reshape(B, T, D)
```

```python
import functools

import jax
import jax.numpy as jnp
import numpy as np
from jax import lax
from jax.experimental import pallas as pl
from jax.experimental.pallas import tpu as pltpu

D_MODEL = 1024
HEAD_DIM = 64
DIL_GROUPS = ((128, 1), (512, 4), (2048, 16))
N_DIL_GROUPS = 3
DIL_HEADS = 4
DIL_WIDTH = N_DIL_GROUPS * DIL_HEADS * HEAD_DIM
DIL_OUT = DIL_HEADS * HEAD_DIM
DIL_BACK = 128
FOX_HEADS = 8
FOX_WIDTH = FOX_HEADS * HEAD_DIM
MEM_HEADS = 4
MEM_HEAD_DIM = 128
MEM_WIDTH = MEM_HEADS * MEM_HEAD_DIM
ROT_DIM = HEAD_DIM // 4
ROPE_THETA = 500000.0
D_FF = 4 * D_MODEL
EPS = 1e-6

LANES = 128
MXU_COLS = 256
FL_PAD = LANES

ROW_TILE = 1024
INPROJ_ROW_TILE = 512
INPROJ_COL = dict(qa=0, ka=DIL_WIDTH, va=2 * DIL_WIDTH, qb=3 * DIL_WIDTH, kb=3 * DIL_WIDTH + FOX_WIDTH,
                  vb=3 * DIL_WIDTH + 2 * FOX_WIDTH, qm=3 * DIL_WIDTH + 3 * FOX_WIDTH,
                  fl=3 * DIL_WIDTH + 3 * FOX_WIDTH + MEM_WIDTH)
INPROJ_ORDER = (("qa", 0), ("ka", 0), ("qa", 1), ("ka", 1), ("qa", 2), ("ka", 2), ("qb", 0), ("kb", 0),
                ("qb", 1), ("kb", 1), ("qm", 0), ("qm", 1), ("va", 2), ("va", 1), ("vb", 0), ("vb", 1), ("va", 0))
FOX_Q_TILE = 2048
FOX_K_TILE = 256
FOX_COLS = 512
FOX_DEN_ROWS = 16
FOX_LOOKAHEAD = 1
FOX_BLOCKS_PER_TRIP = 4
DECAY_ROWS = 16
CUMSUM_BLOCK = 256
DIL_TILE = 2048
DIL_MERGE_ROWS = 256
DIL_CHAINS = 8
DIL_LATER_ITEMS = 12

LOG2E = 1.4426950408889634
MASKED = -jnp.inf
BF16 = jnp.bfloat16
F32 = jnp.float32
NT_DIMS = (((1,), (1,)), ((), ()))


def _dot(a, b):
    return jnp.dot(a, b, preferred_element_type=F32)


def _dot_nt(a, b):
    return lax.dot_general(a, b, NT_DIMS, preferred_element_type=F32)


def _row_rmsnorm(x, g):
    ms = jnp.mean(x * x, axis=-1, keepdims=True)
    return x * lax.rsqrt(ms + EPS) * g


def _head_rmsnorm(y, blockdiag, gain, head_dim):
    y2 = (y * y).astype(BF16)
    half = y.shape[0] // 2
    ss = jnp.concatenate([_dot(y2[:half], blockdiag), _dot(y2[half:], blockdiag)], axis=0)
    return y * lax.rsqrt(ss * (1.0 / head_dim) + EPS) * gain


def _split3(c):
    hi = c.astype(BF16)
    r1 = c - hi.astype(F32)
    mid = r1.astype(BF16)
    lo = (r1 - mid.astype(F32)).astype(BF16)
    return hi, mid, lo


def _inproj_kernel(x_ref, gmix_ref, w_ref, bd64_ref, gqa_ref, gka_ref, gqbt_ref, gkb_ref,
                   gqm_ref, bf_ref, rc_ref, rs1_ref, rs2_ref,
                   h_ref, qa0_ref, qa1_ref, qa2_ref, ka0_ref, ka1_ref, ka2_ref, va0_ref, va1_ref, va2_ref,
                   qbt_ref, kb_ref, vbt_ref, qm_ref, lf_ref, perm_ref):
    h = _row_rmsnorm(x_ref[...], gmix_ref[...]).astype(BF16)
    h_ref[...] = h
    bd64 = bd64_ref[...]
    rc = rc_ref[...]
    rs1 = rs1_ref[...]
    rs2 = rs2_ref[...]
    cw = MXU_COLS
    tm = x_ref.shape[0]

    def proj(col, width=cw):
        return _dot_nt(h, w_ref[col:col + width, :])

    def rope(y):
        return y * rc + pltpu.roll(y, cw - ROT_DIM // 2, 1) * rs1 + pltpu.roll(y, ROT_DIM // 2, 1) * rs2

    def store_rows(out_ref, c, y):
        out_ref[:, c:c + cw] = y.astype(out_ref.dtype)

    def store_transposed(out_ref, c, y):
        out_ref[c:c + cw, :] = y.T.astype(out_ref.dtype)

    def store_transposed_normed(out_ref, c, y):
        yt = y.T
        gain_col = gqbt_ref[...]
        parts = []
        for r0 in range(0, cw, HEAD_DIM):
            blk = yt[r0:r0 + HEAD_DIM]
            ss = jnp.sum(blk * blk, axis=0, keepdims=True)
            gain = jnp.concatenate([gain_col[r0:r0 + HEAD_DIM]] * (tm // LANES), axis=1)
            parts.append(blk * lax.rsqrt(ss * (1.0 / HEAD_DIM) + EPS) * gain)
        out_ref[c:c + cw, :] = jnp.concatenate(parts, axis=0).astype(out_ref.dtype)

    def store_by_residue(out_ref, y):
        d = out_ref.shape[0]
        if d == 1:
            out_ref[0] = y.astype(out_ref.dtype)
            return
        for half in range(cw // LANES):
            perm_ref[half] = y[:, half * LANES:(half + 1) * LANES]
        for r in range(d):
            out_ref[r] = jnp.concatenate(
                [perm_ref[half, pl.ds(r, tm // d, stride=d), :] for half in range(cw // LANES)],
                axis=1).astype(out_ref.dtype)

    rows_of = lambda ref: [(functools.partial(store_rows, ref, c)) for c in range(0, ref.shape[1], cw)]
    cols_of = lambda ref: [(functools.partial(store_transposed, ref, c)) for c in range(0, ref.shape[0], cw)]
    residues_of = lambda refs: [functools.partial(store_by_residue, ref) for ref in refs]
    segments = {
        "qa": (residues_of((qa0_ref, qa1_ref, qa2_ref)), gqa_ref, bd64, HEAD_DIM, True),
        "ka": (residues_of((ka0_ref, ka1_ref, ka2_ref)), gka_ref, bd64, HEAD_DIM, True),
        "va": (residues_of((va0_ref, va1_ref, va2_ref)), None, None, None, False),
        "qb": ([functools.partial(store_transposed_normed, qbt_ref, c) for c in range(0, qbt_ref.shape[0], cw)],
               None, None, None, False),
        "kb": (rows_of(kb_ref), gkb_ref, bd64, HEAD_DIM, False),
        "vb": (cols_of(vbt_ref), None, None, None, False),
        "qm": (rows_of(qm_ref), gqm_ref, None, MEM_HEAD_DIM, False)}
    chunks = [(INPROJ_COL[name] + piece * cw, segments[name][0][piece]) + segments[name][1:]
              for name, piece in INPROJ_ORDER]
    z = proj(INPROJ_COL["fl"], FL_PAD) + bf_ref[...]
    y_next = proj(chunks[0][0])
    lf_ref[...] = jnp.minimum(z, 0.0) - jnp.log1p(jnp.exp(-jnp.abs(z)))
    for idx, (_, store, gain_ref, bd, hd, rot) in enumerate(chunks):
        y = y_next
        y_next = proj(chunks[idx + 1][0]) if idx + 1 < len(chunks) else None
        if gain_ref is not None and hd == LANES:
            gain = gain_ref[...]
            y = jnp.concatenate([_row_rmsnorm(y[:, c:c + hd], gain[:, c:c + hd]) for c in range(0, cw, hd)], axis=1)
        elif gain_ref is not None:
            y = _head_rmsnorm(y, bd, gain_ref[...], hd)
        if rot:
            y = rope(y)
        store(y)


def _cumsum_kernel(lf_ref, tri_ref, pk_ref, onek_ref, pq_ref, oneq_ref, cqt_ref, ck_ref):
    tri = tri_ref[...]
    n_blocks = lf_ref.shape[0] // CUMSUM_BLOCK
    n_hp = ck_ref.shape[0]
    lane = lax.broadcasted_iota(jnp.int32, (CUMSUM_BLOCK, LANES), 1)
    carry = jnp.zeros((1, LANES), F32)
    for blk in range(n_blocks):
        rows = slice(blk * CUMSUM_BLOCK, (blk + 1) * CUMSUM_BLOCK)
        local = _dot(tri, jnp.concatenate(_split3(lf_ref[rows, :]), axis=1))
        c = local[:, :LANES] + local[:, LANES:2 * LANES] + local[:, 2 * LANES:] + carry
        carry = c[CUMSUM_BLOCK - 1:CUMSUM_BLOCK, :]
        c = c * LOG2E
        hi, mid, lo = (t.astype(F32) for t in _split3(c))
        pieces = jnp.where(lane < FOX_HEADS, hi,
                           jnp.where(lane < 2 * FOX_HEADS, pltpu.roll(mid, FOX_HEADS, 1),
                                     pltpu.roll(lo, 2 * FOX_HEADS, 1))).astype(BF16)
        ke = _dot(pieces, pk_ref[...]) + onek_ref[...]
        qe = _dot_nt(pq_ref[...], pieces) + oneq_ref[...]
        for hp in range(n_hp):
            ck_ref[hp, rows, :] = ke[:, hp * LANES:(hp + 1) * LANES].astype(BF16)
            for e in range(2):
                r0 = (2 * hp + e) * DECAY_ROWS
                cqt_ref[hp, e, :, rows] = qe[r0:r0 + DECAY_ROWS, :].astype(BF16)


def _fox_kernel(qt_ref, cqt_ref, k_ref, ck_ref, vt_ref, o_ref, qa_ref, st_ref, bm_ref, m_ref, acc_ref):
    blk = pl.program_id(2)
    tq = FOX_Q_TILE
    tk = FOX_K_TILE
    hd = HEAD_DIM
    cw = FOX_COLS
    zq = jnp.zeros((hd, tq), BF16)
    zd = jnp.zeros((DECAY_ROWS, tq), BF16)
    zpad = jnp.zeros((MXU_COLS - 2 * hd - 2 * DECAY_ROWS, tq), BF16)
    for e in range(2):
        parts = [zq, zq, zd, zd, zpad]
        parts[e] = qt_ref[e * hd:(e + 1) * hd, :]
        parts[2 + e] = cqt_ref[e]
        qa_ref[e] = jnp.concatenate(parts, axis=0)
    units = [(e, c0) for e in range(2) for c0 in range(0, tq, cw)]

    def key_block(key_start):
        rows = pl.ds(pl.multiple_of(key_start, tk), tk)
        return jnp.concatenate([k_ref[rows, :], ck_ref[rows, :]], axis=1)

    def value_block(key_start):
        return vt_ref[:, pl.ds(pl.multiple_of(key_start, tk), tk)]

    def first_visible(u, key_off):
        c0 = units[u][1]
        return cw // 2 if key_off is not None and key_off >= c0 + cw // 2 else 0

    def issue_scores(buf, u, k_aug, key_off=None):
        e, c0 = units[u]
        lo = first_visible(u, key_off)
        st = _dot(k_aug, qa_ref[e, :, c0 + lo:c0 + cw])
        st_ref[buf, u, :, lo:] = st
        bm_ref[buf, u, :, lo:] = jnp.max(st, axis=0, keepdims=True)

    def update(buf, u, vt, key_off):
        e, c0 = units[u]
        lo = first_visible(u, key_off)
        st = st_ref[buf, u, :, lo:]
        if key_off is not None and key_off + tk - 1 > c0 + lo:
            assert key_off == c0 + lo
            masked = jnp.where(causal, st[:, :tk], MASKED)
            st = masked if st.shape[1] == tk else jnp.concatenate([masked, st[:, tk:]], axis=1)
            block_max = jnp.max(st, axis=0, keepdims=True)
        else:
            block_max = bm_ref[buf, u, :, lo:]
        m = m_ref[u, :, lo:]
        m_new = jnp.maximum(m, block_max)
        alpha = jnp.exp2(m - m_new)
        p = jnp.exp2(st - m_new).astype(BF16)
        vt_aug = jnp.concatenate([vt[e * hd:(e + 1) * hd, :], ones_rows], axis=0)
        acc_ref[u, :, lo:] = alpha * acc_ref[u, :, lo:] + _dot(vt_aug, p)
        m_ref[u, :, lo:] = m_new

    ones_rows = jnp.ones((FOX_DEN_ROWS, tk), BF16)
    causal = lax.broadcasted_iota(jnp.int32, (tk, tk), 0) <= lax.broadcasted_iota(jnp.int32, (tk, tk), 1)
    m_ref[...] = jnp.full(m_ref.shape, MASKED, F32)
    acc_ref[...] = jnp.zeros(acc_ref.shape, F32)
    k_first = key_block(0)
    for u in range(len(units)):
        issue_scores(0, u, k_first)

    per_trip = FOX_BLOCKS_PER_TRIP
    assert per_trip % 2 == 0 and (tq // tk) % per_trip == 0

    def block_group(jg, carry):
        for step in range(per_trip):
            j = per_trip * jg + step
            k_next = key_block((j + 1) * tk)
            vt = value_block(j * tk)
            for u in range(min(FOX_LOOKAHEAD, len(units))):
                issue_scores(1 - step % 2, u, k_next)
            for u in range(len(units)):
                if u + FOX_LOOKAHEAD < len(units):
                    issue_scores(1 - step % 2, u + FOX_LOOKAHEAD, k_next)
                update(step % 2, u, vt, None)
        return carry

    lax.fori_loop(0, blk * (tq // tk // per_trip), block_group, 0)
    for jj in range(tq // tk):
        key_off = jj * tk
        active = [u for u, (e, c0) in enumerate(units) if c0 + cw > key_off]
        nxt_off = key_off + tk
        nxt_active = [u for u, (e, c0) in enumerate(units) if c0 + cw > nxt_off] if nxt_off < tq else []
        if nxt_active:
            k_next = key_block(blk * tq + nxt_off)
        vt = value_block(blk * tq + key_off)
        for u in active:
            if u in nxt_active:
                issue_scores((jj + 1) % 2, u, k_next, nxt_off)
            update(jj % 2, u, vt, key_off)
    per_head = []
    for e in range(2):
        cols = [acc_ref[u, :hd, :] * (1.0 / acc_ref[u, hd:hd + 1, :]) for u, unit in enumerate(units) if unit[0] == e]
        per_head.append(jnp.concatenate(cols, axis=1))
    ot = jnp.concatenate(per_head, axis=0)
    o_ref[...] = ot.T.astype(o_ref.dtype)


def _dilated_kernel(hm_ref, q0_ref, k0_ref, kp0_ref, v0_ref, vp0_ref, q1_ref, k1_ref, kp1_ref, v1_ref, vp1_ref,
                    q2_ref, k2_ref, kp2_ref, v2_ref, vp2_ref, ya_ref, o_scr, l_scr):
    jt = pl.program_id(1)
    c = DIL_BACK
    w = DIL_OUT
    nh = DIL_HEADS
    tile = ya_ref.shape[0]
    qi = lax.broadcasted_iota(jnp.int32, (nh * c, 2 * c), 0) & (c - 1)
    kj = lax.broadcasted_iota(jnp.int32, (nh * c, 2 * c), 1)
    dist = qi + c - kj
    in_band = (dist >= 0) & (dist <= DIL_BACK)
    in_band_first = in_band & (kj >= jnp.where(jt > 0, 0, c))
    lane = lax.broadcasted_iota(jnp.int32, (c, w), 1)
    in_head = [(lane >= hh * HEAD_DIM) & (lane < (hh + 1) * HEAD_DIM) for hh in range(nh)]

    def attend(gi, d, items, mask):
        def scores(item):
            q, kk, _, _ = item
            return _dot_nt(jnp.concatenate([q * hm_ref[hh] for hh in range(nh)], axis=0), kk)

        s_next = scores(items[0])
        for idx, (_, _, vv, tok0) in enumerate(items):
            s = s_next
            s_next = scores(items[idx + 1]) if idx + 1 < len(items) else None
            s = jnp.where(mask, s, MASKED)
            m = jnp.max(s, axis=-1, keepdims=True)
            p = jnp.exp2(s - m)
            den = jnp.sum(p, axis=-1, keepdims=True)
            o4 = _dot(p.astype(BF16), vv) * (1.0 / den)
            lse4 = m + jnp.log2(den)
            o = jnp.zeros((c, w), F32)
            lse = jnp.zeros((c, w), F32)
            for hh in range(nh):
                rows = slice(hh * c, (hh + 1) * c)
                o = jnp.where(in_head[hh], o4[rows], o)
                lse = jnp.where(in_head[hh], lse4[rows], lse)
            for half in range(w // LANES):
                lanes = slice(half * LANES, (half + 1) * LANES)
                o_scr[gi, half, pl.ds(tok0, c, stride=d), :] = o[:, lanes]
                l_scr[gi, half, pl.ds(tok0, c, stride=d), :] = lse[:, lanes]

    groups = ((q0_ref, k0_ref, kp0_ref, v0_ref, vp0_ref), (q1_ref, k1_ref, kp1_ref, v1_ref, vp1_ref),
              (q2_ref, k2_ref, kp2_ref, v2_ref, vp2_ref))
    for gi, (q_ref, k_ref, kp_ref, v_ref, vp_ref) in enumerate(groups):
        d = q_ref.shape[0]
        n_sb = q_ref.shape[1] // c
        res_chunk = min(d, DIL_CHAINS)
        assert d % res_chunk == 0

        def first_item(r, q_ref=q_ref, k_ref=k_ref, kp_ref=kp_ref, v_ref=v_ref, vp_ref=vp_ref):
            kk = jnp.concatenate([kp_ref[r], k_ref[r, 0:c]], axis=0)
            vv = jnp.concatenate([vp_ref[r], v_ref[r, 0:c]], axis=0)
            return q_ref[r, 0:c], kk, vv, r

        def later_item(r, sb, d=d, q_ref=q_ref, k_ref=k_ref, v_ref=v_ref):
            start = pl.multiple_of(sb * c, c)
            window = pl.ds(start - c, 2 * c)
            return q_ref[r, pl.ds(start, c)], k_ref[r, window], v_ref[r, window], sb * (c * d) + r

        def first_chunk(it, carry, gi=gi, d=d, res_chunk=res_chunk, first_item=first_item):
            attend(gi, d, [first_item(it * res_chunk + i) for i in range(res_chunk)], in_band_first)
            return carry

        if d == res_chunk:
            first_chunk(0, 0)
        else:
            lax.fori_loop(0, d // res_chunk, first_chunk, 0)
        if n_sb > 1:
            sb_chunk = max(1, DIL_LATER_ITEMS // res_chunk)
            sb_chunk = max(k for k in range(1, sb_chunk + 1) if (n_sb - 1) % k == 0)
            assert d == res_chunk

            def later_chunk(it, carry, gi=gi, d=d, sb_chunk=sb_chunk, later_item=later_item):
                attend(gi, d, [later_item(r, 1 + it * sb_chunk + i) for i in range(sb_chunk) for r in range(d)],
                       in_band)
                return carry

            lax.fori_loop(0, (n_sb - 1) // sb_chunk, later_chunk, 0)

    def merge(ch, carry):
        rows = pl.ds(pl.multiple_of(ch * DIL_MERGE_ROWS, DIL_MERGE_ROWS), DIL_MERGE_ROWS)
        for half in range(w // LANES):
            l = [l_scr[g, half, rows, :] for g in range(N_DIL_GROUPS)]
            mx = jnp.maximum(jnp.maximum(l[0], l[1]), l[2])
            e = [jnp.exp2(lg - mx) for lg in l]
            num = e[0] * o_scr[0, half, rows, :] + e[1] * o_scr[1, half, rows, :] + e[2] * o_scr[2, half, rows, :]
            ya_ref[rows, half * LANES:(half + 1) * LANES] = (num / (e[0] + e[1] + e[2])).astype(ya_ref.dtype)
        return carry

    lax.fori_loop(0, tile // DIL_MERGE_ROWS, merge, 0)


def _memkv_kernel(mem_ref, g_ref, w_ref, gk_ref, km_ref, vm_ref):
    mn = _row_rmsnorm(mem_ref[...], g_ref[...]).astype(BF16)
    kv = _dot(mn, w_ref[...])
    gk = gk_ref[...]
    for hh in range(MEM_HEADS):
        cols = slice(hh * MEM_HEAD_DIM, (hh + 1) * MEM_HEAD_DIM)
        km_ref[:, cols] = _row_rmsnorm(kv[:, cols], gk).astype(km_ref.dtype)
    vm_ref[...] = kv[:, MEM_WIDTH:].astype(vm_ref.dtype)


def _merge_kernel(x_ref, h_ref, ya_ref, yb_ref, qm_ref, km_ref, vm_ref,
                  wg_ref, bg_ref, wa_ref, wb_ref, wm_ref, wo_ref, out_ref, merged_ref):
    ya = ya_ref[...]
    ym = []
    for hh in range(MEM_HEADS):
        cols = slice(hh * MEM_HEAD_DIM, (hh + 1) * MEM_HEAD_DIM)
        s = _dot_nt(qm_ref[:, cols], km_ref[:, cols])
        p = jnp.exp(s - jnp.max(s, axis=-1, keepdims=True))
        den = jnp.sum(p, axis=-1, keepdims=True)
        ym.append((_dot(p.astype(BF16), vm_ref[:, cols]) / den).astype(BF16))
    ym = jnp.concatenate(ym, axis=1)
    yb = yb_ref[...]
    h = h_ref[...]
    cw = 2 * MXU_COLS
    for c in range(0, D_MODEL, cw):
        merged = None
        for k, (y, w_ref) in enumerate(((ya, wa_ref), (yb, wb_ref), (ym, wm_ref))):
            gcol = k * D_MODEL + c
            gate = jax.nn.sigmoid(_dot(h, wg_ref[:, gcol:gcol + cw]) + bg_ref[:, gcol:gcol + cw])
            term = gate * _dot(y, w_ref[:, c:c + cw])
            merged = term if merged is None else merged + term
        merged_ref[:, c:c + cw] = merged.astype(BF16)
    out_ref[...] = x_ref[...] + _dot(merged_ref[...], wo_ref[...])


def _mlp_kernel(x_ref, g_ref, wu_ref, wd_ref, out_ref, h2_ref):
    x = x_ref[...]
    h2_ref[...] = _row_rmsnorm(x, g_ref[...]).astype(BF16)
    cw = D_MODEL
    acc = x
    for c in range(0, D_FF, cw):
        u = jnp.maximum(_dot(h2_ref[...], wu_ref[:, c:c + cw]), 0.0)
        acc = acc + _dot((u * u).astype(BF16), wd_ref[c:c + cw, :])
    out_ref[...] = acc


def _full(shape):
    return pl.BlockSpec(shape, lambda *_: (0,) * len(shape), pipeline_mode=pl.Buffered(1))


def _params(*sem):
    return pltpu.CompilerParams(dimension_semantics=sem)


def _blockdiag(width, head_dim):
    r = jnp.arange(width) // head_dim
    return (r[:, None] == r[None, :]).astype(BF16)


def _decay_placement(n_hp):
    pk = np.zeros((LANES, n_hp * LANES), np.float32)
    onek = np.zeros((1, n_hp * LANES), np.float32)
    pq = np.zeros((n_hp * 2 * DECAY_ROWS, LANES), np.float32)
    oneq = np.zeros((n_hp * 2 * DECAY_ROWS, CUMSUM_BLOCK), np.float32)
    for hp in range(n_hp):
        for e in range(2):
            head = 2 * hp + e
            kcol = hp * LANES + DECAY_ROWS * e
            qrow = (2 * hp + e) * DECAY_ROWS
            for t in range(3):
                onek[0, kcol + t] = 1.0
                pk[t * FOX_HEADS + head, kcol + 3 + t] = -1.0
                pq[qrow + t, t * FOX_HEADS + head] = 1.0
                oneq[qrow + 3 + t, :] = 1.0
    return jnp.asarray(pk, BF16), jnp.asarray(onek), jnp.asarray(pq, BF16), jnp.asarray(oneq)


def _rope_tables(seq, width):
    half = ROT_DIM // 2
    inv_freq = ROPE_THETA ** (-jnp.arange(0, ROT_DIM, 2, dtype=F32) / ROT_DIM)
    ang = jnp.arange(seq, dtype=F32)[:, None] * inv_freq[None, :]
    cos, sin = jnp.cos(ang), jnp.sin(ang)
    ones = jnp.ones((seq, HEAD_DIM - ROT_DIM), F32)
    zeros = jnp.zeros((seq, HEAD_DIM - ROT_DIM), F32)
    zhalf = jnp.zeros((seq, half), F32)
    rc = jnp.concatenate([cos, cos, ones], axis=1)
    rs1 = jnp.concatenate([-sin, zhalf, zeros], axis=1)
    rs2 = jnp.concatenate([zhalf, sin, zeros], axis=1)
    reps = width // HEAD_DIM
    return tuple(jnp.tile(t, (1, reps)) for t in (rc, rs1, rs2))


def kernel(x, mem, g_mix, w_in, b_f, g_qA, g_kA, g_qB, g_kB, g_mem, w_mem_kv, g_qM, g_kM, w_gate, b_gate,
           w_br_a, w_br_b, w_br_m, w_out, g_mlp, w_up, w_down):
    B, T, D = x.shape
    assert D == D_MODEL and w_in.shape[0] == 1, "single-layer kernel"
    N = B * T
    mem_len = mem.shape[1]
    tm = ROW_TILE
    assert T % tm == 0 and T % FOX_Q_TILE == 0 and FOX_Q_TILE % FOX_K_TILE == 0
    x2d = x.reshape(N, D)

    wi = w_in[0]
    offs = [0]
    for wdt in (DIL_WIDTH, DIL_WIDTH, DIL_WIDTH, FOX_WIDTH, FOX_WIDTH, FOX_WIDTH, FOX_HEADS, MEM_WIDTH):
        offs.append(offs[-1] + wdt)
    wt = jnp.swapaxes(wi, 0, 1)
    wt_fl = jnp.pad(wt[offs[6]:offs[7]], ((0, FL_PAD - FOX_HEADS), (0, 0)))
    w_all = jnp.concatenate([wt[:offs[6]], wt[offs[7]:], wt_fl], axis=0).astype(BF16)
    assert offs[6] == INPROJ_COL["qm"] and w_all.shape[0] == INPROJ_COL["fl"] + FL_PAD
    n_cols = w_all.shape[0]
    bf_pad = jnp.pad(b_f[0], (0, FL_PAD - FOX_HEADS)).reshape(1, FL_PAD)
    cw = MXU_COLS
    q_scale = HEAD_DIM ** -0.5
    gqa = (jnp.tile(g_qA[0], cw // HEAD_DIM) * (q_scale * LOG2E)).reshape(1, cw)
    gka = jnp.tile(g_kA[0], cw // HEAD_DIM).reshape(1, cw)
    gqbt = jnp.broadcast_to((jnp.tile(g_qB[0], cw // HEAD_DIM) * (q_scale * LOG2E))[:, None], (cw, LANES))
    gkb = jnp.tile(g_kB[0], cw // HEAD_DIM).reshape(1, cw)
    gqm = (jnp.tile(g_qM[0], cw // MEM_HEAD_DIM) * MEM_HEAD_DIM ** -0.5).reshape(1, cw)
    rc, rs1, rs2 = _rope_tables(T, cw)
    bd64 = _blockdiag(cw, HEAD_DIM)

    row = lambda width: pl.BlockSpec((tm, width), lambda i: (i, 0))
    tm1 = INPROJ_ROW_TILE
    per_seq = T // tm1
    assert T % tm1 == 0
    row1 = lambda width: pl.BlockSpec((tm1, width), lambda i: (i, 0))
    rope_spec = pl.BlockSpec((tm1, cw), lambda i: (i % per_seq, 0))
    tspec = pl.BlockSpec((None, FOX_WIDTH, tm1), lambda i: (i // per_seq, 0, i % per_seq))
    tshape = jax.ShapeDtypeStruct((B, FOX_WIDTH, T), BF16)
    rshape = lambda width: jax.ShapeDtypeStruct((N, width), BF16)
    dils = [d for _, d in DIL_GROUPS]
    assert all(win // d == DIL_BACK and tm1 % d == 0 and (tm1 // d) % 16 == 0 for win, d in DIL_GROUPS)
    dspecs = [pl.BlockSpec((None, d, tm1 // d, DIL_OUT), lambda i: (i // per_seq, 0, i % per_seq, 0))
              for d in dils]
    dshapes = [jax.ShapeDtypeStruct((B, d, T // d, DIL_OUT), BF16) for d in dils]
    outs = pl.pallas_call(
        _inproj_kernel,
        grid=(N // tm1,),
        in_specs=[row1(D), _full((1, D)), _full((n_cols, D)), _full((cw, cw)),
                  _full((1, cw)), _full((1, cw)), _full((cw, LANES)), _full((1, cw)), _full((1, cw)),
                  _full((1, FL_PAD)), rope_spec, rope_spec, rope_spec],
        out_specs=[row1(D)] + dspecs * 3 + [tspec, row1(FOX_WIDTH), tspec, row1(MEM_WIDTH), row1(FL_PAD)],
        out_shape=[rshape(D)] + dshapes * 3 + [tshape, rshape(FOX_WIDTH), tshape, rshape(MEM_WIDTH),
                                               jax.ShapeDtypeStruct((N, FL_PAD), F32)],
        scratch_shapes=[pltpu.VMEM((cw // LANES, tm1, LANES), F32)],
        compiler_params=_params("parallel"),
        name="inproj",
    )(x2d, g_mix, w_all, bd64, gqa, gka, gqbt, gkb, gqm, bf_pad, rc, rs1, rs2)
    h, qa_g, ka_g, va_g = outs[0], outs[1:4], outs[4:7], outs[7:10]
    qbt, kb, vbt, qm, logf = outs[10:]

    n_hp = FOX_HEADS // 2
    tri = (jnp.arange(CUMSUM_BLOCK)[:, None] >= jnp.arange(CUMSUM_BLOCK)[None, :]).astype(BF16)
    pk, onek, pq, oneq = _decay_placement(n_hp)
    cqt, ck_ext = pl.pallas_call(
        _cumsum_kernel,
        grid=(B,),
        in_specs=[pl.BlockSpec((T, FL_PAD), lambda b: (b, 0)), _full((CUMSUM_BLOCK, CUMSUM_BLOCK)),
                  _full(pk.shape), _full(onek.shape), _full(pq.shape), _full(oneq.shape)],
        out_specs=[pl.BlockSpec((None, n_hp, 2, DECAY_ROWS, T), lambda b: (b, 0, 0, 0, 0)),
                   pl.BlockSpec((None, n_hp, T, LANES), lambda b: (b, 0, 0, 0))],
        out_shape=[jax.ShapeDtypeStruct((B, n_hp, 2, DECAY_ROWS, T), BF16),
                   jax.ShapeDtypeStruct((B, n_hp, T, LANES), BF16)],
        compiler_params=_params("parallel"),
        name="decay_cumsum",
    )(logf, tri, pk, onek, pq, oneq)

    tq = FOX_Q_TILE
    nq = T // tq
    n_units = 2 * tq // FOX_COLS
    yb = pl.pallas_call(
        _fox_kernel,
        grid=(B, n_hp, nq),
        in_specs=[pl.BlockSpec((None, LANES, tq), lambda b, hp, i: (b, hp, i)),
                  pl.BlockSpec((None, None, 2, DECAY_ROWS, tq), lambda b, hp, i: (b, hp, 0, 0, i)),
                  pl.BlockSpec((T, LANES), lambda b, hp, i: (b, hp)),
                  pl.BlockSpec((None, None, T, LANES), lambda b, hp, i: (b, hp, 0, 0)),
                  pl.BlockSpec((None, LANES, T), lambda b, hp, i: (b, hp, 0))],
        out_specs=pl.BlockSpec((tq, LANES), lambda b, hp, i: (b * nq + i, hp)),
        out_shape=jax.ShapeDtypeStruct((N, FOX_WIDTH), BF16),
        scratch_shapes=[pltpu.VMEM((2, MXU_COLS, tq), BF16),
                        pltpu.VMEM((2, n_units, FOX_K_TILE, FOX_COLS), F32),
                        pltpu.VMEM((2, n_units, 1, FOX_COLS), F32),
                        pltpu.VMEM((n_units, 1, FOX_COLS), F32),
                        pltpu.VMEM((n_units, HEAD_DIM + FOX_DEN_ROWS, FOX_COLS), F32)],
        compiler_params=_params("parallel", "parallel", "arbitrary"),
        name="fox_attention",
    )(qbt, cqt, kb, ck_ext, vbt)

    tile = DIL_TILE
    assert T % tile == 0 and all((tile // d) % DIL_BACK == 0 for d in dils)
    lane_head = jnp.arange(DIL_OUT) // HEAD_DIM
    head_mask = jnp.broadcast_to((lane_head[None, :] == jnp.arange(DIL_HEADS)[:, None])[:, None, :],
                                 (DIL_HEADS, DIL_BACK, DIL_OUT)).astype(BF16)
    dil_specs, dil_args = [], []
    for gi, d in enumerate(dils):
        per_tile = tile // d // DIL_BACK
        cur = pl.BlockSpec((None, d, tile // d, DIL_OUT), lambda b, j: (b, 0, j, 0))
        prev = pl.BlockSpec((None, d, DIL_BACK, DIL_OUT),
                            lambda b, j, per_tile=per_tile: (b, 0, jnp.maximum(j * per_tile - 1, 0), 0))
        dil_specs += [cur, cur, prev, cur, prev]
        dil_args += [qa_g[gi], ka_g[gi], ka_g[gi], va_g[gi], va_g[gi]]
    ya = pl.pallas_call(
        _dilated_kernel,
        grid=(B, T // tile),
        in_specs=[_full((DIL_HEADS, DIL_BACK, DIL_OUT))] + dil_specs,
        out_specs=pl.BlockSpec((tile, DIL_OUT), lambda b, j: (b * (T // tile) + j, 0)),
        out_shape=jax.ShapeDtypeStruct((N, DIL_OUT), BF16),
        scratch_shapes=[pltpu.VMEM((N_DIL_GROUPS, DIL_OUT // LANES, tile, LANES), F32)] * 2,
        compiler_params=_params("parallel", "arbitrary"),
        name="dilated_attention",
    )(head_mask, *dil_args)

    gkm = g_kM[0].reshape(1, MEM_HEAD_DIM)
    km, vm = pl.pallas_call(
        _memkv_kernel,
        grid=(B,),
        in_specs=[pl.BlockSpec((None, mem_len, D), lambda b: (b, 0, 0)), _full((1, D)),
                  _full((D, 2 * MEM_WIDTH)), _full((1, MEM_HEAD_DIM))],
        out_specs=[pl.BlockSpec((None, mem_len, MEM_WIDTH), lambda b: (b, 0, 0))] * 2,
        out_shape=[jax.ShapeDtypeStruct((B, mem_len, MEM_WIDTH), BF16)] * 2,
        compiler_params=_params("parallel"),
        name="mem_kv",
    )(mem, g_mem, w_mem_kv[0].astype(BF16), gkm)

    mem_spec = pl.BlockSpec((None, mem_len, MEM_WIDTH), lambda i: (i // (T // tm), 0, 0))
    x_mid = pl.pallas_call(
        _merge_kernel,
        grid=(N // tm,),
        in_specs=[row(D), row(D), row(DIL_OUT), row(FOX_WIDTH), row(MEM_WIDTH), mem_spec, mem_spec,
                  _full((D, 3 * D)), _full((1, 3 * D)), _full((DIL_OUT, D)), _full((FOX_WIDTH, D)),
                  _full((MEM_WIDTH, D)), _full((D, D))],
        out_specs=row(D),
        out_shape=jax.ShapeDtypeStruct((N, D), F32),
        scratch_shapes=[pltpu.VMEM((tm, D), BF16)],
        compiler_params=_params("parallel"),
        name="merge_outproj",
    )(x2d, h, ya, yb, qm, km, vm, w_gate[0].astype(BF16), b_gate, w_br_a[0].astype(BF16),
      w_br_b[0].astype(BF16), w_br_m[0].astype(BF16), w_out[0].astype(BF16))

    out = pl.pallas_call(
        _mlp_kernel,
        grid=(N // tm,),
        in_specs=[row(D), _full((1, D)), _full((D, D_FF)), _full((D_FF, D))],
        out_specs=row(D),
        out_shape=jax.ShapeDtypeStruct((N, D), F32),
        scratch_shapes=[pltpu.VMEM((tm, D), BF16)],
        compiler_params=_params("parallel"),
        name="mlp",
    )(x_mid, g_mlp, w_up[0].astype(BF16), w_down[0].astype(BF16))
    return out.reshape(B, T, D)
```

```python
import functools

import jax
import jax.numpy as jnp
import numpy as np
from jax import lax
from jax.experimental import pallas as pl
from jax.experimental.pallas import tpu as pltpu

D_MODEL = 1024
HEAD_DIM = 64
DIL_GROUPS = ((128, 1), (512, 4), (2048, 16))
N_DIL_GROUPS = 3
DIL_HEADS = 4
DIL_WIDTH = N_DIL_GROUPS * DIL_HEADS * HEAD_DIM
DIL_OUT = DIL_HEADS * HEAD_DIM
DIL_BACK = 128
FOX_HEADS = 8
FOX_WIDTH = FOX_HEADS * HEAD_DIM
MEM_HEADS = 4
MEM_HEAD_DIM = 128
MEM_WIDTH = MEM_HEADS * MEM_HEAD_DIM
ROT_DIM = HEAD_DIM // 4
ROPE_THETA = 500000.0
D_FF = 4 * D_MODEL
EPS = 1e-6

LANES = 128
MXU_COLS = 256
FL_PAD = LANES

ROW_TILE = 1024
INPROJ_ROW_TILE = 1024
INPROJ_COL = dict(qa=0, ka=DIL_WIDTH, va=2 * DIL_WIDTH, qb=3 * DIL_WIDTH, kb=3 * DIL_WIDTH + FOX_WIDTH,
                  vb=3 * DIL_WIDTH + 2 * FOX_WIDTH, qm=3 * DIL_WIDTH + 3 * FOX_WIDTH,
                  fl=3 * DIL_WIDTH + 3 * FOX_WIDTH + MEM_WIDTH)
INPROJ_ORDER = (("qa", 0), ("ka", 0), ("qa", 1), ("ka", 1), ("qa", 2), ("ka", 2), ("qb", 0), ("kb", 0),
                ("qb", 1), ("kb", 1), ("qm", 0), ("qm", 1), ("va", 2), ("va", 1), ("vb", 0), ("vb", 1), ("va", 0))
FOX_Q_TILE = 2048
FOX_K_TILE = 256
FOX_COLS = 512
FOX_DEN_ROWS = 16
FOX_LOOKAHEAD = 1
FOX_BLOCKS_PER_TRIP = 4
DECAY_ROWS = 16
CUMSUM_BLOCK = 256
DIL_TILE = 2048
DIL_MERGE_ROWS = 256
DIL_CHAINS = 16
DIL_LATER_ITEMS = 15

LOG2E = 1.4426950408889634
MASKED = -jnp.inf
BF16 = jnp.bfloat16
F32 = jnp.float32
NT_DIMS = (((1,), (1,)), ((), ()))


def _dot(a, b):
    return jnp.dot(a, b, preferred_element_type=F32)


def _dot_nt(a, b):
    return lax.dot_general(a, b, NT_DIMS, preferred_element_type=F32)


def _row_rmsnorm(x, g):
    ms = jnp.mean(x * x, axis=-1, keepdims=True)
    return x * lax.rsqrt(ms + EPS) * g


def _head_rmsnorm(y, blockdiag, gain, head_dim):
    y2 = (y * y).astype(BF16)
    half = y.shape[0] // 2
    ss = jnp.concatenate([_dot(y2[:half], blockdiag), _dot(y2[half:], blockdiag)], axis=0)
    return y * lax.rsqrt(ss * (1.0 / head_dim) + EPS) * gain


def _split3(c):
    hi = c.astype(BF16)
    r1 = c - hi.astype(F32)
    mid = r1.astype(BF16)
    lo = (r1 - mid.astype(F32)).astype(BF16)
    return hi, mid, lo


def _inproj_kernel(x_ref, gmix_ref, w_ref, bd64_ref, gqa_ref, gka_ref, gqbt_ref, gkb_ref,
                   gqm_ref, bf_ref, rc_ref, rs1_ref, rs2_ref,
                   h_ref, qa0_ref, qa1_ref, qa2_ref, ka0_ref, ka1_ref, ka2_ref, va0_ref, va1_ref, va2_ref,
                   qbt_ref, kb_ref, vbt_ref, qm_ref, lf_ref, perm_ref):
    h = _row_rmsnorm(x_ref[...], gmix_ref[...]).astype(BF16)
    h_ref[...] = h
    bd64 = bd64_ref[...]
    rc = rc_ref[...]
    rs1 = rs1_ref[...]
    rs2 = rs2_ref[...]
    cw = MXU_COLS
    tm = x_ref.shape[0]

    def proj(col, width=cw):
        return _dot(h, w_ref[:, col:col + width])

    def rope(y):
        return y * rc + pltpu.roll(y, cw - ROT_DIM // 2, 1) * rs1 + pltpu.roll(y, ROT_DIM // 2, 1) * rs2

    def store_rows(out_ref, c, y):
        out_ref[:, c:c + cw] = y.astype(out_ref.dtype)

    def store_transposed(out_ref, c, y):
        out_ref[c:c + cw, :] = y.T.astype(out_ref.dtype)

    def store_transposed_normed(out_ref, c, y):
        yt = y.T
        gain_col = gqbt_ref[...]
        parts = []
        for r0 in range(0, cw, HEAD_DIM):
            blk = yt[r0:r0 + HEAD_DIM]
            ss = jnp.sum(blk * blk, axis=0, keepdims=True)
            gain = jnp.concatenate([gain_col[r0:r0 + HEAD_DIM]] * (tm // LANES), axis=1)
            parts.append(blk * lax.rsqrt(ss * (1.0 / HEAD_DIM) + EPS) * gain)
        out_ref[c:c + cw, :] = jnp.concatenate(parts, axis=0).astype(out_ref.dtype)

    def store_by_residue(out_ref, y):
        d = out_ref.shape[0]
        if d == 1:
            out_ref[0] = y.astype(out_ref.dtype)
            return
        for half in range(cw // LANES):
            perm_ref[half] = y[:, half * LANES:(half + 1) * LANES]
        for r in range(d):
            out_ref[r] = jnp.concatenate(
                [perm_ref[half, pl.ds(r, tm // d, stride=d), :] for half in range(cw // LANES)],
                axis=1).astype(out_ref.dtype)

    rows_of = lambda ref: [(functools.partial(store_rows, ref, c)) for c in range(0, ref.shape[1], cw)]
    cols_of = lambda ref: [(functools.partial(store_transposed, ref, c)) for c in range(0, ref.shape[0], cw)]
    residues_of = lambda refs: [functools.partial(store_by_residue, ref) for ref in refs]
    segments = {
        "qa": (residues_of((qa0_ref, qa1_ref, qa2_ref)), gqa_ref, bd64, HEAD_DIM, True),
        "ka": (residues_of((ka0_ref, ka1_ref, ka2_ref)), gka_ref, bd64, HEAD_DIM, True),
        "va": (residues_of((va0_ref, va1_ref, va2_ref)), None, None, None, False),
        "qb": ([functools.partial(store_transposed_normed, qbt_ref, c) for c in range(0, qbt_ref.shape[0], cw)],
               None, None, None, False),
        "kb": (rows_of(kb_ref), gkb_ref, bd64, HEAD_DIM, False),
        "vb": (cols_of(vbt_ref), None, None, None, False),
        "qm": (rows_of(qm_ref), gqm_ref, None, MEM_HEAD_DIM, False)}
    chunks = [(INPROJ_COL[name] + piece * cw, segments[name][0][piece]) + segments[name][1:]
              for name, piece in INPROJ_ORDER]
    z = proj(INPROJ_COL["fl"], FL_PAD) + bf_ref[...]
    y_next = proj(chunks[0][0])
    lf_ref[...] = jnp.minimum(z, 0.0) - jnp.log1p(jnp.exp(-jnp.abs(z)))
    for idx, (_, store, gain_ref, bd, hd, rot) in enumerate(chunks):
        y = y_next
        y_next = proj(chunks[idx + 1][0]) if idx + 1 < len(chunks) else None
        if gain_ref is not None and hd == LANES:
            gain = gain_ref[...]
            y = jnp.concatenate([_row_rmsnorm(y[:, c:c + hd], gain[:, c:c + hd]) for c in range(0, cw, hd)], axis=1)
        elif gain_ref is not None:
            y = _head_rmsnorm(y, bd, gain_ref[...], hd)
        if rot:
            y = rope(y)
        store(y)


def _cumsum_kernel(lf_ref, tri_ref, pk_ref, onek_ref, pq_ref, oneq_ref, cqt_ref, ck_ref):
    tri = tri_ref[...]
    n_blocks = lf_ref.shape[0] // CUMSUM_BLOCK
    n_hp = ck_ref.shape[0]
    lane = lax.broadcasted_iota(jnp.int32, (CUMSUM_BLOCK, LANES), 1)
    carry = jnp.zeros((1, LANES), F32)
    for blk in range(n_blocks):
        rows = slice(blk * CUMSUM_BLOCK, (blk + 1) * CUMSUM_BLOCK)
        local = _dot(tri, jnp.concatenate(_split3(lf_ref[rows, :]), axis=1))
        c = local[:, :LANES] + local[:, LANES:2 * LANES] + local[:, 2 * LANES:] + carry
        carry = c[CUMSUM_BLOCK - 1:CUMSUM_BLOCK, :]
        c = c * LOG2E
        hi, mid, lo = (t.astype(F32) for t in _split3(c))
        pieces = jnp.where(lane < FOX_HEADS, hi,
                           jnp.where(lane < 2 * FOX_HEADS, pltpu.roll(mid, FOX_HEADS, 1),
                                     pltpu.roll(lo, 2 * FOX_HEADS, 1))).astype(BF16)
        ke = _dot(pieces, pk_ref[...]) + onek_ref[...]
        qe = _dot_nt(pq_ref[...], pieces) + oneq_ref[...]
        for hp in range(n_hp):
            ck_ref[hp, rows, :] = ke[:, hp * LANES:(hp + 1) * LANES].astype(BF16)
            for e in range(2):
                r0 = (2 * hp + e) * DECAY_ROWS
                cqt_ref[hp, e, :, rows] = qe[r0:r0 + DECAY_ROWS, :].astype(BF16)


def _fox_kernel(qt_ref, cqt_ref, k_ref, ck_ref, vt_ref, o_ref, qa_ref, st_ref, bm_ref, m_ref, acc_ref):
    blk = pl.program_id(2)
    tq = FOX_Q_TILE
    tk = FOX_K_TILE
    hd = HEAD_DIM
    cw = FOX_COLS
    zq = jnp.zeros((hd, tq), BF16)
    zd = jnp.zeros((DECAY_ROWS, tq), BF16)
    zpad = jnp.zeros((MXU_COLS - 2 * hd - 2 * DECAY_ROWS, tq), BF16)
    for e in range(2):
        parts = [zq, zq, zd, zd, zpad]
        parts[e] = qt_ref[e * hd:(e + 1) * hd, :]
        parts[2 + e] = cqt_ref[e]
        qa_ref[e] = jnp.concatenate(parts, axis=0)
    units = [(e, c0) for e in range(2) for c0 in range(0, tq, cw)]

    def key_block(key_start):
        rows = pl.ds(pl.multiple_of(key_start, tk), tk)
        return jnp.concatenate([k_ref[rows, :], ck_ref[rows, :]], axis=1)

    def value_block(key_start):
        return vt_ref[:, pl.ds(pl.multiple_of(key_start, tk), tk)]

    def first_visible(u, key_off):
        c0 = units[u][1]
        return cw // 2 if key_off is not None and key_off >= c0 + cw // 2 else 0

    def issue_scores(buf, u, k_aug, key_off=None):
        e, c0 = units[u]
        lo = first_visible(u, key_off)
        st = _dot(k_aug, qa_ref[e, :, c0 + lo:c0 + cw])
        st_ref[buf, u, :, lo:] = st
        bm_ref[buf, u, :, lo:] = jnp.max(st, axis=0, keepdims=True)

    def update(buf, u, vt, key_off):
        e, c0 = units[u]
        lo = first_visible(u, key_off)
        st = st_ref[buf, u, :, lo:]
        if key_off is not None and key_off + tk - 1 > c0 + lo:
            assert key_off == c0 + lo
            masked = jnp.where(causal, st[:, :tk], MASKED)
            st = masked if st.shape[1] == tk else jnp.concatenate([masked, st[:, tk:]], axis=1)
            block_max = jnp.max(st, axis=0, keepdims=True)
        else:
            block_max = bm_ref[buf, u, :, lo:]
        m = m_ref[u, :, lo:]
        m_new = jnp.maximum(m, block_max)
        alpha = jnp.exp2(m - m_new)
        p = jnp.exp2(st - m_new).astype(BF16)
        vt_aug = jnp.concatenate([vt[e * hd:(e + 1) * hd, :], ones_rows], axis=0)
        acc_ref[u, :, lo:] = alpha * acc_ref[u, :, lo:] + _dot(vt_aug, p)
        m_ref[u, :, lo:] = m_new

    ones_rows = jnp.ones((FOX_DEN_ROWS, tk), BF16)
    causal = lax.broadcasted_iota(jnp.int32, (tk, tk), 0) <= lax.broadcasted_iota(jnp.int32, (tk, tk), 1)
    m_ref[...] = jnp.full(m_ref.shape, MASKED, F32)
    acc_ref[...] = jnp.zeros(acc_ref.shape, F32)
    k_first = key_block(0)
    for u in range(len(units)):
        issue_scores(0, u, k_first)

    per_trip = FOX_BLOCKS_PER_TRIP
    assert per_trip % 2 == 0 and (tq // tk) % per_trip == 0

    def block_group(jg, carry):
        for step in range(per_trip):
            j = per_trip * jg + step
            k_next = key_block((j + 1) * tk)
            vt = value_block(j * tk)
            for u in range(min(FOX_LOOKAHEAD, len(units))):
                issue_scores(1 - step % 2, u, k_next)
            for u in range(len(units)):
                if u + FOX_LOOKAHEAD < len(units):
                    issue_scores(1 - step % 2, u + FOX_LOOKAHEAD, k_next)
                update(step % 2, u, vt, None)
        return carry

    lax.fori_loop(0, blk * (tq // tk // per_trip), block_group, 0)
    for jj in range(tq // tk):
        key_off = jj * tk
        active = [u for u, (e, c0) in enumerate(units) if c0 + cw > key_off]
        nxt_off = key_off + tk
        nxt_active = [u for u, (e, c0) in enumerate(units) if c0 + cw > nxt_off] if nxt_off < tq else []
        if nxt_active:
            k_next = key_block(blk * tq + nxt_off)
        vt = value_block(blk * tq + key_off)
        for u in active:
            if u in nxt_active:
                issue_scores((jj + 1) % 2, u, k_next, nxt_off)
            update(jj % 2, u, vt, key_off)
    per_head = []
    for e in range(2):
        cols = [acc_ref[u, :hd, :] * (1.0 / acc_ref[u, hd:hd + 1, :]) for u, unit in enumerate(units) if unit[0] == e]
        per_head.append(jnp.concatenate(cols, axis=1))
    ot = jnp.concatenate(per_head, axis=0)
    o_ref[...] = ot.T.astype(o_ref.dtype)


def _dilated_kernel(hm_ref, q0_ref, k0_ref, kp0_ref, v0_ref, vp0_ref, q1_ref, k1_ref, kp1_ref, v1_ref, vp1_ref,
                    q2_ref, k2_ref, kp2_ref, v2_ref, vp2_ref, ya_ref, o_scr, l_scr):
    jt = pl.program_id(1)
    c = DIL_BACK
    w = DIL_OUT
    nh = DIL_HEADS
    tile = ya_ref.shape[0]
    qi = lax.broadcasted_iota(jnp.int32, (nh * c, 2 * c), 0) & (c - 1)
    kj = lax.broadcasted_iota(jnp.int32, (nh * c, 2 * c), 1)
    dist = qi + c - kj
    in_band = (dist >= 0) & (dist <= DIL_BACK)
    in_band_first = in_band & (kj >= jnp.where(jt > 0, 0, c))
    lane = lax.broadcasted_iota(jnp.int32, (c, w), 1)
    in_head = [(lane >= hh * HEAD_DIM) & (lane < (hh + 1) * HEAD_DIM) for hh in range(nh)]

    def attend(gi, d, items, mask):
        def scores(item):
            q, kk, _, _ = item
            return _dot_nt(jnp.concatenate([q * hm_ref[hh] for hh in range(nh)], axis=0), kk)

        s_next = scores(items[0])
        for idx, (_, _, vv, tok0) in enumerate(items):
            s = s_next
            s_next = scores(items[idx + 1]) if idx + 1 < len(items) else None
            s = jnp.where(mask, s, MASKED)
            m = jnp.max(s, axis=-1, keepdims=True)
            p = jnp.exp2(s - m)
            den = jnp.sum(p, axis=-1, keepdims=True)
            o4 = _dot(p.astype(BF16), vv) * (1.0 / den)
            lse4 = m + jnp.log2(den)
            o = jnp.zeros((c, w), F32)
            lse = jnp.zeros((c, w), F32)
            for hh in range(nh):
                rows = slice(hh * c, (hh + 1) * c)
                o = jnp.where(in_head[hh], o4[rows], o)
                lse = jnp.where(in_head[hh], lse4[rows], lse)
            for half in range(w // LANES):
                lanes = slice(half * LANES, (half + 1) * LANES)
                o_scr[gi, half, pl.ds(tok0, c, stride=d), :] = o[:, lanes]
                l_scr[gi, half, pl.ds(tok0, c, stride=d), :] = lse[:, lanes]

    groups = ((q0_ref, k0_ref, kp0_ref, v0_ref, vp0_ref), (q1_ref, k1_ref, kp1_ref, v1_ref, vp1_ref),
              (q2_ref, k2_ref, kp2_ref, v2_ref, vp2_ref))
    for gi, (q_ref, k_ref, kp_ref, v_ref, vp_ref) in enumerate(groups):
        d = q_ref.shape[0]
        n_sb = q_ref.shape[1] // c
        res_chunk = min(d, DIL_CHAINS)
        assert d % res_chunk == 0

        def first_item(r, q_ref=q_ref, k_ref=k_ref, kp_ref=kp_ref, v_ref=v_ref, vp_ref=vp_ref):
            kk = jnp.concatenate([kp_ref[r], k_ref[r, 0:c]], axis=0)
            vv = jnp.concatenate([vp_ref[r], v_ref[r, 0:c]], axis=0)
            return q_ref[r, 0:c], kk, vv, r

        def later_item(r, sb, d=d, q_ref=q_ref, k_ref=k_ref, v_ref=v_ref):
            start = pl.multiple_of(sb * c, c)
            window = pl.ds(start - c, 2 * c)
            return q_ref[r, pl.ds(start, c)], k_ref[r, window], v_ref[r, window], sb * (c * d) + r

        def first_chunk(it, carry, gi=gi, d=d, res_chunk=res_chunk, first_item=first_item):
            attend(gi, d, [first_item(it * res_chunk + i) for i in range(res_chunk)], in_band_first)
            return carry

        if d == res_chunk:
            first_chunk(0, 0)
        else:
            lax.fori_loop(0, d // res_chunk, first_chunk, 0)
        if n_sb > 1:
            sb_chunk = max(1, DIL_LATER_ITEMS // res_chunk)
            sb_chunk = max(k for k in range(1, sb_chunk + 1) if (n_sb - 1) % k == 0)
            assert d == res_chunk

            def later_chunk(it, carry, gi=gi, d=d, sb_chunk=sb_chunk, later_item=later_item):
                attend(gi, d, [later_item(r, 1 + it * sb_chunk + i) for i in range(sb_chunk) for r in range(d)],
                       in_band)
                return carry

            lax.fori_loop(0, (n_sb - 1) // sb_chunk, later_chunk, 0)

    def merge(ch, carry):
        rows = pl.ds(pl.multiple_of(ch * DIL_MERGE_ROWS, DIL_MERGE_ROWS), DIL_MERGE_ROWS)
        for half in range(w // LANES):
            l = [l_scr[g, half, rows, :] for g in range(N_DIL_GROUPS)]
            mx = jnp.maximum(jnp.maximum(l[0], l[1]), l[2])
            e = [jnp.exp2(lg - mx) for lg in l]
            num = e[0] * o_scr[0, half, rows, :] + e[1] * o_scr[1, half, rows, :] + e[2] * o_scr[2, half, rows, :]
            ya_ref[rows, half * LANES:(half + 1) * LANES] = (num / (e[0] + e[1] + e[2])).astype(ya_ref.dtype)
        return carry

    lax.fori_loop(0, tile // DIL_MERGE_ROWS, merge, 0)


def _memkv_kernel(mem_ref, g_ref, w_ref, gk_ref, km_ref, vm_ref):
    mn = _row_rmsnorm(mem_ref[...], g_ref[...]).astype(BF16)
    kv = _dot(mn, w_ref[...])
    gk = gk_ref[...]
    for hh in range(MEM_HEADS):
        cols = slice(hh * MEM_HEAD_DIM, (hh + 1) * MEM_HEAD_DIM)
        km_ref[:, cols] = _row_rmsnorm(kv[:, cols], gk).astype(km_ref.dtype)
    vm_ref[...] = kv[:, MEM_WIDTH:].astype(vm_ref.dtype)


def _merge_kernel(x_ref, h_ref, ya_ref, yb_ref, qm_ref, km_ref, vm_ref,
                  wg_ref, bg_ref, wa_ref, wb_ref, wm_ref, wo_ref, out_ref, merged_ref):
    ya = ya_ref[...]
    ym = []
    for hh in range(MEM_HEADS):
        cols = slice(hh * MEM_HEAD_DIM, (hh + 1) * MEM_HEAD_DIM)
        s = _dot_nt(qm_ref[:, cols], km_ref[:, cols])
        p = jnp.exp(s - jnp.max(s, axis=-1, keepdims=True))
        den = jnp.sum(p, axis=-1, keepdims=True)
        ym.append((_dot(p.astype(BF16), vm_ref[:, cols]) / den).astype(BF16))
    ym = jnp.concatenate(ym, axis=1)
    yb = yb_ref[...]
    h = h_ref[...]
    cw = 2 * MXU_COLS
    for c in range(0, D_MODEL, cw):
        merged = None
        for k, (y, w_ref) in enumerate(((ya, wa_ref), (yb, wb_ref), (ym, wm_ref))):
            gcol = k * D_MODEL + c
            gate = jax.nn.sigmoid(_dot(h, wg_ref[:, gcol:gcol + cw]) + bg_ref[:, gcol:gcol + cw])
            term = gate * _dot(y, w_ref[:, c:c + cw])
            merged = term if merged is None else merged + term
        merged_ref[:, c:c + cw] = merged.astype(BF16)
    out_ref[...] = x_ref[...] + _dot(merged_ref[...], wo_ref[...])


def _mlp_kernel(x_ref, g_ref, wu_ref, wd_ref, out_ref, h2_ref):
    x = x_ref[...]
    h2_ref[...] = _row_rmsnorm(x, g_ref[...]).astype(BF16)
    cw = D_MODEL
    acc = x
    for c in range(0, D_FF, cw):
        u = jnp.maximum(_dot(h2_ref[...], wu_ref[:, c:c + cw]), 0.0)
        acc = acc + _dot((u * u).astype(BF16), wd_ref[c:c + cw, :])
    out_ref[...] = acc


def _full(shape):
    return pl.BlockSpec(shape, lambda *_: (0,) * len(shape), pipeline_mode=pl.Buffered(1))


def _params(*sem):
    return pltpu.CompilerParams(dimension_semantics=sem)


def _blockdiag(width, head_dim):
    r = jnp.arange(width) // head_dim
    return (r[:, None] == r[None, :]).astype(BF16)


def _decay_placement(n_hp):
    pk = np.zeros((LANES, n_hp * LANES), np.float32)
    onek = np.zeros((1, n_hp * LANES), np.float32)
    pq = np.zeros((n_hp * 2 * DECAY_ROWS, LANES), np.float32)
    oneq = np.zeros((n_hp * 2 * DECAY_ROWS, CUMSUM_BLOCK), np.float32)
    for hp in range(n_hp):
        for e in range(2):
            head = 2 * hp + e
            kcol = hp * LANES + DECAY_ROWS * e
            qrow = (2 * hp + e) * DECAY_ROWS
            for t in range(3):
                onek[0, kcol + t] = 1.0
                pk[t * FOX_HEADS + head, kcol + 3 + t] = -1.0
                pq[qrow + t, t * FOX_HEADS + head] = 1.0
                oneq[qrow + 3 + t, :] = 1.0
    return jnp.asarray(pk, BF16), jnp.asarray(onek), jnp.asarray(pq, BF16), jnp.asarray(oneq)


def _rope_tables(seq, width):
    half = ROT_DIM // 2
    inv_freq = ROPE_THETA ** (-jnp.arange(0, ROT_DIM, 2, dtype=F32) / ROT_DIM)
    ang = jnp.arange(seq, dtype=F32)[:, None] * inv_freq[None, :]
    cos, sin = jnp.cos(ang), jnp.sin(ang)
    ones = jnp.ones((seq, HEAD_DIM - ROT_DIM), F32)
    zeros = jnp.zeros((seq, HEAD_DIM - ROT_DIM), F32)
    zhalf = jnp.zeros((seq, half), F32)
    rc = jnp.concatenate([cos, cos, ones], axis=1)
    rs1 = jnp.concatenate([-sin, zhalf, zeros], axis=1)
    rs2 = jnp.concatenate([zhalf, sin, zeros], axis=1)
    reps = width // HEAD_DIM
    return tuple(jnp.tile(t, (1, reps)) for t in (rc, rs1, rs2))


def kernel(x, mem, g_mix, w_in, b_f, g_qA, g_kA, g_qB, g_kB, g_mem, w_mem_kv, g_qM, g_kM, w_gate, b_gate,
           w_br_a, w_br_b, w_br_m, w_out, g_mlp, w_up, w_down):
    B, T, D = x.shape
    assert D == D_MODEL and w_in.shape[0] == 1, "single-layer kernel"
    N = B * T
    mem_len = mem.shape[1]
    tm = ROW_TILE
    assert T % tm == 0 and T % FOX_Q_TILE == 0 and FOX_Q_TILE % FOX_K_TILE == 0
    x2d = x.reshape(N, D)

    wi = w_in[0]
    offs = [0]
    for wdt in (DIL_WIDTH, DIL_WIDTH, DIL_WIDTH, FOX_WIDTH, FOX_WIDTH, FOX_WIDTH, FOX_HEADS, MEM_WIDTH):
        offs.append(offs[-1] + wdt)
    w_fl = jnp.pad(wi[:, offs[6]:offs[7]], ((0, 0), (0, FL_PAD - FOX_HEADS)))
    w_all = jnp.concatenate([wi[:, :offs[6]], wi[:, offs[7]:], w_fl], axis=1).astype(BF16)
    assert offs[6] == INPROJ_COL["qm"] and w_all.shape[1] == INPROJ_COL["fl"] + FL_PAD
    n_cols = w_all.shape[1]
    bf_pad = jnp.pad(b_f[0], (0, FL_PAD - FOX_HEADS)).reshape(1, FL_PAD)
    cw = MXU_COLS
    q_scale = HEAD_DIM ** -0.5
    gqa = (jnp.tile(g_qA[0], cw // HEAD_DIM) * (q_scale * LOG2E)).reshape(1, cw)
    gka = jnp.tile(g_kA[0], cw // HEAD_DIM).reshape(1, cw)
    gqbt = jnp.broadcast_to((jnp.tile(g_qB[0], cw // HEAD_DIM) * (q_scale * LOG2E))[:, None], (cw, LANES))
    gkb = jnp.tile(g_kB[0], cw // HEAD_DIM).reshape(1, cw)
    gqm = (jnp.tile(g_qM[0], cw // MEM_HEAD_DIM) * MEM_HEAD_DIM ** -0.5).reshape(1, cw)
    rc, rs1, rs2 = _rope_tables(T, cw)
    bd64 = _blockdiag(cw, HEAD_DIM)

    row = lambda width: pl.BlockSpec((tm, width), lambda i: (i, 0))
    tm1 = INPROJ_ROW_TILE
    per_seq = T // tm1
    assert T % tm1 == 0
    row1 = lambda width: pl.BlockSpec((tm1, width), lambda i: (i, 0))
    rope_spec = pl.BlockSpec((tm1, cw), lambda i: (i % per_seq, 0))
    tspec = pl.BlockSpec((None, FOX_WIDTH, tm1), lambda i: (i // per_seq, 0, i % per_seq))
    tshape = jax.ShapeDtypeStruct((B, FOX_WIDTH, T), BF16)
    rshape = lambda width: jax.ShapeDtypeStruct((N, width), BF16)
    dils = [d for _, d in DIL_GROUPS]
    assert all(win // d == DIL_BACK and tm1 % d == 0 and (tm1 // d) % 16 == 0 for win, d in DIL_GROUPS)
    dspecs = [pl.BlockSpec((None, d, tm1 // d, DIL_OUT), lambda i: (i // per_seq, 0, i % per_seq, 0))
              for d in dils]
    dshapes = [jax.ShapeDtypeStruct((B, d, T // d, DIL_OUT), BF16) for d in dils]
    outs = pl.pallas_call(
        _inproj_kernel,
        grid=(N // tm1,),
        in_specs=[row1(D), _full((1, D)), _full((D, n_cols)), _full((cw, cw)),
                  _full((1, cw)), _full((1, cw)), _full((cw, LANES)), _full((1, cw)), _full((1, cw)),
                  _full((1, FL_PAD)), rope_spec, rope_spec, rope_spec],
        out_specs=[row1(D)] + dspecs * 3 + [tspec, row1(FOX_WIDTH), tspec, row1(MEM_WIDTH), row1(FL_PAD)],
        out_shape=[rshape(D)] + dshapes * 3 + [tshape, rshape(FOX_WIDTH), tshape, rshape(MEM_WIDTH),
                                               jax.ShapeDtypeStruct((N, FL_PAD), F32)],
        scratch_shapes=[pltpu.VMEM((cw // LANES, tm1, LANES), F32)],
        compiler_params=_params("parallel"),
        name="inproj",
    )(x2d, g_mix, w_all, bd64, gqa, gka, gqbt, gkb, gqm, bf_pad, rc, rs1, rs2)
    h, qa_g, ka_g, va_g = outs[0], outs[1:4], outs[4:7], outs[7:10]
    qbt, kb, vbt, qm, logf = outs[10:]

    n_hp = FOX_HEADS // 2
    tri = (jnp.arange(CUMSUM_BLOCK)[:, None] >= jnp.arange(CUMSUM_BLOCK)[None, :]).astype(BF16)
    pk, onek, pq, oneq = _decay_placement(n_hp)
    cqt, ck_ext = pl.pallas_call(
        _cumsum_kernel,
        grid=(B,),
        in_specs=[pl.BlockSpec((T, FL_PAD), lambda b: (b, 0)), _full((CUMSUM_BLOCK, CUMSUM_BLOCK)),
                  _full(pk.shape), _full(onek.shape), _full(pq.shape), _full(oneq.shape)],
        out_specs=[pl.BlockSpec((None, n_hp, 2, DECAY_ROWS, T), lambda b: (b, 0, 0, 0, 0)),
                   pl.BlockSpec((None, n_hp, T, LANES), lambda b: (b, 0, 0, 0))],
        out_shape=[jax.ShapeDtypeStruct((B, n_hp, 2, DECAY_ROWS, T), BF16),
                   jax.ShapeDtypeStruct((B, n_hp, T, LANES), BF16)],
        compiler_params=_params("parallel"),
        name="decay_cumsum",
    )(logf, tri, pk, onek, pq, oneq)

    tq = FOX_Q_TILE
    nq = T // tq
    n_units = 2 * tq // FOX_COLS
    yb = pl.pallas_call(
        _fox_kernel,
        grid=(B, n_hp, nq),
        in_specs=[pl.BlockSpec((None, LANES, tq), lambda b, hp, i: (b, hp, i)),
                  pl.BlockSpec((None, None, 2, DECAY_ROWS, tq), lambda b, hp, i: (b, hp, 0, 0, i)),
                  pl.BlockSpec((T, LANES), lambda b, hp, i: (b, hp)),
                  pl.BlockSpec((None, None, T, LANES), lambda b, hp, i: (b, hp, 0, 0)),
                  pl.BlockSpec((None, LANES, T), lambda b, hp, i: (b, hp, 0))],
        out_specs=pl.BlockSpec((tq, LANES), lambda b, hp, i: (b * nq + i, hp)),
        out_shape=jax.ShapeDtypeStruct((N, FOX_WIDTH), BF16),
        scratch_shapes=[pltpu.VMEM((2, MXU_COLS, tq), BF16),
                        pltpu.VMEM((2, n_units, FOX_K_TILE, FOX_COLS), F32),
                        pltpu.VMEM((2, n_units, 1, FOX_COLS), F32),
                        pltpu.VMEM((n_units, 1, FOX_COLS), F32),
                        pltpu.VMEM((n_units, HEAD_DIM + FOX_DEN_ROWS, FOX_COLS), F32)],
        compiler_params=_params("parallel", "parallel", "arbitrary"),
        name="fox_attention",
    )(qbt, cqt, kb, ck_ext, vbt)

    tile = DIL_TILE
    assert T % tile == 0 and all((tile // d) % DIL_BACK == 0 for d in dils)
    lane_head = jnp.arange(DIL_OUT) // HEAD_DIM
    head_mask = jnp.broadcast_to((lane_head[None, :] == jnp.arange(DIL_HEADS)[:, None])[:, None, :],
                                 (DIL_HEADS, DIL_BACK, DIL_OUT)).astype(BF16)
    dil_specs, dil_args = [], []
    for gi, d in enumerate(dils):
        per_tile = tile // d // DIL_BACK
        cur = pl.BlockSpec((None, d, tile // d, DIL_OUT), lambda b, j: (b, 0, j, 0))
        prev = pl.BlockSpec((None, d, DIL_BACK, DIL_OUT),
                            lambda b, j, per_tile=per_tile: (b, 0, jnp.maximum(j * per_tile - 1, 0), 0))
        dil_specs += [cur, cur, prev, cur, prev]
        dil_args += [qa_g[gi], ka_g[gi], ka_g[gi], va_g[gi], va_g[gi]]
    ya = pl.pallas_call(
        _dilated_kernel,
        grid=(B, T // tile),
        in_specs=[_full((DIL_HEADS, DIL_BACK, DIL_OUT))] + dil_specs,
        out_specs=pl.BlockSpec((tile, DIL_OUT), lambda b, j: (b * (T // tile) + j, 0)),
        out_shape=jax.ShapeDtypeStruct((N, DIL_OUT), BF16),
        scratch_shapes=[pltpu.VMEM((N_DIL_GROUPS, DIL_OUT // LANES, tile, LANES), F32)] * 2,
        compiler_params=_params("parallel", "arbitrary"),
        name="dilated_attention",
    )(head_mask, *dil_args)

    gkm = g_kM[0].reshape(1, MEM_HEAD_DIM)
    km, vm = pl.pallas_call(
        _memkv_kernel,
        grid=(B,),
        in_specs=[pl.BlockSpec((None, mem_len, D), lambda b: (b, 0, 0)), _full((1, D)),
                  _full((D, 2 * MEM_WIDTH)), _full((1, MEM_HEAD_DIM))],
        out_specs=[pl.BlockSpec((None, mem_len, MEM_WIDTH), lambda b: (b, 0, 0))] * 2,
        out_shape=[jax.ShapeDtypeStruct((B, mem_len, MEM_WIDTH), BF16)] * 2,
        compiler_params=_params("parallel"),
        name="mem_kv",
    )(mem, g_mem, w_mem_kv[0].astype(BF16), gkm)

    mem_spec = pl.BlockSpec((None, mem_len, MEM_WIDTH), lambda i: (i // (T // tm), 0, 0))
    x_mid = pl.pallas_call(
        _merge_kernel,
        grid=(N // tm,),
        in_specs=[row(D), row(D), row(DIL_OUT), row(FOX_WIDTH), row(MEM_WIDTH), mem_spec, mem_spec,
                  _full((D, 3 * D)), _full((1, 3 * D)), _full((DIL_OUT, D)), _full((FOX_WIDTH, D)),
                  _full((MEM_WIDTH, D)), _full((D, D))],
        out_specs=row(D),
        out_shape=jax.ShapeDtypeStruct((N, D), F32),
        scratch_shapes=[pltpu.VMEM((tm, D), BF16)],
        compiler_params=_params("parallel"),
        name="merge_outproj",
    )(x2d, h, ya, yb, qm, km, vm, w_gate[0].astype(BF16), b_gate, w_br_a[0].astype(BF16),
      w_br_b[0].astype(BF16), w_br_m[0].astype(BF16), w_out[0].astype(BF16))

    out = pl.pallas_call(
        _mlp_kernel,
        grid=(N // tm,),
        in_specs=[row(D), _full((1, D)), _full((D, D_FF)), _full((D_FF, D))],
        out_specs=row(D),
        out_shape=jax.ShapeDtypeStruct((N, D), F32),
        scratch_shapes=[pltpu.VMEM((tm, D), BF16)],
        compiler_params=_params("parallel"),
        name="mlp",
    )(x_mid, g_mlp, w_up[0].astype(BF16), w_down[0].astype(BF16))
    return out.reshape(B, T, D)
```

```python
import functools

import jax
import jax.numpy as jnp
import numpy as np
from jax import lax
from jax.experimental import pallas as pl
from jax.experimental.pallas import tpu as pltpu

D_MODEL = 1024
HEAD_DIM = 64
DIL_GROUPS = ((128, 1), (512, 4), (2048, 16))
N_DIL_GROUPS = 3
DIL_HEADS = 4
DIL_WIDTH = N_DIL_GROUPS * DIL_HEADS * HEAD_DIM
DIL_OUT = DIL_HEADS * HEAD_DIM
DIL_BACK = 128
FOX_HEADS = 8
FOX_WIDTH = FOX_HEADS * HEAD_DIM
MEM_HEADS = 4
MEM_HEAD_DIM = 128
MEM_WIDTH = MEM_HEADS * MEM_HEAD_DIM
ROT_DIM = HEAD_DIM // 4
ROPE_THETA = 500000.0
D_FF = 4 * D_MODEL
EPS = 1e-6

LANES = 128
MXU_COLS = 256
FL_PAD = LANES

ROW_TILE = 1024
INPROJ_ROW_TILE = 1024
INPROJ_COL = dict(qa=0, ka=DIL_WIDTH, va=2 * DIL_WIDTH, qb=3 * DIL_WIDTH, kb=3 * DIL_WIDTH + FOX_WIDTH,
                  vb=3 * DIL_WIDTH + 2 * FOX_WIDTH, qm=3 * DIL_WIDTH + 3 * FOX_WIDTH,
                  fl=3 * DIL_WIDTH + 3 * FOX_WIDTH + MEM_WIDTH)
INPROJ_ORDER = (("qa", 0), ("ka", 0), ("qa", 1), ("ka", 1), ("qa", 2), ("ka", 2), ("qb", 0), ("kb", 0),
                ("qb", 1), ("kb", 1), ("qm", 0), ("qm", 1), ("va", 2), ("va", 1), ("vb", 0), ("vb", 1), ("va", 0))
FOX_Q_TILE = 2048
FOX_K_TILE = 256
FOX_COLS = 512
FOX_DEN_ROWS = 16
FOX_LOOKAHEAD = 1
FOX_BLOCKS_PER_TRIP = 4
DECAY_ROWS = 16
CUMSUM_BLOCK = 256
DIL_TILE = 2048
DIL_MERGE_ROWS = 256
DIL_CHAINS = 16
DIL_LATER_ITEMS = 15

LOG2E = 1.4426950408889634
MASKED = -jnp.inf
BF16 = jnp.bfloat16
F32 = jnp.float32
NT_DIMS = (((1,), (1,)), ((), ()))


def _dot(a, b):
    return jnp.dot(a, b, preferred_element_type=F32)


def _dot_nt(a, b):
    return lax.dot_general(a, b, NT_DIMS, preferred_element_type=F32)


def _row_rmsnorm(x, g):
    ms = jnp.mean(x * x, axis=-1, keepdims=True)
    return x * lax.rsqrt(ms + EPS) * g


def _head_rmsnorm(y, blockdiag, gain, head_dim):
    y2 = (y * y).astype(BF16)
    half = y.shape[0] // 2
    ss = jnp.concatenate([_dot(y2[:half], blockdiag), _dot(y2[half:], blockdiag)], axis=0)
    return y * lax.rsqrt(ss * (1.0 / head_dim) + EPS) * gain


def _split3(c):
    hi = c.astype(BF16)
    r1 = c - hi.astype(F32)
    mid = r1.astype(BF16)
    lo = (r1 - mid.astype(F32)).astype(BF16)
    return hi, mid, lo


def _inproj_kernel(x_ref, gmix_ref, w_ref, bd64_ref, gqa_ref, gka_ref, gqbt_ref, gkb_ref,
                   gqm_ref, bf_ref, rc_ref, rs1_ref, rs2_ref,
                   h_ref, qa0_ref, qa1_ref, qa2_ref, ka0_ref, ka1_ref, ka2_ref, va0_ref, va1_ref, va2_ref,
                   qbt_ref, kb_ref, vbt_ref, qm_ref, lf_ref, perm_ref):
    h = _row_rmsnorm(x_ref[...], gmix_ref[...]).astype(BF16)
    h_ref[...] = h
    bd64 = bd64_ref[...]
    rc = rc_ref[...]
    rs1 = rs1_ref[...]
    rs2 = rs2_ref[...]
    cw = MXU_COLS
    tm = x_ref.shape[0]

    def proj(col, width=cw):
        return _dot(h, w_ref[:, col:col + width])

    def rope(y):
        down = pltpu.roll(y, cw - ROT_DIM // 2, 1)
        up = pltpu.roll(y, ROT_DIM // 2, 1)
        return jnp.concatenate(
            [y[:, c:c + LANES] * rc + down[:, c:c + LANES] * rs1 + up[:, c:c + LANES] * rs2
             for c in range(0, cw, LANES)], axis=1)

    def store_rows(out_ref, c, y):
        out_ref[:, c:c + cw] = y.astype(out_ref.dtype)

    def store_transposed(out_ref, c, y):
        out_ref[c:c + cw, :] = y.T.astype(out_ref.dtype)

    def store_transposed_normed(out_ref, c, y):
        yt = y.T
        gain_col = gqbt_ref[...]
        parts = []
        for r0 in range(0, cw, HEAD_DIM):
            blk = yt[r0:r0 + HEAD_DIM]
            ss = jnp.sum(blk * blk, axis=0, keepdims=True)
            gain = jnp.concatenate([gain_col[r0:r0 + HEAD_DIM]] * (tm // LANES), axis=1)
            parts.append(blk * lax.rsqrt(ss * (1.0 / HEAD_DIM) + EPS) * gain)
        out_ref[c:c + cw, :] = jnp.concatenate(parts, axis=0).astype(out_ref.dtype)

    def store_by_residue(out_ref, y):
        d = out_ref.shape[0]
        if d == 1:
            out_ref[0] = y.astype(out_ref.dtype)
            return
        for half in range(cw // LANES):
            perm_ref[half] = y[:, half * LANES:(half + 1) * LANES]
        for r in range(d):
            out_ref[r] = jnp.concatenate(
                [perm_ref[half, pl.ds(r, tm // d, stride=d), :] for half in range(cw // LANES)],
                axis=1).astype(out_ref.dtype)

    rows_of = lambda ref: [(functools.partial(store_rows, ref, c)) for c in range(0, ref.shape[1], cw)]
    cols_of = lambda ref: [(functools.partial(store_transposed, ref, c)) for c in range(0, ref.shape[0], cw)]
    residues_of = lambda refs: [functools.partial(store_by_residue, ref) for ref in refs]
    segments = {
        "qa": (residues_of((qa0_ref, qa1_ref, qa2_ref)), gqa_ref, bd64, HEAD_DIM, True),
        "ka": (residues_of((ka0_ref, ka1_ref, ka2_ref)), gka_ref, bd64, HEAD_DIM, True),
        "va": (residues_of((va0_ref, va1_ref, va2_ref)), None, None, None, False),
        "qb": ([functools.partial(store_transposed_normed, qbt_ref, c) for c in range(0, qbt_ref.shape[0], cw)],
               None, None, None, False),
        "kb": (rows_of(kb_ref), gkb_ref, bd64, HEAD_DIM, False),
        "vb": (cols_of(vbt_ref), None, None, None, False),
        "qm": (rows_of(qm_ref), gqm_ref, None, MEM_HEAD_DIM, False)}
    chunks = [(INPROJ_COL[name] + piece * cw, segments[name][0][piece]) + segments[name][1:]
              for name, piece in INPROJ_ORDER]
    z = proj(INPROJ_COL["fl"], FL_PAD) + bf_ref[...]
    y_next = proj(chunks[0][0])
    lf_ref[...] = jnp.minimum(z, 0.0) - jnp.log1p(jnp.exp(-jnp.abs(z)))
    for idx, (_, store, gain_ref, bd, hd, rot) in enumerate(chunks):
        y = y_next
        y_next = proj(chunks[idx + 1][0]) if idx + 1 < len(chunks) else None
        if gain_ref is not None and hd == LANES:
            gain = gain_ref[...]
            y = jnp.concatenate([_row_rmsnorm(y[:, c:c + hd], gain[:, c:c + hd]) for c in range(0, cw, hd)], axis=1)
        elif gain_ref is not None:
            y = _head_rmsnorm(y, bd, gain_ref[...], hd)
        if rot:
            y = rope(y)
        store(y)


def _cumsum_kernel(lf_ref, tri_ref, pk_ref, onek_ref, pq_ref, oneq_ref, cqt_ref, ck_ref):
    tri = tri_ref[...]
    n_blocks = lf_ref.shape[0] // CUMSUM_BLOCK
    n_hp = ck_ref.shape[0]
    lane = lax.broadcasted_iota(jnp.int32, (CUMSUM_BLOCK, LANES), 1)
    carry = jnp.zeros((1, LANES), F32)
    for blk in range(n_blocks):
        rows = slice(blk * CUMSUM_BLOCK, (blk + 1) * CUMSUM_BLOCK)
        local = _dot(tri, jnp.concatenate(_split3(lf_ref[rows, :]), axis=1))
        c = local[:, :LANES] + local[:, LANES:2 * LANES] + local[:, 2 * LANES:] + carry
        carry = c[CUMSUM_BLOCK - 1:CUMSUM_BLOCK, :]
        c = c * LOG2E
        hi, mid, lo = (t.astype(F32) for t in _split3(c))
        pieces = jnp.where(lane < FOX_HEADS, hi,
                           jnp.where(lane < 2 * FOX_HEADS, pltpu.roll(mid, FOX_HEADS, 1),
                                     pltpu.roll(lo, 2 * FOX_HEADS, 1))).astype(BF16)
        ke = _dot(pieces, pk_ref[...]) + onek_ref[...]
        qe = _dot_nt(pq_ref[...], pieces) + oneq_ref[...]
        for hp in range(n_hp):
            ck_ref[hp, rows, :] = ke[:, hp * LANES:(hp + 1) * LANES].astype(BF16)
            for e in range(2):
                r0 = (2 * hp + e) * DECAY_ROWS
                cqt_ref[hp, e, :, rows] = qe[r0:r0 + DECAY_ROWS, :].astype(BF16)


def _fox_kernel(qt_ref, cqt_ref, k_ref, ck_ref, vt_ref, o_ref, qa_ref, st_ref, bm_ref, m_ref, acc_ref):
    blk = pl.program_id(2)
    tq = FOX_Q_TILE
    tk = FOX_K_TILE
    hd = HEAD_DIM
    cw = FOX_COLS
    zq = jnp.zeros((hd, tq), BF16)
    zd = jnp.zeros((DECAY_ROWS, tq), BF16)
    zpad = jnp.zeros((MXU_COLS - 2 * hd - 2 * DECAY_ROWS, tq), BF16)
    for e in range(2):
        parts = [zq, zq, zd, zd, zpad]
        parts[e] = qt_ref[e * hd:(e + 1) * hd, :]
        parts[2 + e] = cqt_ref[e]
        qa_ref[e] = jnp.concatenate(parts, axis=0)
    units = [(e, c0) for e in range(2) for c0 in range(0, tq, cw)]

    def key_block(key_start):
        rows = pl.ds(pl.multiple_of(key_start, tk), tk)
        return jnp.concatenate([k_ref[rows, :], ck_ref[rows, :]], axis=1)

    def value_block(key_start):
        return vt_ref[:, pl.ds(pl.multiple_of(key_start, tk), tk)]

    def first_visible(u, key_off):
        c0 = units[u][1]
        return cw // 2 if key_off is not None and key_off >= c0 + cw // 2 else 0

    def issue_scores(buf, u, k_aug, key_off=None):
        e, c0 = units[u]
        lo = first_visible(u, key_off)
        st = _dot(k_aug, qa_ref[e, :, c0 + lo:c0 + cw])
        st_ref[buf, u, :, lo:] = st
        bm_ref[buf, u, :, lo:] = jnp.max(st, axis=0, keepdims=True)

    def update(buf, u, vt, key_off):
        e, c0 = units[u]
        lo = first_visible(u, key_off)
        st = st_ref[buf, u, :, lo:]
        if key_off is not None and key_off + tk - 1 > c0 + lo:
            assert key_off == c0 + lo
            masked = jnp.where(causal, st[:, :tk], MASKED)
            st = masked if st.shape[1] == tk else jnp.concatenate([masked, st[:, tk:]], axis=1)
            block_max = jnp.max(st, axis=0, keepdims=True)
        else:
            block_max = bm_ref[buf, u, :, lo:]
        m = m_ref[u, :, lo:]
        m_new = jnp.maximum(m, block_max)
        alpha = jnp.exp2(m - m_new)
        p = jnp.exp2(st - m_new).astype(BF16)
        vt_aug = jnp.concatenate([vt[e * hd:(e + 1) * hd, :], ones_rows], axis=0)
        acc_ref[u, :, lo:] = alpha * acc_ref[u, :, lo:] + _dot(vt_aug, p)
        m_ref[u, :, lo:] = m_new

    ones_rows = jnp.ones((FOX_DEN_ROWS, tk), BF16)
    causal = lax.broadcasted_iota(jnp.int32, (tk, tk), 0) <= lax.broadcasted_iota(jnp.int32, (tk, tk), 1)
    m_ref[...] = jnp.full(m_ref.shape, MASKED, F32)
    acc_ref[...] = jnp.zeros(acc_ref.shape, F32)
    k_first = key_block(0)
    for u in range(len(units)):
        issue_scores(0, u, k_first)

    per_trip = FOX_BLOCKS_PER_TRIP
    assert per_trip % 2 == 0 and (tq // tk) % per_trip == 0

    def block_group(jg, carry):
        for step in range(per_trip):
            j = per_trip * jg + step
            k_next = key_block((j + 1) * tk)
            vt = value_block(j * tk)
            for u in range(min(FOX_LOOKAHEAD, len(units))):
                issue_scores(1 - step % 2, u, k_next)
            for u in range(len(units)):
                if u + FOX_LOOKAHEAD < len(units):
                    issue_scores(1 - step % 2, u + FOX_LOOKAHEAD, k_next)
                update(step % 2, u, vt, None)
        return carry

    lax.fori_loop(0, blk * (tq // tk // per_trip), block_group, 0)
    for jj in range(tq // tk):
        key_off = jj * tk
        active = [u for u, (e, c0) in enumerate(units) if c0 + cw > key_off]
        nxt_off = key_off + tk
        nxt_active = [u for u, (e, c0) in enumerate(units) if c0 + cw > nxt_off] if nxt_off < tq else []
        if nxt_active:
            k_next = key_block(blk * tq + nxt_off)
        vt = value_block(blk * tq + key_off)
        for u in active:
            if u in nxt_active:
                issue_scores((jj + 1) % 2, u, k_next, nxt_off)
            update(jj % 2, u, vt, key_off)
    per_head = []
    for e in range(2):
        cols = [acc_ref[u, :hd, :] * (1.0 / acc_ref[u, hd:hd + 1, :]) for u, unit in enumerate(units) if unit[0] == e]
        per_head.append(jnp.concatenate(cols, axis=1))
    ot = jnp.concatenate(per_head, axis=0)
    o_ref[...] = ot.T.astype(o_ref.dtype)


def _dilated_kernel(hm_ref, q0_ref, k0_ref, kp0_ref, v0_ref, vp0_ref, q1_ref, k1_ref, kp1_ref, v1_ref, vp1_ref,
                    q2_ref, k2_ref, kp2_ref, v2_ref, vp2_ref, ya_ref, o_scr, l_scr):
    jt = pl.program_id(1)
    c = DIL_BACK
    w = DIL_OUT
    nh = DIL_HEADS
    tile = ya_ref.shape[0]
    qi = lax.broadcasted_iota(jnp.int32, (nh * c, 2 * c), 0) & (c - 1)
    kj = lax.broadcasted_iota(jnp.int32, (nh * c, 2 * c), 1)
    dist = qi + c - kj
    in_band = (dist >= 0) & (dist <= DIL_BACK)
    in_band_first = in_band & (kj >= jnp.where(jt > 0, 0, c))
    lane = lax.broadcasted_iota(jnp.int32, (c, w), 1)
    in_head = [(lane >= hh * HEAD_DIM) & (lane < (hh + 1) * HEAD_DIM) for hh in range(nh)]

    def attend(gi, d, items, mask):
        def scores(item):
            q, kk, _, _ = item
            return _dot_nt(jnp.concatenate([q * hm_ref[hh] for hh in range(nh)], axis=0), kk)

        s_next = scores(items[0])
        for idx, (_, _, vv, tok0) in enumerate(items):
            s = s_next
            s_next = scores(items[idx + 1]) if idx + 1 < len(items) else None
            s = jnp.where(mask, s, MASKED)
            m = jnp.max(s, axis=-1, keepdims=True)
            p = jnp.exp2(s - m)
            den = jnp.sum(p, axis=-1, keepdims=True)
            o4 = _dot(p.astype(BF16), vv) * (1.0 / den)
            lse4 = m + jnp.log2(den)
            o = jnp.zeros((c, w), F32)
            lse = jnp.zeros((c, w), F32)
            for hh in range(nh):
                rows = slice(hh * c, (hh + 1) * c)
                o = jnp.where(in_head[hh], o4[rows], o)
                lse = jnp.where(in_head[hh], lse4[rows], lse)
            for half in range(w // LANES):
                lanes = slice(half * LANES, (half + 1) * LANES)
                o_scr[gi, half, pl.ds(tok0, c, stride=d), :] = o[:, lanes]
                l_scr[gi, half, pl.ds(tok0, c, stride=d), :] = lse[:, lanes]

    groups = ((q0_ref, k0_ref, kp0_ref, v0_ref, vp0_ref), (q1_ref, k1_ref, kp1_ref, v1_ref, vp1_ref),
              (q2_ref, k2_ref, kp2_ref, v2_ref, vp2_ref))
    for gi, (q_ref, k_ref, kp_ref, v_ref, vp_ref) in enumerate(groups):
        d = q_ref.shape[0]
        n_sb = q_ref.shape[1] // c
        res_chunk = min(d, DIL_CHAINS)
        assert d % res_chunk == 0

        def first_item(r, q_ref=q_ref, k_ref=k_ref, kp_ref=kp_ref, v_ref=v_ref, vp_ref=vp_ref):
            kk = jnp.concatenate([kp_ref[r], k_ref[r, 0:c]], axis=0)
            vv = jnp.concatenate([vp_ref[r], v_ref[r, 0:c]], axis=0)
            return q_ref[r, 0:c], kk, vv, r

        def later_item(r, sb, d=d, q_ref=q_ref, k_ref=k_ref, v_ref=v_ref):
            start = pl.multiple_of(sb * c, c)
            window = pl.ds(start - c, 2 * c)
            return q_ref[r, pl.ds(start, c)], k_ref[r, window], v_ref[r, window], sb * (c * d) + r

        def first_chunk(it, carry, gi=gi, d=d, res_chunk=res_chunk, first_item=first_item):
            attend(gi, d, [first_item(it * res_chunk + i) for i in range(res_chunk)], in_band_first)
            return carry

        if d == res_chunk:
            first_chunk(0, 0)
        else:
            lax.fori_loop(0, d // res_chunk, first_chunk, 0)
        if n_sb > 1:
            sb_chunk = max(1, DIL_LATER_ITEMS // res_chunk)
            sb_chunk = max(k for k in range(1, sb_chunk + 1) if (n_sb - 1) % k == 0)
            assert d == res_chunk

            def later_chunk(it, carry, gi=gi, d=d, sb_chunk=sb_chunk, later_item=later_item):
                attend(gi, d, [later_item(r, 1 + it * sb_chunk + i) for i in range(sb_chunk) for r in range(d)],
                       in_band)
                return carry

            lax.fori_loop(0, (n_sb - 1) // sb_chunk, later_chunk, 0)

    def merge(ch, carry):
        rows = pl.ds(pl.multiple_of(ch * DIL_MERGE_ROWS, DIL_MERGE_ROWS), DIL_MERGE_ROWS)
        for half in range(w // LANES):
            l = [l_scr[g, half, rows, :] for g in range(N_DIL_GROUPS)]
            mx = jnp.maximum(jnp.maximum(l[0], l[1]), l[2])
            e = [jnp.exp2(lg - mx) for lg in l]
            num = e[0] * o_scr[0, half, rows, :] + e[1] * o_scr[1, half, rows, :] + e[2] * o_scr[2, half, rows, :]
            ya_ref[rows, half * LANES:(half + 1) * LANES] = (num / (e[0] + e[1] + e[2])).astype(ya_ref.dtype)
        return carry

    lax.fori_loop(0, tile // DIL_MERGE_ROWS, merge, 0)


def _memkv_kernel(mem_ref, g_ref, w_ref, gk_ref, km_ref, vm_ref):
    mn = _row_rmsnorm(mem_ref[...], g_ref[...]).astype(BF16)
    kv = _dot(mn, w_ref[...])
    gk = gk_ref[...]
    for hh in range(MEM_HEADS):
        cols = slice(hh * MEM_HEAD_DIM, (hh + 1) * MEM_HEAD_DIM)
        km_ref[:, cols] = _row_rmsnorm(kv[:, cols], gk).astype(km_ref.dtype)
    vm_ref[...] = kv[:, MEM_WIDTH:].astype(vm_ref.dtype)


def _merge_kernel(x_ref, h_ref, ya_ref, yb_ref, qm_ref, km_ref, vm_ref,
                  wg_ref, bg_ref, wa_ref, wb_ref, wm_ref, wo_ref, out_ref, merged_ref):
    ya = ya_ref[...]
    ym = []
    for hh in range(MEM_HEADS):
        cols = slice(hh * MEM_HEAD_DIM, (hh + 1) * MEM_HEAD_DIM)
        s = _dot_nt(qm_ref[:, cols], km_ref[:, cols])
        p = jnp.exp(s - jnp.max(s, axis=-1, keepdims=True))
        den = jnp.sum(p, axis=-1, keepdims=True)
        ym.append((_dot(p.astype(BF16), vm_ref[:, cols]) / den).astype(BF16))
    ym = jnp.concatenate(ym, axis=1)
    yb = yb_ref[...]
    h = h_ref[...]
    cw = 2 * MXU_COLS
    for c in range(0, D_MODEL, cw):
        merged = None
        for k, (y, w_ref) in enumerate(((ya, wa_ref), (yb, wb_ref), (ym, wm_ref))):
            gcol = k * D_MODEL + c
            gate = jax.nn.sigmoid(_dot(h, wg_ref[:, gcol:gcol + cw]) + bg_ref[:, gcol:gcol + cw])
            term = gate * _dot(y, w_ref[:, c:c + cw])
            merged = term if merged is None else merged + term
        merged_ref[:, c:c + cw] = merged.astype(BF16)
    out_ref[...] = x_ref[...] + _dot(merged_ref[...], wo_ref[...])


def _mlp_kernel(x_ref, g_ref, wu_ref, wd_ref, out_ref, h2_ref):
    x = x_ref[...]
    h2_ref[...] = _row_rmsnorm(x, g_ref[...]).astype(BF16)
    cw = D_MODEL
    acc = x
    for c in range(0, D_FF, cw):
        u = jnp.maximum(_dot(h2_ref[...], wu_ref[:, c:c + cw]), 0.0)
        acc = acc + _dot((u * u).astype(BF16), wd_ref[c:c + cw, :])
    out_ref[...] = acc


def _full(shape):
    return pl.BlockSpec(shape, lambda *_: (0,) * len(shape), pipeline_mode=pl.Buffered(1))


def _params(*sem):
    return pltpu.CompilerParams(dimension_semantics=sem)


def _blockdiag(width, head_dim):
    r = jnp.arange(width) // head_dim
    return (r[:, None] == r[None, :]).astype(BF16)


def _decay_placement(n_hp):
    pk = np.zeros((LANES, n_hp * LANES), np.float32)
    onek = np.zeros((1, n_hp * LANES), np.float32)
    pq = np.zeros((n_hp * 2 * DECAY_ROWS, LANES), np.float32)
    oneq = np.zeros((n_hp * 2 * DECAY_ROWS, CUMSUM_BLOCK), np.float32)
    for hp in range(n_hp):
        for e in range(2):
            head = 2 * hp + e
            kcol = hp * LANES + DECAY_ROWS * e
            qrow = (2 * hp + e) * DECAY_ROWS
            for t in range(3):
                onek[0, kcol + t] = 1.0
                pk[t * FOX_HEADS + head, kcol + 3 + t] = -1.0
                pq[qrow + t, t * FOX_HEADS + head] = 1.0
                oneq[qrow + 3 + t, :] = 1.0
    return jnp.asarray(pk, BF16), jnp.asarray(onek), jnp.asarray(pq, BF16), jnp.asarray(oneq)


def _rope_tables(seq, width):
    half = ROT_DIM // 2
    inv_freq = ROPE_THETA ** (-jnp.arange(0, ROT_DIM, 2, dtype=F32) / ROT_DIM)
    ang = jnp.arange(seq, dtype=F32)[:, None] * inv_freq[None, :]
    cos, sin = jnp.cos(ang), jnp.sin(ang)
    ones = jnp.ones((seq, HEAD_DIM - ROT_DIM), F32)
    zeros = jnp.zeros((seq, HEAD_DIM - ROT_DIM), F32)
    zhalf = jnp.zeros((seq, half), F32)
    rc = jnp.concatenate([cos, cos, ones], axis=1)
    rs1 = jnp.concatenate([-sin, zhalf, zeros], axis=1)
    rs2 = jnp.concatenate([zhalf, sin, zeros], axis=1)
    reps = width // HEAD_DIM
    return tuple(jnp.tile(t, (1, reps)) for t in (rc, rs1, rs2))


def kernel(x, mem, g_mix, w_in, b_f, g_qA, g_kA, g_qB, g_kB, g_mem, w_mem_kv, g_qM, g_kM, w_gate, b_gate,
           w_br_a, w_br_b, w_br_m, w_out, g_mlp, w_up, w_down):
    B, T, D = x.shape
    assert D == D_MODEL and w_in.shape[0] == 1, "single-layer kernel"
    N = B * T
    mem_len = mem.shape[1]
    tm = ROW_TILE
    assert T % tm == 0 and T % FOX_Q_TILE == 0 and FOX_Q_TILE % FOX_K_TILE == 0
    x2d = x.reshape(N, D)

    wi = w_in[0]
    offs = [0]
    for wdt in (DIL_WIDTH, DIL_WIDTH, DIL_WIDTH, FOX_WIDTH, FOX_WIDTH, FOX_WIDTH, FOX_HEADS, MEM_WIDTH):
        offs.append(offs[-1] + wdt)
    w_fl = jnp.pad(wi[:, offs[6]:offs[7]], ((0, 0), (0, FL_PAD - FOX_HEADS)))
    w_all = jnp.concatenate([wi[:, :offs[6]], wi[:, offs[7]:], w_fl], axis=1).astype(BF16)
    assert offs[6] == INPROJ_COL["qm"] and w_all.shape[1] == INPROJ_COL["fl"] + FL_PAD
    n_cols = w_all.shape[1]
    bf_pad = jnp.pad(b_f[0], (0, FL_PAD - FOX_HEADS)).reshape(1, FL_PAD)
    cw = MXU_COLS
    q_scale = HEAD_DIM ** -0.5
    gqa = (jnp.tile(g_qA[0], cw // HEAD_DIM) * (q_scale * LOG2E)).reshape(1, cw)
    gka = jnp.tile(g_kA[0], cw // HEAD_DIM).reshape(1, cw)
    gqbt = jnp.broadcast_to((jnp.tile(g_qB[0], cw // HEAD_DIM) * (q_scale * LOG2E))[:, None], (cw, LANES))
    gkb = jnp.tile(g_kB[0], cw // HEAD_DIM).reshape(1, cw)
    gqm = (jnp.tile(g_qM[0], cw // MEM_HEAD_DIM) * MEM_HEAD_DIM ** -0.5).reshape(1, cw)
    rc, rs1, rs2 = _rope_tables(T, LANES)
    bd64 = _blockdiag(cw, HEAD_DIM)

    row = lambda width: pl.BlockSpec((tm, width), lambda i: (i, 0))
    tm1 = INPROJ_ROW_TILE
    per_seq = T // tm1
    assert T % tm1 == 0
    row1 = lambda width: pl.BlockSpec((tm1, width), lambda i: (i, 0))
    rope_spec = pl.BlockSpec((tm1, LANES), lambda i: (i % per_seq, 0))
    tspec = pl.BlockSpec((None, FOX_WIDTH, tm1), lambda i: (i // per_seq, 0, i % per_seq))
    tshape = jax.ShapeDtypeStruct((B, FOX_WIDTH, T), BF16)
    rshape = lambda width: jax.ShapeDtypeStruct((N, width), BF16)
    dils = [d for _, d in DIL_GROUPS]
    assert all(win // d == DIL_BACK and tm1 % d == 0 and (tm1 // d) % 16 == 0 for win, d in DIL_GROUPS)
    dspecs = [pl.BlockSpec((None, d, tm1 // d, DIL_OUT), lambda i: (i // per_seq, 0, i % per_seq, 0))
              for d in dils]
    dshapes = [jax.ShapeDtypeStruct((B, d, T // d, DIL_OUT), BF16) for d in dils]
    outs = pl.pallas_call(
        _inproj_kernel,
        grid=(N // tm1,),
        in_specs=[row1(D), _full((1, D)), _full((D, n_cols)), _full((cw, cw)),
                  _full((1, cw)), _full((1, cw)), _full((cw, LANES)), _full((1, cw)), _full((1, cw)),
                  _full((1, FL_PAD)), rope_spec, rope_spec, rope_spec],
        out_specs=[row1(D)] + dspecs * 3 + [tspec, row1(FOX_WIDTH), tspec, row1(MEM_WIDTH), row1(FL_PAD)],
        out_shape=[rshape(D)] + dshapes * 3 + [tshape, rshape(FOX_WIDTH), tshape, rshape(MEM_WIDTH),
                                               jax.ShapeDtypeStruct((N, FL_PAD), F32)],
        scratch_shapes=[pltpu.VMEM((cw // LANES, tm1, LANES), F32)],
        compiler_params=_params("parallel"),
        name="inproj",
    )(x2d, g_mix, w_all, bd64, gqa, gka, gqbt, gkb, gqm, bf_pad, rc, rs1, rs2)
    h, qa_g, ka_g, va_g = outs[0], outs[1:4], outs[4:7], outs[7:10]
    qbt, kb, vbt, qm, logf = outs[10:]

    n_hp = FOX_HEADS // 2
    tri = (jnp.arange(CUMSUM_BLOCK)[:, None] >= jnp.arange(CUMSUM_BLOCK)[None, :]).astype(BF16)
    pk, onek, pq, oneq = _decay_placement(n_hp)
    cqt, ck_ext = pl.pallas_call(
        _cumsum_kernel,
        grid=(B,),
        in_specs=[pl.BlockSpec((T, FL_PAD), lambda b: (b, 0)), _full((CUMSUM_BLOCK, CUMSUM_BLOCK)),
                  _full(pk.shape), _full(onek.shape), _full(pq.shape), _full(oneq.shape)],
        out_specs=[pl.BlockSpec((None, n_hp, 2, DECAY_ROWS, T), lambda b: (b, 0, 0, 0, 0)),
                   pl.BlockSpec((None, n_hp, T, LANES), lambda b: (b, 0, 0, 0))],
        out_shape=[jax.ShapeDtypeStruct((B, n_hp, 2, DECAY_ROWS, T), BF16),
                   jax.ShapeDtypeStruct((B, n_hp, T, LANES), BF16)],
        compiler_params=_params("parallel"),
        name="decay_cumsum",
    )(logf, tri, pk, onek, pq, oneq)

    tq = FOX_Q_TILE
    nq = T // tq
    n_units = 2 * tq // FOX_COLS
    yb = pl.pallas_call(
        _fox_kernel,
        grid=(B, n_hp, nq),
        in_specs=[pl.BlockSpec((None, LANES, tq), lambda b, hp, i: (b, hp, i)),
                  pl.BlockSpec((None, None, 2, DECAY_ROWS, tq), lambda b, hp, i: (b, hp, 0, 0, i)),
                  pl.BlockSpec((T, LANES), lambda b, hp, i: (b, hp)),
                  pl.BlockSpec((None, None, T, LANES), lambda b, hp, i: (b, hp, 0, 0)),
                  pl.BlockSpec((None, LANES, T), lambda b, hp, i: (b, hp, 0))],
        out_specs=pl.BlockSpec((tq, LANES), lambda b, hp, i: (b * nq + i, hp)),
        out_shape=jax.ShapeDtypeStruct((N, FOX_WIDTH), BF16),
        scratch_shapes=[pltpu.VMEM((2, MXU_COLS, tq), BF16),
                        pltpu.VMEM((2, n_units, FOX_K_TILE, FOX_COLS), F32),
                        pltpu.VMEM((2, n_units, 1, FOX_COLS), F32),
                        pltpu.VMEM((n_units, 1, FOX_COLS), F32),
                        pltpu.VMEM((n_units, HEAD_DIM + FOX_DEN_ROWS, FOX_COLS), F32)],
        compiler_params=_params("parallel", "parallel", "arbitrary"),
        name="fox_attention",
    )(qbt, cqt, kb, ck_ext, vbt)

    tile = DIL_TILE
    assert T % tile == 0 and all((tile // d) % DIL_BACK == 0 for d in dils)
    lane_head = jnp.arange(DIL_OUT) // HEAD_DIM
    head_mask = jnp.broadcast_to((lane_head[None, :] == jnp.arange(DIL_HEADS)[:, None])[:, None, :],
                                 (DIL_HEADS, DIL_BACK, DIL_OUT)).astype(BF16)
    dil_specs, dil_args = [], []
    for gi, d in enumerate(dils):
        per_tile = tile // d // DIL_BACK
        cur = pl.BlockSpec((None, d, tile // d, DIL_OUT), lambda b, j: (b, 0, j, 0))
        prev = pl.BlockSpec((None, d, DIL_BACK, DIL_OUT),
                            lambda b, j, per_tile=per_tile: (b, 0, jnp.maximum(j * per_tile - 1, 0), 0))
        dil_specs += [cur, cur, prev, cur, prev]
        dil_args += [qa_g[gi], ka_g[gi], ka_g[gi], va_g[gi], va_g[gi]]
    ya = pl.pallas_call(
        _dilated_kernel,
        grid=(B, T // tile),
        in_specs=[_full((DIL_HEADS, DIL_BACK, DIL_OUT))] + dil_specs,
        out_specs=pl.BlockSpec((tile, DIL_OUT), lambda b, j: (b * (T // tile) + j, 0)),
        out_shape=jax.ShapeDtypeStruct((N, DIL_OUT), BF16),
        scratch_shapes=[pltpu.VMEM((N_DIL_GROUPS, DIL_OUT // LANES, tile, LANES), F32)] * 2,
        compiler_params=_params("parallel", "arbitrary"),
        name="dilated_attention",
    )(head_mask, *dil_args)

    gkm = g_kM[0].reshape(1, MEM_HEAD_DIM)
    km, vm = pl.pallas_call(
        _memkv_kernel,
        grid=(B,),
        in_specs=[pl.BlockSpec((None, mem_len, D), lambda b: (b, 0, 0)), _full((1, D)),
                  _full((D, 2 * MEM_WIDTH)), _full((1, MEM_HEAD_DIM))],
        out_specs=[pl.BlockSpec((None, mem_len, MEM_WIDTH), lambda b: (b, 0, 0))] * 2,
        out_shape=[jax.ShapeDtypeStruct((B, mem_len, MEM_WIDTH), BF16)] * 2,
        compiler_params=_params("parallel"),
        name="mem_kv",
    )(mem, g_mem, w_mem_kv[0].astype(BF16), gkm)

    mem_spec = pl.BlockSpec((None, mem_len, MEM_WIDTH), lambda i: (i // (T // tm), 0, 0))
    x_mid = pl.pallas_call(
        _merge_kernel,
        grid=(N // tm,),
        in_specs=[row(D), row(D), row(DIL_OUT), row(FOX_WIDTH), row(MEM_WIDTH), mem_spec, mem_spec,
                  _full((D, 3 * D)), _full((1, 3 * D)), _full((DIL_OUT, D)), _full((FOX_WIDTH, D)),
                  _full((MEM_WIDTH, D)), _full((D, D))],
        out_specs=row(D),
        out_shape=jax.ShapeDtypeStruct((N, D), F32),
        scratch_shapes=[pltpu.VMEM((tm, D), BF16)],
        compiler_params=_params("parallel"),
        name="merge_outproj",
    )(x2d, h, ya, yb, qm, km, vm, w_gate[0].astype(BF16), b_gate, w_br_a[0].astype(BF16),
      w_br_b[0].astype(BF16), w_br_m[0].astype(BF16), w_out[0].astype(BF16))

    out = pl.pallas_call(
        _mlp_kernel,
        grid=(N // tm,),
        in_specs=[row(D), _full((1, D)), _full((D, D_FF)), _full((D_FF, D))],
        out_specs=row(D),
        out_shape=jax.ShapeDtypeStruct((N, D), F32),
        scratch_shapes=[pltpu.VMEM((tm, D), BF16)],
        compiler_params=_params("parallel"),
        name="mlp",
    )(x_mid, g_mlp, w_up[0].astype(BF16), w_down[0].astype(BF16))
    return out.reshape(B, T, D)
```

```python
import functools

import jax
import jax.numpy as jnp
import numpy as np
from jax import lax
from jax.experimental import pallas as pl
from jax.experimental.pallas import tpu as pltpu

D_MODEL = 1024
HEAD_DIM = 64
DIL_GROUPS = ((128, 1), (512, 4), (2048, 16))
N_DIL_GROUPS = 3
DIL_HEADS = 4
DIL_WIDTH = N_DIL_GROUPS * DIL_HEADS * HEAD_DIM
DIL_OUT = DIL_HEADS * HEAD_DIM
DIL_BACK = 128
FOX_HEADS = 8
FOX_WIDTH = FOX_HEADS * HEAD_DIM
MEM_HEADS = 4
MEM_HEAD_DIM = 128
MEM_WIDTH = MEM_HEADS * MEM_HEAD_DIM
ROT_DIM = HEAD_DIM // 4
ROPE_THETA = 500000.0
D_FF = 4 * D_MODEL
EPS = 1e-6

LANES = 128
MXU_COLS = 256
FL_PAD = LANES

ROW_TILE = 1024
INPROJ_ROW_TILE = 1024
INPROJ_COL = dict(qa=0, ka=DIL_WIDTH, va=2 * DIL_WIDTH, qb=3 * DIL_WIDTH, kb=3 * DIL_WIDTH + FOX_WIDTH,
                  vb=3 * DIL_WIDTH + 2 * FOX_WIDTH, qm=3 * DIL_WIDTH + 3 * FOX_WIDTH,
                  fl=3 * DIL_WIDTH + 3 * FOX_WIDTH + MEM_WIDTH)
INPROJ_ORDER = (("qa", 0), ("ka", 0), ("qa", 1), ("ka", 1), ("qa", 2), ("ka", 2), ("qb", 0), ("kb", 0),
                ("qb", 1), ("kb", 1), ("qm", 0), ("qm", 1), ("va", 2), ("va", 1), ("vb", 0), ("vb", 1), ("va", 0))
FOX_Q_TILE = 2048
FOX_K_TILE = 256
FOX_COLS = 512
FOX_DEN_ROWS = 16
FOX_LOOKAHEAD = 1
FOX_BLOCKS_PER_TRIP = 8
DECAY_ROWS = 16
CUMSUM_BLOCK = 256
DIL_TILE = 2048
DIL_MERGE_ROWS = 256
DIL_CHAINS = 16
DIL_LATER_ITEMS = 15

LOG2E = 1.4426950408889634
MASKED = -jnp.inf
BF16 = jnp.bfloat16
F32 = jnp.float32
NT_DIMS = (((1,), (1,)), ((), ()))


def _dot(a, b):
    return jnp.dot(a, b, preferred_element_type=F32)


def _dot_nt(a, b):
    return lax.dot_general(a, b, NT_DIMS, preferred_element_type=F32)


def _row_rmsnorm(x, g):
    ms = jnp.mean(x * x, axis=-1, keepdims=True)
    return x * lax.rsqrt(ms + EPS) * g


def _head_rmsnorm(y, blockdiag, gain, head_dim):
    y2 = (y * y).astype(BF16)
    half = y.shape[0] // 2
    ss = jnp.concatenate([_dot(y2[:half], blockdiag), _dot(y2[half:], blockdiag)], axis=0)
    return y * lax.rsqrt(ss * (1.0 / head_dim) + EPS) * gain


def _split3(c):
    hi = c.astype(BF16)
    r1 = c - hi.astype(F32)
    mid = r1.astype(BF16)
    lo = (r1 - mid.astype(F32)).astype(BF16)
    return hi, mid, lo


def _inproj_kernel(x_ref, gmix_ref, w_ref, bd64_ref, gqa_ref, gka_ref, gqbt_ref, gkb_ref,
                   gqm_ref, bf_ref, rc_ref, rs1_ref, rs2_ref,
                   h_ref, qa0_ref, qa1_ref, qa2_ref, ka0_ref, ka1_ref, ka2_ref, va0_ref, va1_ref, va2_ref,
                   qbt_ref, kb_ref, vbt_ref, qm_ref, lf_ref, perm_ref):
    h = _row_rmsnorm(x_ref[...], gmix_ref[...]).astype(BF16)
    h_ref[...] = h
    bd64 = bd64_ref[...]
    rc = rc_ref[...]
    rs1 = rs1_ref[...]
    rs2 = rs2_ref[...]
    cw = MXU_COLS
    tm = x_ref.shape[0]

    def proj(col, width=cw):
        return _dot(h, w_ref[:, col:col + width])

    def rope(y):
        down = pltpu.roll(y, cw - ROT_DIM // 2, 1)
        up = pltpu.roll(y, ROT_DIM // 2, 1)
        return jnp.concatenate(
            [y[:, c:c + LANES] * rc + down[:, c:c + LANES] * rs1 + up[:, c:c + LANES] * rs2
             for c in range(0, cw, LANES)], axis=1)

    def store_rows(out_ref, c, y):
        out_ref[:, c:c + cw] = y.astype(out_ref.dtype)

    def store_transposed(out_ref, c, y):
        out_ref[c:c + cw, :] = y.T.astype(out_ref.dtype)

    def store_transposed_normed(out_ref, c, y):
        yt = y.T
        gain_col = gqbt_ref[...]
        parts = []
        for r0 in range(0, cw, HEAD_DIM):
            blk = yt[r0:r0 + HEAD_DIM]
            ss = jnp.sum(blk * blk, axis=0, keepdims=True)
            gain = jnp.concatenate([gain_col[r0:r0 + HEAD_DIM]] * (tm // LANES), axis=1)
            parts.append(blk * lax.rsqrt(ss * (1.0 / HEAD_DIM) + EPS) * gain)
        out_ref[c:c + cw, :] = jnp.concatenate(parts, axis=0).astype(out_ref.dtype)

    def store_by_residue(out_ref, y):
        d = out_ref.shape[0]
        if d == 1:
            out_ref[0] = y.astype(out_ref.dtype)
            return
        for half in range(cw // LANES):
            perm_ref[half] = y[:, half * LANES:(half + 1) * LANES]
        for r in range(d):
            out_ref[r] = jnp.concatenate(
                [perm_ref[half, pl.ds(r, tm // d, stride=d), :] for half in range(cw // LANES)],
                axis=1).astype(out_ref.dtype)

    rows_of = lambda ref: [(functools.partial(store_rows, ref, c)) for c in range(0, ref.shape[1], cw)]
    cols_of = lambda ref: [(functools.partial(store_transposed, ref, c)) for c in range(0, ref.shape[0], cw)]
    residues_of = lambda refs: [functools.partial(store_by_residue, ref) for ref in refs]
    segments = {
        "qa": (residues_of((qa0_ref, qa1_ref, qa2_ref)), gqa_ref, bd64, HEAD_DIM, True),
        "ka": (residues_of((ka0_ref, ka1_ref, ka2_ref)), gka_ref, bd64, HEAD_DIM, True),
        "va": (residues_of((va0_ref, va1_ref, va2_ref)), None, None, None, False),
        "qb": ([functools.partial(store_transposed_normed, qbt_ref, c) for c in range(0, qbt_ref.shape[0], cw)],
               None, None, None, False),
        "kb": (rows_of(kb_ref), gkb_ref, bd64, HEAD_DIM, False),
        "vb": (cols_of(vbt_ref), None, None, None, False),
        "qm": (rows_of(qm_ref), gqm_ref, None, MEM_HEAD_DIM, False)}
    chunks = [(INPROJ_COL[name] + piece * cw, segments[name][0][piece]) + segments[name][1:]
              for name, piece in INPROJ_ORDER]
    z = proj(INPROJ_COL["fl"], FL_PAD) + bf_ref[...]
    y_next = proj(chunks[0][0])
    lf_ref[...] = jnp.minimum(z, 0.0) - jnp.log1p(jnp.exp(-jnp.abs(z)))
    for idx, (_, store, gain_ref, bd, hd, rot) in enumerate(chunks):
        y = y_next
        y_next = proj(chunks[idx + 1][0]) if idx + 1 < len(chunks) else None
        if gain_ref is not None and hd == LANES:
            gain = gain_ref[...]
            y = jnp.concatenate([_row_rmsnorm(y[:, c:c + hd], gain[:, c:c + hd]) for c in range(0, cw, hd)], axis=1)
        elif gain_ref is not None:
            y = _head_rmsnorm(y, bd, gain_ref[...], hd)
        if rot:
            y = rope(y)
        store(y)


def _cumsum_kernel(lf_ref, tri_ref, pk_ref, onek_ref, pq_ref, oneq_ref, cqt_ref, ck_ref):
    tri = tri_ref[...]
    n_blocks = lf_ref.shape[0] // CUMSUM_BLOCK
    n_hp = ck_ref.shape[0]
    lane = lax.broadcasted_iota(jnp.int32, (CUMSUM_BLOCK, LANES), 1)
    carry = jnp.zeros((1, LANES), F32)
    for blk in range(n_blocks):
        rows = slice(blk * CUMSUM_BLOCK, (blk + 1) * CUMSUM_BLOCK)
        local = _dot(tri, jnp.concatenate(_split3(lf_ref[rows, :]), axis=1))
        c = local[:, :LANES] + local[:, LANES:2 * LANES] + local[:, 2 * LANES:] + carry
        carry = c[CUMSUM_BLOCK - 1:CUMSUM_BLOCK, :]
        c = c * LOG2E
        hi, mid, lo = (t.astype(F32) for t in _split3(c))
        pieces = jnp.where(lane < FOX_HEADS, hi,
                           jnp.where(lane < 2 * FOX_HEADS, pltpu.roll(mid, FOX_HEADS, 1),
                                     pltpu.roll(lo, 2 * FOX_HEADS, 1))).astype(BF16)
        ke = _dot(pieces, pk_ref[...]) + onek_ref[...]
        qe = _dot_nt(pq_ref[...], pieces) + oneq_ref[...]
        for hp in range(n_hp):
            ck_ref[hp, rows, :] = ke[:, hp * LANES:(hp + 1) * LANES].astype(BF16)
            for e in range(2):
                r0 = (2 * hp + e) * DECAY_ROWS
                cqt_ref[hp, e, :, rows] = qe[r0:r0 + DECAY_ROWS, :].astype(BF16)


def _fox_kernel(qt_ref, cqt_ref, k_ref, ck_ref, vt_ref, o_ref, qa_ref, st_ref, bm_ref, m_ref, acc_ref):
    blk = pl.program_id(2)
    tq = FOX_Q_TILE
    tk = FOX_K_TILE
    hd = HEAD_DIM
    cw = FOX_COLS
    zq = jnp.zeros((hd, tq), BF16)
    zd = jnp.zeros((DECAY_ROWS, tq), BF16)
    zpad = jnp.zeros((MXU_COLS - 2 * hd - 2 * DECAY_ROWS, tq), BF16)
    for e in range(2):
        parts = [zq, zq, zd, zd, zpad]
        parts[e] = qt_ref[e * hd:(e + 1) * hd, :]
        parts[2 + e] = cqt_ref[e]
        qa_ref[e] = jnp.concatenate(parts, axis=0)
    units = [(e, c0) for e in range(2) for c0 in range(0, tq, cw)]

    def key_block(key_start):
        rows = pl.ds(pl.multiple_of(key_start, tk), tk)
        return jnp.concatenate([k_ref[rows, :], ck_ref[rows, :]], axis=1)

    def value_block(key_start):
        return vt_ref[:, pl.ds(pl.multiple_of(key_start, tk), tk)]

    def first_visible(u, key_off):
        c0 = units[u][1]
        return cw // 2 if key_off is not None and key_off >= c0 + cw // 2 else 0

    def issue_scores(buf, u, k_aug, key_off=None):
        e, c0 = units[u]
        lo = first_visible(u, key_off)
        st = _dot(k_aug, qa_ref[e, :, c0 + lo:c0 + cw])
        st_ref[buf, u, :, lo:] = st
        bm_ref[buf, u, :, lo:] = jnp.max(st, axis=0, keepdims=True)

    def update(buf, u, vt, key_off):
        e, c0 = units[u]
        lo = first_visible(u, key_off)
        st = st_ref[buf, u, :, lo:]
        if key_off is not None and key_off + tk - 1 > c0 + lo:
            assert key_off == c0 + lo
            masked = jnp.where(causal, st[:, :tk], MASKED)
            st = masked if st.shape[1] == tk else jnp.concatenate([masked, st[:, tk:]], axis=1)
            block_max = jnp.max(st, axis=0, keepdims=True)
        else:
            block_max = bm_ref[buf, u, :, lo:]
        m = m_ref[u, :, lo:]
        m_new = jnp.maximum(m, block_max)
        alpha = jnp.exp2(m - m_new)
        p = jnp.exp2(st - m_new).astype(BF16)
        vt_aug = jnp.concatenate([vt[e * hd:(e + 1) * hd, :], ones_rows], axis=0)
        acc_ref[u, :, lo:] = alpha * acc_ref[u, :, lo:] + _dot(vt_aug, p)
        m_ref[u, :, lo:] = m_new

    ones_rows = jnp.ones((FOX_DEN_ROWS, tk), BF16)
    causal = lax.broadcasted_iota(jnp.int32, (tk, tk), 0) <= lax.broadcasted_iota(jnp.int32, (tk, tk), 1)
    m_ref[...] = jnp.full(m_ref.shape, MASKED, F32)
    acc_ref[...] = jnp.zeros(acc_ref.shape, F32)
    k_first = key_block(0)
    for u in range(len(units)):
        issue_scores(0, u, k_first)

    per_trip = FOX_BLOCKS_PER_TRIP
    assert per_trip % 2 == 0 and (tq // tk) % per_trip == 0

    def block_group(jg, carry):
        for step in range(per_trip):
            j = per_trip * jg + step
            k_next = key_block((j + 1) * tk)
            vt = value_block(j * tk)
            for u in range(min(FOX_LOOKAHEAD, len(units))):
                issue_scores(1 - step % 2, u, k_next)
            for u in range(len(units)):
                if u + FOX_LOOKAHEAD < len(units):
                    issue_scores(1 - step % 2, u + FOX_LOOKAHEAD, k_next)
                update(step % 2, u, vt, None)
        return carry

    lax.fori_loop(0, blk * (tq // tk // per_trip), block_group, 0)
    for jj in range(tq // tk):
        key_off = jj * tk
        active = [u for u, (e, c0) in enumerate(units) if c0 + cw > key_off]
        nxt_off = key_off + tk
        nxt_active = [u for u, (e, c0) in enumerate(units) if c0 + cw > nxt_off] if nxt_off < tq else []
        if nxt_active:
            k_next = key_block(blk * tq + nxt_off)
        vt = value_block(blk * tq + key_off)
        for u in active:
            if u in nxt_active:
                issue_scores((jj + 1) % 2, u, k_next, nxt_off)
            update(jj % 2, u, vt, key_off)
    per_head = []
    for e in range(2):
        cols = [acc_ref[u, :hd, :] * (1.0 / acc_ref[u, hd:hd + 1, :]) for u, unit in enumerate(units) if unit[0] == e]
        per_head.append(jnp.concatenate(cols, axis=1))
    ot = jnp.concatenate(per_head, axis=0)
    o_ref[...] = ot.T.astype(o_ref.dtype)


def _dilated_kernel(hm_ref, q0_ref, k0_ref, kp0_ref, v0_ref, vp0_ref, q1_ref, k1_ref, kp1_ref, v1_ref, vp1_ref,
                    q2_ref, k2_ref, kp2_ref, v2_ref, vp2_ref, ya_ref, o_scr, l_scr):
    jt = pl.program_id(1)
    c = DIL_BACK
    w = DIL_OUT
    nh = DIL_HEADS
    tile = ya_ref.shape[0]
    qi = lax.broadcasted_iota(jnp.int32, (nh * c, 2 * c), 0) & (c - 1)
    kj = lax.broadcasted_iota(jnp.int32, (nh * c, 2 * c), 1)
    dist = qi + c - kj
    in_band = (dist >= 0) & (dist <= DIL_BACK)
    in_band_first = in_band & (kj >= jnp.where(jt > 0, 0, c))
    lane = lax.broadcasted_iota(jnp.int32, (c, w), 1)
    in_head = [(lane >= hh * HEAD_DIM) & (lane < (hh + 1) * HEAD_DIM) for hh in range(nh)]

    def attend(gi, d, items, mask):
        def scores(item):
            q, kk, _, _ = item
            return _dot_nt(jnp.concatenate([q * hm_ref[hh] for hh in range(nh)], axis=0), kk)

        s_next = scores(items[0])
        for idx, (_, _, vv, tok0) in enumerate(items):
            s = s_next
            s_next = scores(items[idx + 1]) if idx + 1 < len(items) else None
            s = jnp.where(mask, s, MASKED)
            m = jnp.max(s, axis=-1, keepdims=True)
            p = jnp.exp2(s - m)
            den = jnp.sum(p, axis=-1, keepdims=True)
            o4 = _dot(p.astype(BF16), vv) * (1.0 / den)
            lse4 = m + jnp.log2(den)
            o = jnp.zeros((c, w), F32)
            lse = jnp.zeros((c, w), F32)
            for hh in range(nh):
                rows = slice(hh * c, (hh + 1) * c)
                o = jnp.where(in_head[hh], o4[rows], o)
                lse = jnp.where(in_head[hh], lse4[rows], lse)
            for half in range(w // LANES):
                lanes = slice(half * LANES, (half + 1) * LANES)
                o_scr[gi, half, pl.ds(tok0, c, stride=d), :] = o[:, lanes]
                l_scr[gi, half, pl.ds(tok0, c, stride=d), :] = lse[:, lanes]

    groups = ((q0_ref, k0_ref, kp0_ref, v0_ref, vp0_ref), (q1_ref, k1_ref, kp1_ref, v1_ref, vp1_ref),
              (q2_ref, k2_ref, kp2_ref, v2_ref, vp2_ref))
    for gi, (q_ref, k_ref, kp_ref, v_ref, vp_ref) in enumerate(groups):
        d = q_ref.shape[0]
        n_sb = q_ref.shape[1] // c
        res_chunk = min(d, DIL_CHAINS)
        assert d % res_chunk == 0

        def first_item(r, q_ref=q_ref, k_ref=k_ref, kp_ref=kp_ref, v_ref=v_ref, vp_ref=vp_ref):
            kk = jnp.concatenate([kp_ref[r], k_ref[r, 0:c]], axis=0)
            vv = jnp.concatenate([vp_ref[r], v_ref[r, 0:c]], axis=0)
            return q_ref[r, 0:c], kk, vv, r

        def later_item(r, sb, d=d, q_ref=q_ref, k_ref=k_ref, v_ref=v_ref):
            start = pl.multiple_of(sb * c, c)
            window = pl.ds(start - c, 2 * c)
            return q_ref[r, pl.ds(start, c)], k_ref[r, window], v_ref[r, window], sb * (c * d) + r

        def first_chunk(it, carry, gi=gi, d=d, res_chunk=res_chunk, first_item=first_item):
            attend(gi, d, [first_item(it * res_chunk + i) for i in range(res_chunk)], in_band_first)
            return carry

        if d == res_chunk:
            first_chunk(0, 0)
        else:
            lax.fori_loop(0, d // res_chunk, first_chunk, 0)
        if n_sb > 1:
            sb_chunk = max(1, DIL_LATER_ITEMS // res_chunk)
            sb_chunk = max(k for k in range(1, sb_chunk + 1) if (n_sb - 1) % k == 0)
            assert d == res_chunk

            def later_chunk(it, carry, gi=gi, d=d, sb_chunk=sb_chunk, later_item=later_item):
                attend(gi, d, [later_item(r, 1 + it * sb_chunk + i) for i in range(sb_chunk) for r in range(d)],
                       in_band)
                return carry

            lax.fori_loop(0, (n_sb - 1) // sb_chunk, later_chunk, 0)

    def merge(ch, carry):
        rows = pl.ds(pl.multiple_of(ch * DIL_MERGE_ROWS, DIL_MERGE_ROWS), DIL_MERGE_ROWS)
        for half in range(w // LANES):
            l = [l_scr[g, half, rows, :] for g in range(N_DIL_GROUPS)]
            mx = jnp.maximum(jnp.maximum(l[0], l[1]), l[2])
            e = [jnp.exp2(lg - mx) for lg in l]
            num = e[0] * o_scr[0, half, rows, :] + e[1] * o_scr[1, half, rows, :] + e[2] * o_scr[2, half, rows, :]
            ya_ref[rows, half * LANES:(half + 1) * LANES] = (num / (e[0] + e[1] + e[2])).astype(ya_ref.dtype)
        return carry

    lax.fori_loop(0, tile // DIL_MERGE_ROWS, merge, 0)


def _memkv_kernel(mem_ref, g_ref, w_ref, gk_ref, km_ref, vm_ref):
    mn = _row_rmsnorm(mem_ref[...], g_ref[...]).astype(BF16)
    kv = _dot(mn, w_ref[...])
    gk = gk_ref[...]
    for hh in range(MEM_HEADS):
        cols = slice(hh * MEM_HEAD_DIM, (hh + 1) * MEM_HEAD_DIM)
        km_ref[:, cols] = _row_rmsnorm(kv[:, cols], gk).astype(km_ref.dtype)
    vm_ref[...] = kv[:, MEM_WIDTH:].astype(vm_ref.dtype)


def _merge_kernel(x_ref, h_ref, ya_ref, yb_ref, qm_ref, km_ref, vm_ref,
                  wg_ref, bg_ref, wa_ref, wb_ref, wm_ref, wo_ref, out_ref, merged_ref):
    ya = ya_ref[...]
    ym = []
    for hh in range(MEM_HEADS):
        cols = slice(hh * MEM_HEAD_DIM, (hh + 1) * MEM_HEAD_DIM)
        s = _dot_nt(qm_ref[:, cols], km_ref[:, cols])
        p = jnp.exp(s - jnp.max(s, axis=-1, keepdims=True))
        den = jnp.sum(p, axis=-1, keepdims=True)
        ym.append((_dot(p.astype(BF16), vm_ref[:, cols]) / den).astype(BF16))
    ym = jnp.concatenate(ym, axis=1)
    yb = yb_ref[...]
    h = h_ref[...]
    cw = 2 * MXU_COLS
    for c in range(0, D_MODEL, cw):
        merged = None
        for k, (y, w_ref) in enumerate(((ya, wa_ref), (yb, wb_ref), (ym, wm_ref))):
            gcol = k * D_MODEL + c
            gate = jax.nn.sigmoid(_dot(h, wg_ref[:, gcol:gcol + cw]) + bg_ref[:, gcol:gcol + cw])
            term = gate * _dot(y, w_ref[:, c:c + cw])
            merged = term if merged is None else merged + term
        merged_ref[:, c:c + cw] = merged.astype(BF16)
    out_ref[...] = x_ref[...] + _dot(merged_ref[...], wo_ref[...])


def _mlp_kernel(x_ref, g_ref, wu_ref, wd_ref, out_ref, h2_ref):
    x = x_ref[...]
    h2_ref[...] = _row_rmsnorm(x, g_ref[...]).astype(BF16)
    cw = D_MODEL
    acc = x
    for c in range(0, D_FF, cw):
        u = jnp.maximum(_dot(h2_ref[...], wu_ref[:, c:c + cw]), 0.0)
        acc = acc + _dot((u * u).astype(BF16), wd_ref[c:c + cw, :])
    out_ref[...] = acc


def _full(shape):
    return pl.BlockSpec(shape, lambda *_: (0,) * len(shape), pipeline_mode=pl.Buffered(1))


def _params(*sem):
    return pltpu.CompilerParams(dimension_semantics=sem)


def _blockdiag(width, head_dim):
    r = jnp.arange(width) // head_dim
    return (r[:, None] == r[None, :]).astype(BF16)


def _decay_placement(n_hp):
    pk = np.zeros((LANES, n_hp * LANES), np.float32)
    onek = np.zeros((1, n_hp * LANES), np.float32)
    pq = np.zeros((n_hp * 2 * DECAY_ROWS, LANES), np.float32)
    oneq = np.zeros((n_hp * 2 * DECAY_ROWS, CUMSUM_BLOCK), np.float32)
    for hp in range(n_hp):
        for e in range(2):
            head = 2 * hp + e
            kcol = hp * LANES + DECAY_ROWS * e
            qrow = (2 * hp + e) * DECAY_ROWS
            for t in range(3):
                onek[0, kcol + t] = 1.0
                pk[t * FOX_HEADS + head, kcol + 3 + t] = -1.0
                pq[qrow + t, t * FOX_HEADS + head] = 1.0
                oneq[qrow + 3 + t, :] = 1.0
    return jnp.asarray(pk, BF16), jnp.asarray(onek), jnp.asarray(pq, BF16), jnp.asarray(oneq)


def _rope_tables(seq, width):
    half = ROT_DIM // 2
    inv_freq = ROPE_THETA ** (-jnp.arange(0, ROT_DIM, 2, dtype=F32) / ROT_DIM)
    ang = jnp.arange(seq, dtype=F32)[:, None] * inv_freq[None, :]
    cos, sin = jnp.cos(ang), jnp.sin(ang)
    ones = jnp.ones((seq, HEAD_DIM - ROT_DIM), F32)
    zeros = jnp.zeros((seq, HEAD_DIM - ROT_DIM), F32)
    zhalf = jnp.zeros((seq, half), F32)
    rc = jnp.concatenate([cos, cos, ones], axis=1)
    rs1 = jnp.concatenate([-sin, zhalf, zeros], axis=1)
    rs2 = jnp.concatenate([zhalf, sin, zeros], axis=1)
    reps = width // HEAD_DIM
    return tuple(jnp.tile(t, (1, reps)) for t in (rc, rs1, rs2))


def kernel(x, mem, g_mix, w_in, b_f, g_qA, g_kA, g_qB, g_kB, g_mem, w_mem_kv, g_qM, g_kM, w_gate, b_gate,
           w_br_a, w_br_b, w_br_m, w_out, g_mlp, w_up, w_down):
    B, T, D = x.shape
    assert D == D_MODEL and w_in.shape[0] == 1, "single-layer kernel"
    N = B * T
    mem_len = mem.shape[1]
    tm = ROW_TILE
    assert T % tm == 0 and T % FOX_Q_TILE == 0 and FOX_Q_TILE % FOX_K_TILE == 0
    x2d = x.reshape(N, D)

    wi = w_in[0]
    offs = [0]
    for wdt in (DIL_WIDTH, DIL_WIDTH, DIL_WIDTH, FOX_WIDTH, FOX_WIDTH, FOX_WIDTH, FOX_HEADS, MEM_WIDTH):
        offs.append(offs[-1] + wdt)
    w_fl = jnp.pad(wi[:, offs[6]:offs[7]], ((0, 0), (0, FL_PAD - FOX_HEADS)))
    w_all = jnp.concatenate([wi[:, :offs[6]], wi[:, offs[7]:], w_fl], axis=1).astype(BF16)
    assert offs[6] == INPROJ_COL["qm"] and w_all.shape[1] == INPROJ_COL["fl"] + FL_PAD
    n_cols = w_all.shape[1]
    bf_pad = jnp.pad(b_f[0], (0, FL_PAD - FOX_HEADS)).reshape(1, FL_PAD)
    cw = MXU_COLS
    q_scale = HEAD_DIM ** -0.5
    gqa = (jnp.tile(g_qA[0], cw // HEAD_DIM) * (q_scale * LOG2E)).reshape(1, cw)
    gka = jnp.tile(g_kA[0], cw // HEAD_DIM).reshape(1, cw)
    gqbt = jnp.broadcast_to((jnp.tile(g_qB[0], cw // HEAD_DIM) * (q_scale * LOG2E))[:, None], (cw, LANES))
    gkb = jnp.tile(g_kB[0], cw // HEAD_DIM).reshape(1, cw)
    gqm = (jnp.tile(g_qM[0], cw // MEM_HEAD_DIM) * MEM_HEAD_DIM ** -0.5).reshape(1, cw)
    rc, rs1, rs2 = _rope_tables(T, LANES)
    bd64 = _blockdiag(cw, HEAD_DIM)

    row = lambda width: pl.BlockSpec((tm, width), lambda i: (i, 0))
    tm1 = INPROJ_ROW_TILE
    per_seq = T // tm1
    assert T % tm1 == 0
    row1 = lambda width: pl.BlockSpec((tm1, width), lambda i: (i, 0))
    rope_spec = pl.BlockSpec((tm1, LANES), lambda i: (i % per_seq, 0))
    tspec = pl.BlockSpec((None, FOX_WIDTH, tm1), lambda i: (i // per_seq, 0, i % per_seq))
    tshape = jax.ShapeDtypeStruct((B, FOX_WIDTH, T), BF16)
    rshape = lambda width: jax.ShapeDtypeStruct((N, width), BF16)
    dils = [d for _, d in DIL_GROUPS]
    assert all(win // d == DIL_BACK and tm1 % d == 0 and (tm1 // d) % 16 == 0 for win, d in DIL_GROUPS)
    dspecs = [pl.BlockSpec((None, d, tm1 // d, DIL_OUT), lambda i: (i // per_seq, 0, i % per_seq, 0))
              for d in dils]
    dshapes = [jax.ShapeDtypeStruct((B, d, T // d, DIL_OUT), BF16) for d in dils]
    outs = pl.pallas_call(
        _inproj_kernel,
        grid=(N // tm1,),
        in_specs=[row1(D), _full((1, D)), _full((D, n_cols)), _full((cw, cw)),
                  _full((1, cw)), _full((1, cw)), _full((cw, LANES)), _full((1, cw)), _full((1, cw)),
                  _full((1, FL_PAD)), rope_spec, rope_spec, rope_spec],
        out_specs=[row1(D)] + dspecs * 3 + [tspec, row1(FOX_WIDTH), tspec, row1(MEM_WIDTH), row1(FL_PAD)],
        out_shape=[rshape(D)] + dshapes * 3 + [tshape, rshape(FOX_WIDTH), tshape, rshape(MEM_WIDTH),
                                               jax.ShapeDtypeStruct((N, FL_PAD), F32)],
        scratch_shapes=[pltpu.VMEM((cw // LANES, tm1, LANES), F32)],
        compiler_params=_params("parallel"),
        name="inproj",
    )(x2d, g_mix, w_all, bd64, gqa, gka, gqbt, gkb, gqm, bf_pad, rc, rs1, rs2)
    h, qa_g, ka_g, va_g = outs[0], outs[1:4], outs[4:7], outs[7:10]
    qbt, kb, vbt, qm, logf = outs[10:]

    n_hp = FOX_HEADS // 2
    tri = (jnp.arange(CUMSUM_BLOCK)[:, None] >= jnp.arange(CUMSUM_BLOCK)[None, :]).astype(BF16)
    pk, onek, pq, oneq = _decay_placement(n_hp)
    cqt, ck_ext = pl.pallas_call(
        _cumsum_kernel,
        grid=(B,),
        in_specs=[pl.BlockSpec((T, FL_PAD), lambda b: (b, 0)), _full((CUMSUM_BLOCK, CUMSUM_BLOCK)),
                  _full(pk.shape), _full(onek.shape), _full(pq.shape), _full(oneq.shape)],
        out_specs=[pl.BlockSpec((None, n_hp, 2, DECAY_ROWS, T), lambda b: (b, 0, 0, 0, 0)),
                   pl.BlockSpec((None, n_hp, T, LANES), lambda b: (b, 0, 0, 0))],
        out_shape=[jax.ShapeDtypeStruct((B, n_hp, 2, DECAY_ROWS, T), BF16),
                   jax.ShapeDtypeStruct((B, n_hp, T, LANES), BF16)],
        compiler_params=_params("parallel"),
        name="decay_cumsum",
    )(logf, tri, pk, onek, pq, oneq)

    tq = FOX_Q_TILE
    nq = T // tq
    n_units = 2 * tq // FOX_COLS
    yb = pl.pallas_call(
        _fox_kernel,
        grid=(B, n_hp, nq),
        in_specs=[pl.BlockSpec((None, LANES, tq), lambda b, hp, i: (b, hp, i)),
                  pl.BlockSpec((None, None, 2, DECAY_ROWS, tq), lambda b, hp, i: (b, hp, 0, 0, i)),
                  pl.BlockSpec((T, LANES), lambda b, hp, i: (b, hp)),
                  pl.BlockSpec((None, None, T, LANES), lambda b, hp, i: (b, hp, 0, 0)),
                  pl.BlockSpec((None, LANES, T), lambda b, hp, i: (b, hp, 0))],
        out_specs=pl.BlockSpec((tq, LANES), lambda b, hp, i: (b * nq + i, hp)),
        out_shape=jax.ShapeDtypeStruct((N, FOX_WIDTH), BF16),
        scratch_shapes=[pltpu.VMEM((2, MXU_COLS, tq), BF16),
                        pltpu.VMEM((2, n_units, FOX_K_TILE, FOX_COLS), F32),
                        pltpu.VMEM((2, n_units, 1, FOX_COLS), F32),
                        pltpu.VMEM((n_units, 1, FOX_COLS), F32),
                        pltpu.VMEM((n_units, HEAD_DIM + FOX_DEN_ROWS, FOX_COLS), F32)],
        compiler_params=_params("parallel", "parallel", "arbitrary"),
        name="fox_attention",
    )(qbt, cqt, kb, ck_ext, vbt)

    tile = DIL_TILE
    assert T % tile == 0 and all((tile // d) % DIL_BACK == 0 for d in dils)
    lane_head = jnp.arange(DIL_OUT) // HEAD_DIM
    head_mask = jnp.broadcast_to((lane_head[None, :] == jnp.arange(DIL_HEADS)[:, None])[:, None, :],
                                 (DIL_HEADS, DIL_BACK, DIL_OUT)).astype(BF16)
    dil_specs, dil_args = [], []
    for gi, d in enumerate(dils):
        per_tile = tile // d // DIL_BACK
        cur = pl.BlockSpec((None, d, tile // d, DIL_OUT), lambda b, j: (b, 0, j, 0))
        prev = pl.BlockSpec((None, d, DIL_BACK, DIL_OUT),
                            lambda b, j, per_tile=per_tile: (b, 0, jnp.maximum(j * per_tile - 1, 0), 0))
        dil_specs += [cur, cur, prev, cur, prev]
        dil_args += [qa_g[gi], ka_g[gi], ka_g[gi], va_g[gi], va_g[gi]]
    ya = pl.pallas_call(
        _dilated_kernel,
        grid=(B, T // tile),
        in_specs=[_full((DIL_HEADS, DIL_BACK, DIL_OUT))] + dil_specs,
        out_specs=pl.BlockSpec((tile, DIL_OUT), lambda b, j: (b * (T // tile) + j, 0)),
        out_shape=jax.ShapeDtypeStruct((N, DIL_OUT), BF16),
        scratch_shapes=[pltpu.VMEM((N_DIL_GROUPS, DIL_OUT // LANES, tile, LANES), F32)] * 2,
        compiler_params=_params("parallel", "arbitrary"),
        name="dilated_attention",
    )(head_mask, *dil_args)

    gkm = g_kM[0].reshape(1, MEM_HEAD_DIM)
    km, vm = pl.pallas_call(
        _memkv_kernel,
        grid=(B,),
        in_specs=[pl.BlockSpec((None, mem_len, D), lambda b: (b, 0, 0)), _full((1, D)),
                  _full((D, 2 * MEM_WIDTH)), _full((1, MEM_HEAD_DIM))],
        out_specs=[pl.BlockSpec((None, mem_len, MEM_WIDTH), lambda b: (b, 0, 0))] * 2,
        out_shape=[jax.ShapeDtypeStruct((B, mem_len, MEM_WIDTH), BF16)] * 2,
        compiler_params=_params("parallel"),
        name="mem_kv",
    )(mem, g_mem, w_mem_kv[0].astype(BF16), gkm)

    mem_spec = pl.BlockSpec((None, mem_len, MEM_WIDTH), lambda i: (i // (T // tm), 0, 0))
    x_mid = pl.pallas_call(
        _merge_kernel,
        grid=(N // tm,),
        in_specs=[row(D), row(D), row(DIL_OUT), row(FOX_WIDTH), row(MEM_WIDTH), mem_spec, mem_spec,
                  _full((D, 3 * D)), _full((1, 3 * D)), _full((DIL_OUT, D)), _full((FOX_WIDTH, D)),
                  _full((MEM_WIDTH, D)), _full((D, D))],
        out_specs=row(D),
        out_shape=jax.ShapeDtypeStruct((N, D), F32),
        scratch_shapes=[pltpu.VMEM((tm, D), BF16)],
        compiler_params=_params("parallel"),
        name="merge_outproj",
    )(x2d, h, ya, yb, qm, km, vm, w_gate[0].astype(BF16), b_gate, w_br_a[0].astype(BF16),
      w_br_b[0].astype(BF16), w_br_m[0].astype(BF16), w_out[0].astype(BF16))

    out = pl.pallas_call(
        _mlp_kernel,
        grid=(N // tm,),
        in_specs=[row(D), _full((1, D)), _full((D, D_FF)), _full((D_FF, D))],
        out_specs=row(D),
        out_shape=jax.ShapeDtypeStruct((N, D), F32),
        scratch_shapes=[pltpu.VMEM((tm, D), BF16)],
        compiler_params=_params("parallel"),
        name="mlp",
    )(x_mid, g_mlp, w_up[0].astype(BF16), w_down[0].astype(BF16))
    return out.reshape(B, T, D)
```

```python
import functools

import jax
import jax.numpy as jnp
import numpy as np
from jax import lax
from jax.experimental import pallas as pl
from jax.experimental.pallas import tpu as pltpu

D_MODEL = 1024
HEAD_DIM = 64
DIL_GROUPS = ((128, 1), (512, 4), (2048, 16))
N_DIL_GROUPS = 3
DIL_HEADS = 4
DIL_WIDTH = N_DIL_GROUPS * DIL_HEADS * HEAD_DIM
DIL_OUT = DIL_HEADS * HEAD_DIM
DIL_BACK = 128
FOX_HEADS = 8
FOX_WIDTH = FOX_HEADS * HEAD_DIM
MEM_HEADS = 4
MEM_HEAD_DIM = 128
MEM_WIDTH = MEM_HEADS * MEM_HEAD_DIM
ROT_DIM = HEAD_DIM // 4
ROPE_THETA = 500000.0
D_FF = 4 * D_MODEL
EPS = 1e-6

LANES = 128
MXU_COLS = 256
FL_PAD = LANES

ROW_TILE = 1024
INPROJ_ROW_TILE = 1024
INPROJ_COL = dict(qa=0, ka=DIL_WIDTH, va=2 * DIL_WIDTH, qb=3 * DIL_WIDTH, kb=3 * DIL_WIDTH + FOX_WIDTH,
                  vb=3 * DIL_WIDTH + 2 * FOX_WIDTH, qm=3 * DIL_WIDTH + 3 * FOX_WIDTH,
                  fl=3 * DIL_WIDTH + 3 * FOX_WIDTH + MEM_WIDTH)
INPROJ_ORDER = (("qa", 0), ("ka", 0), ("qa", 1), ("ka", 1), ("qa", 2), ("ka", 2), ("qb", 0), ("kb", 0),
                ("qb", 1), ("kb", 1), ("qm", 0), ("qm", 1), ("va", 2), ("va", 1), ("vb", 0), ("vb", 1), ("va", 0))
FOX_Q_TILE = 2048
FOX_K_TILE = 256
FOX_COLS = 512
FOX_DEN_ROWS = 16
FOX_BLOCKS_PER_TRIP = 4
DECAY_ROWS = 16
CUMSUM_BLOCK = 256
DIL_TILE = 2048
DIL_MERGE_ROWS = 256
DIL_CHAINS = 16
DIL_LATER_ITEMS = 15

LOG2E = 1.4426950408889634
MASKED = -jnp.inf
BF16 = jnp.bfloat16
F32 = jnp.float32
NT_DIMS = (((1,), (1,)), ((), ()))


def _dot(a, b):
    return jnp.dot(a, b, preferred_element_type=F32)


def _dot_nt(a, b):
    return lax.dot_general(a, b, NT_DIMS, preferred_element_type=F32)


def _row_rmsnorm(x, g):
    ms = jnp.mean(x * x, axis=-1, keepdims=True)
    return x * lax.rsqrt(ms + EPS) * g


def _head_rmsnorm(y, blockdiag, gain, head_dim):
    ss = _dot((y * y).astype(BF16), blockdiag)
    return y * lax.rsqrt(ss * (1.0 / head_dim) + EPS) * gain


def _split3(c):
    hi = c.astype(BF16)
    r1 = c - hi.astype(F32)
    mid = r1.astype(BF16)
    lo = (r1 - mid.astype(F32)).astype(BF16)
    return hi, mid, lo


def _inproj_kernel(x_ref, gmix_ref, w_ref, bd64_ref, gqa_ref, gka_ref, gqbt_ref, gkb_ref,
                   gqm_ref, bf_ref, rc_ref, rs1_ref, rs2_ref,
                   h_ref, qa0_ref, qa1_ref, qa2_ref, ka0_ref, ka1_ref, ka2_ref, va0_ref, va1_ref, va2_ref,
                   qbt_ref, kb_ref, vbt_ref, qm_ref, lf_ref, perm_ref):
    h = _row_rmsnorm(x_ref[...], gmix_ref[...]).astype(BF16)
    h_ref[...] = h
    bd64 = bd64_ref[...]
    rc = rc_ref[...]
    rs1 = rs1_ref[...]
    rs2 = rs2_ref[...]
    cw = MXU_COLS
    tm = x_ref.shape[0]

    def proj(col, width=cw):
        return _dot(h, w_ref[:, col:col + width])

    def rope(y):
        down = pltpu.roll(y, cw - ROT_DIM // 2, 1)
        up = pltpu.roll(y, ROT_DIM // 2, 1)
        return jnp.concatenate(
            [y[:, c:c + LANES] * rc + down[:, c:c + LANES] * rs1 + up[:, c:c + LANES] * rs2
             for c in range(0, cw, LANES)], axis=1)

    def store_rows(out_ref, c, y):
        out_ref[:, c:c + cw] = y.astype(out_ref.dtype)

    def store_transposed(out_ref, c, y):
        out_ref[c:c + cw, :] = y.T.astype(out_ref.dtype)

    def store_transposed_normed(out_ref, c, y):
        yt = y.T
        gain_col = gqbt_ref[...]
        parts = []
        for r0 in range(0, cw, HEAD_DIM):
            blk = yt[r0:r0 + HEAD_DIM]
            ss = jnp.sum(blk * blk, axis=0, keepdims=True)
            gain = jnp.concatenate([gain_col[r0:r0 + HEAD_DIM]] * (tm // LANES), axis=1)
            parts.append(blk * lax.rsqrt(ss * (1.0 / HEAD_DIM) + EPS) * gain)
        out_ref[c:c + cw, :] = jnp.concatenate(parts, axis=0).astype(out_ref.dtype)

    def store_by_residue(out_ref, y):
        d = out_ref.shape[0]
        if d == 1:
            out_ref[0] = y.astype(out_ref.dtype)
            return
        for half in range(cw // LANES):
            perm_ref[half] = y[:, half * LANES:(half + 1) * LANES]
        for r in range(d):
            out_ref[r] = jnp.concatenate(
                [perm_ref[half, pl.ds(r, tm // d, stride=d), :] for half in range(cw // LANES)],
                axis=1).astype(out_ref.dtype)

    rows_of = lambda ref: [(functools.partial(store_rows, ref, c)) for c in range(0, ref.shape[1], cw)]
    cols_of = lambda ref: [(functools.partial(store_transposed, ref, c)) for c in range(0, ref.shape[0], cw)]
    residues_of = lambda refs: [functools.partial(store_by_residue, ref) for ref in refs]
    segments = {
        "qa": (residues_of((qa0_ref, qa1_ref, qa2_ref)), gqa_ref, bd64, HEAD_DIM, True),
        "ka": (residues_of((ka0_ref, ka1_ref, ka2_ref)), gka_ref, bd64, HEAD_DIM, True),
        "va": (residues_of((va0_ref, va1_ref, va2_ref)), None, None, None, False),
        "qb": ([functools.partial(store_transposed_normed, qbt_ref, c) for c in range(0, qbt_ref.shape[0], cw)],
               None, None, None, False),
        "kb": (rows_of(kb_ref), gkb_ref, bd64, HEAD_DIM, False),
        "vb": (cols_of(vbt_ref), None, None, None, False),
        "qm": (rows_of(qm_ref), gqm_ref, None, MEM_HEAD_DIM, False)}
    chunks = [(INPROJ_COL[name] + piece * cw, segments[name][0][piece]) + segments[name][1:]
              for name, piece in INPROJ_ORDER]
    z = proj(INPROJ_COL["fl"], FL_PAD) + bf_ref[...]
    y_next = proj(chunks[0][0])
    lf_ref[...] = jnp.minimum(z, 0.0) - jnp.log1p(jnp.exp(-jnp.abs(z)))
    for idx, (_, store, gain_ref, bd, hd, rot) in enumerate(chunks):
        y = y_next
        y_next = proj(chunks[idx + 1][0]) if idx + 1 < len(chunks) else None
        if gain_ref is not None and hd == LANES:
            gain = gain_ref[...]
            y = jnp.concatenate([_row_rmsnorm(y[:, c:c + hd], gain[:, c:c + hd]) for c in range(0, cw, hd)], axis=1)
        elif gain_ref is not None:
            y = _head_rmsnorm(y, bd, gain_ref[...], hd)
        if rot:
            y = rope(y)
        store(y)


def _cumsum_kernel(lf_ref, tri_ref, pk_ref, onek_ref, pq_ref, oneq_ref, cqt_ref, ck_ref):
    tri = tri_ref[...]
    n_blocks = lf_ref.shape[0] // CUMSUM_BLOCK
    n_hp = ck_ref.shape[0]
    lane = lax.broadcasted_iota(jnp.int32, (CUMSUM_BLOCK, LANES), 1)
    carry = jnp.zeros((1, LANES), F32)
    for blk in range(n_blocks):
        rows = slice(blk * CUMSUM_BLOCK, (blk + 1) * CUMSUM_BLOCK)
        local = _dot(tri, jnp.concatenate(_split3(lf_ref[rows, :]), axis=1))
        c = local[:, :LANES] + local[:, LANES:2 * LANES] + local[:, 2 * LANES:] + carry
        carry = c[CUMSUM_BLOCK - 1:CUMSUM_BLOCK, :]
        c = c * LOG2E
        hi, mid, lo = (t.astype(F32) for t in _split3(c))
        pieces = jnp.where(lane < FOX_HEADS, hi,
                           jnp.where(lane < 2 * FOX_HEADS, pltpu.roll(mid, FOX_HEADS, 1),
                                     pltpu.roll(lo, 2 * FOX_HEADS, 1))).astype(BF16)
        ke = _dot(pieces, pk_ref[...]) + onek_ref[...]
        qe = _dot_nt(pq_ref[...], pieces) + oneq_ref[...]
        for hp in range(n_hp):
            ck_ref[hp, rows, :] = ke[:, hp * LANES:(hp + 1) * LANES].astype(BF16)
            for e in range(2):
                r0 = (2 * hp + e) * DECAY_ROWS
                cqt_ref[hp, e, :, rows] = qe[r0:r0 + DECAY_ROWS, :].astype(BF16)


def _fox_kernel(qt_ref, cqt_ref, k_ref, ck_ref, vt_ref, o_ref, qa_ref, st_ref, bm_ref, m_ref, acc_ref):
    blk = pl.program_id(2)
    tq = FOX_Q_TILE
    tk = FOX_K_TILE
    hd = HEAD_DIM
    cw = FOX_COLS
    zq = jnp.zeros((hd, tq), BF16)
    zd = jnp.zeros((DECAY_ROWS, tq), BF16)
    zpad = jnp.zeros((MXU_COLS - 2 * hd - 2 * DECAY_ROWS, tq), BF16)
    for e in range(2):
        parts = [zq, zq, zd, zd, zpad]
        parts[e] = qt_ref[e * hd:(e + 1) * hd, :]
        parts[2 + e] = cqt_ref[e]
        qa_ref[e] = jnp.concatenate(parts, axis=0)
    units = [(e, c0) for e in range(2) for c0 in range(0, tq, cw)]

    def key_block(key_start):
        rows = pl.ds(pl.multiple_of(key_start, tk), tk)
        return jnp.concatenate([k_ref[rows, :], ck_ref[rows, :]], axis=1)

    def value_block(key_start):
        return vt_ref[:, pl.ds(pl.multiple_of(key_start, tk), tk)]

    def first_visible(u, key_off):
        c0 = units[u][1]
        return cw // 2 if key_off is not None and key_off >= c0 + cw // 2 else 0

    def issue_scores(buf, u, k_aug, key_off=None):
        e, c0 = units[u]
        lo = first_visible(u, key_off)
        st = _dot(k_aug, qa_ref[e, :, c0 + lo:c0 + cw])
        st_ref[buf, u, :, lo:] = st
        bm_ref[buf, u, :, lo:] = jnp.max(st, axis=0, keepdims=True)

    def update(buf, u, vt, key_off):
        e, c0 = units[u]
        lo = first_visible(u, key_off)
        st = st_ref[buf, u, :, lo:]
        if key_off is not None and key_off + tk - 1 > c0 + lo:
            assert key_off == c0 + lo
            masked = jnp.where(causal, st[:, :tk], MASKED)
            st = masked if st.shape[1] == tk else jnp.concatenate([masked, st[:, tk:]], axis=1)
            block_max = jnp.max(st, axis=0, keepdims=True)
        else:
            block_max = bm_ref[buf, u, :, lo:]
        m = m_ref[u, :, lo:]
        m_new = jnp.maximum(m, block_max)
        alpha = jnp.exp2(m - m_new)
        p = jnp.exp2(st - m_new).astype(BF16)
        vt_aug = jnp.concatenate([vt[e * hd:(e + 1) * hd, :], ones_rows], axis=0)
        acc_ref[u, :, lo:] = alpha * acc_ref[u, :, lo:] + _dot(vt_aug, p)
        m_ref[u, :, lo:] = m_new

    ones_rows = jnp.ones((FOX_DEN_ROWS, tk), BF16)
    causal = lax.broadcasted_iota(jnp.int32, (tk, tk), 0) <= lax.broadcasted_iota(jnp.int32, (tk, tk), 1)
    m_ref[...] = jnp.full(m_ref.shape, MASKED, F32)
    acc_ref[...] = jnp.zeros(acc_ref.shape, F32)
    k_first = key_block(0)
    for u in range(len(units)):
        issue_scores(0, u, k_first)

    per_trip = FOX_BLOCKS_PER_TRIP
    assert per_trip % 2 == 0 and (tq // tk) % per_trip == 0

    def block_group(jg, carry):
        for step in range(per_trip):
            j = per_trip * jg + step
            k_next = key_block((j + 1) * tk)
            vt = value_block(j * tk)
            issue_scores(1 - step % 2, 0, k_next)
            for u in range(len(units)):
                if u + 1 < len(units):
                    issue_scores(1 - step % 2, u + 1, k_next)
                update(step % 2, u, vt, None)
        return carry

    lax.fori_loop(0, blk * (tq // tk // per_trip), block_group, 0)
    for jj in range(tq // tk):
        key_off = jj * tk
        active = [u for u, (e, c0) in enumerate(units) if c0 + cw > key_off]
        nxt_off = key_off + tk
        nxt_active = [u for u, (e, c0) in enumerate(units) if c0 + cw > nxt_off] if nxt_off < tq else []
        if nxt_active:
            k_next = key_block(blk * tq + nxt_off)
        vt = value_block(blk * tq + key_off)
        for u in active:
            if u in nxt_active:
                issue_scores((jj + 1) % 2, u, k_next, nxt_off)
            update(jj % 2, u, vt, key_off)
    per_head = []
    for e in range(2):
        cols = [acc_ref[u, :hd, :] * (1.0 / acc_ref[u, hd:hd + 1, :]) for u, unit in enumerate(units) if unit[0] == e]
        per_head.append(jnp.concatenate(cols, axis=1))
    ot = jnp.concatenate(per_head, axis=0)
    o_ref[...] = ot.T.astype(o_ref.dtype)


def _dilated_kernel(hm_ref, q0_ref, k0_ref, kp0_ref, v0_ref, vp0_ref, q1_ref, k1_ref, kp1_ref, v1_ref, vp1_ref,
                    q2_ref, k2_ref, kp2_ref, v2_ref, vp2_ref, ya_ref, o_scr, l_scr):
    jt = pl.program_id(1)
    c = DIL_BACK
    w = DIL_OUT
    nh = DIL_HEADS
    tile = ya_ref.shape[0]
    qi = lax.broadcasted_iota(jnp.int32, (nh * c, 2 * c), 0) & (c - 1)
    kj = lax.broadcasted_iota(jnp.int32, (nh * c, 2 * c), 1)
    dist = qi + c - kj
    in_band = (dist >= 0) & (dist <= DIL_BACK)
    in_band_first = in_band & (kj >= jnp.where(jt > 0, 0, c))
    lane = lax.broadcasted_iota(jnp.int32, (c, w), 1)
    in_head = [(lane >= hh * HEAD_DIM) & (lane < (hh + 1) * HEAD_DIM) for hh in range(nh)]

    def attend(gi, d, items, mask):
        def scores(item):
            q, kk, _, _ = item
            return _dot_nt(jnp.concatenate([q * hm_ref[hh] for hh in range(nh)], axis=0), kk)

        s_next = scores(items[0])
        for idx, (_, _, vv, tok0) in enumerate(items):
            s = s_next
            s_next = scores(items[idx + 1]) if idx + 1 < len(items) else None
            s = jnp.where(mask, s, MASKED)
            m = jnp.max(s, axis=-1, keepdims=True)
            p = jnp.exp2(s - m)
            den = jnp.sum(p, axis=-1, keepdims=True)
            o4 = _dot(p.astype(BF16), vv) * (1.0 / den)
            lse4 = m + jnp.log2(den)
            o = jnp.zeros((c, w), F32)
            lse = jnp.zeros((c, w), F32)
            for hh in range(nh):
                rows = slice(hh * c, (hh + 1) * c)
                o = jnp.where(in_head[hh], o4[rows], o)
                lse = jnp.where(in_head[hh], lse4[rows], lse)
            for half in range(w // LANES):
                lanes = slice(half * LANES, (half + 1) * LANES)
                o_scr[gi, half, pl.ds(tok0, c, stride=d), :] = o[:, lanes]
                l_scr[gi, half, pl.ds(tok0, c, stride=d), :] = lse[:, lanes]

    groups = ((q0_ref, k0_ref, kp0_ref, v0_ref, vp0_ref), (q1_ref, k1_ref, kp1_ref, v1_ref, vp1_ref),
              (q2_ref, k2_ref, kp2_ref, v2_ref, vp2_ref))
    for gi, (q_ref, k_ref, kp_ref, v_ref, vp_ref) in enumerate(groups):
        d = q_ref.shape[0]
        n_sb = q_ref.shape[1] // c
        res_chunk = min(d, DIL_CHAINS)
        assert d % res_chunk == 0

        def first_item(r, q_ref=q_ref, k_ref=k_ref, kp_ref=kp_ref, v_ref=v_ref, vp_ref=vp_ref):
            kk = jnp.concatenate([kp_ref[r], k_ref[r, 0:c]], axis=0)
            vv = jnp.concatenate([vp_ref[r], v_ref[r, 0:c]], axis=0)
            return q_ref[r, 0:c], kk, vv, r

        def later_item(r, sb, d=d, q_ref=q_ref, k_ref=k_ref, v_ref=v_ref):
            start = pl.multiple_of(sb * c, c)
            window = pl.ds(start - c, 2 * c)
            return q_ref[r, pl.ds(start, c)], k_ref[r, window], v_ref[r, window], sb * (c * d) + r

        def first_chunk(it, carry, gi=gi, d=d, res_chunk=res_chunk, first_item=first_item):
            attend(gi, d, [first_item(it * res_chunk + i) for i in range(res_chunk)], in_band_first)
            return carry

        if d == res_chunk:
            first_chunk(0, 0)
        else:
            lax.fori_loop(0, d // res_chunk, first_chunk, 0)
        if n_sb > 1:
            sb_chunk = max(1, DIL_LATER_ITEMS // res_chunk)
            sb_chunk = max(k for k in range(1, sb_chunk + 1) if (n_sb - 1) % k == 0)
            assert d == res_chunk

            def later_chunk(it, carry, gi=gi, d=d, sb_chunk=sb_chunk, later_item=later_item):
                attend(gi, d, [later_item(r, 1 + it * sb_chunk + i) for i in range(sb_chunk) for r in range(d)],
                       in_band)
                return carry

            lax.fori_loop(0, (n_sb - 1) // sb_chunk, later_chunk, 0)

    def merge(ch, carry):
        rows = pl.ds(pl.multiple_of(ch * DIL_MERGE_ROWS, DIL_MERGE_ROWS), DIL_MERGE_ROWS)
        for half in range(w // LANES):
            l = [l_scr[g, half, rows, :] for g in range(N_DIL_GROUPS)]
            mx = jnp.maximum(jnp.maximum(l[0], l[1]), l[2])
            e = [jnp.exp2(lg - mx) for lg in l]
            num = e[0] * o_scr[0, half, rows, :] + e[1] * o_scr[1, half, rows, :] + e[2] * o_scr[2, half, rows, :]
            ya_ref[rows, half * LANES:(half + 1) * LANES] = (num / (e[0] + e[1] + e[2])).astype(ya_ref.dtype)
        return carry

    lax.fori_loop(0, tile // DIL_MERGE_ROWS, merge, 0)


def _memkv_kernel(mem_ref, g_ref, w_ref, gk_ref, km_ref, vm_ref):
    mn = _row_rmsnorm(mem_ref[...], g_ref[...]).astype(BF16)
    kv = _dot(mn, w_ref[...])
    gk = gk_ref[...]
    for hh in range(MEM_HEADS):
        cols = slice(hh * MEM_HEAD_DIM, (hh + 1) * MEM_HEAD_DIM)
        km_ref[:, cols] = _row_rmsnorm(kv[:, cols], gk).astype(km_ref.dtype)
    vm_ref[...] = kv[:, MEM_WIDTH:].astype(vm_ref.dtype)


def _merge_kernel(x_ref, h_ref, ya_ref, yb_ref, qm_ref, km_ref, vm_ref,
                  wg_ref, bg_ref, wa_ref, wb_ref, wm_ref, wo_ref, out_ref, merged_ref):
    ya = ya_ref[...]
    ym = []
    for hh in range(MEM_HEADS):
        cols = slice(hh * MEM_HEAD_DIM, (hh + 1) * MEM_HEAD_DIM)
        s = _dot_nt(qm_ref[:, cols], km_ref[:, cols])
        p = jnp.exp(s - jnp.max(s, axis=-1, keepdims=True))
        den = jnp.sum(p, axis=-1, keepdims=True)
        ym.append((_dot(p.astype(BF16), vm_ref[:, cols]) / den).astype(BF16))
    ym = jnp.concatenate(ym, axis=1)
    yb = yb_ref[...]
    h = h_ref[...]
    cw = 2 * MXU_COLS
    for c in range(0, D_MODEL, cw):
        merged = None
        for k, (y, w_ref) in enumerate(((ya, wa_ref), (yb, wb_ref), (ym, wm_ref))):
            gcol = k * D_MODEL + c
            gate = jax.nn.sigmoid(_dot(h, wg_ref[:, gcol:gcol + cw]) + bg_ref[:, gcol:gcol + cw])
            term = gate * _dot(y, w_ref[:, c:c + cw])
            merged = term if merged is None else merged + term
        merged_ref[:, c:c + cw] = merged.astype(BF16)
    out_ref[...] = x_ref[...] + _dot(merged_ref[...], wo_ref[...])


def _mlp_kernel(x_ref, g_ref, wu_ref, wd_ref, out_ref, h2_ref):
    x = x_ref[...]
    h2_ref[...] = _row_rmsnorm(x, g_ref[...]).astype(BF16)
    cw = D_MODEL
    acc = x
    for c in range(0, D_FF, cw):
        u = jnp.maximum(_dot(h2_ref[...], wu_ref[:, c:c + cw]), 0.0)
        acc = acc + _dot((u * u).astype(BF16), wd_ref[c:c + cw, :])
    out_ref[...] = acc


def _full(shape):
    return pl.BlockSpec(shape, lambda *_: (0,) * len(shape), pipeline_mode=pl.Buffered(1))


def _params(*sem):
    return pltpu.CompilerParams(dimension_semantics=sem)


def _blockdiag(width, head_dim):
    r = jnp.arange(width) // head_dim
    return (r[:, None] == r[None, :]).astype(BF16)


def _decay_placement(n_hp):
    pk = np.zeros((LANES, n_hp * LANES), np.float32)
    onek = np.zeros((1, n_hp * LANES), np.float32)
    pq = np.zeros((n_hp * 2 * DECAY_ROWS, LANES), np.float32)
    oneq = np.zeros((n_hp * 2 * DECAY_ROWS, CUMSUM_BLOCK), np.float32)
    for hp in range(n_hp):
        for e in range(2):
            head = 2 * hp + e
            kcol = hp * LANES + DECAY_ROWS * e
            qrow = (2 * hp + e) * DECAY_ROWS
            for t in range(3):
                onek[0, kcol + t] = 1.0
                pk[t * FOX_HEADS + head, kcol + 3 + t] = -1.0
                pq[qrow + t, t * FOX_HEADS + head] = 1.0
                oneq[qrow + 3 + t, :] = 1.0
    return jnp.asarray(pk, BF16), jnp.asarray(onek), jnp.asarray(pq, BF16), jnp.asarray(oneq)


def _rope_tables(seq, width):
    half = ROT_DIM // 2
    inv_freq = ROPE_THETA ** (-jnp.arange(0, ROT_DIM, 2, dtype=F32) / ROT_DIM)
    ang = jnp.arange(seq, dtype=F32)[:, None] * inv_freq[None, :]
    cos, sin = jnp.cos(ang), jnp.sin(ang)
    ones = jnp.ones((seq, HEAD_DIM - ROT_DIM), F32)
    zeros = jnp.zeros((seq, HEAD_DIM - ROT_DIM), F32)
    zhalf = jnp.zeros((seq, half), F32)
    rc = jnp.concatenate([cos, cos, ones], axis=1)
    rs1 = jnp.concatenate([-sin, zhalf, zeros], axis=1)
    rs2 = jnp.concatenate([zhalf, sin, zeros], axis=1)
    reps = width // HEAD_DIM
    return tuple(jnp.tile(t, (1, reps)) for t in (rc, rs1, rs2))


def kernel(x, mem, g_mix, w_in, b_f, g_qA, g_kA, g_qB, g_kB, g_mem, w_mem_kv, g_qM, g_kM, w_gate, b_gate,
           w_br_a, w_br_b, w_br_m, w_out, g_mlp, w_up, w_down):
    B, T, D = x.shape
    assert D == D_MODEL and w_in.shape[0] == 1, "single-layer kernel"
    N = B * T
    mem_len = mem.shape[1]
    tm = ROW_TILE
    assert T % tm == 0 and T % FOX_Q_TILE == 0 and FOX_Q_TILE % FOX_K_TILE == 0
    x2d = x.reshape(N, D)

    wi = w_in[0]
    offs = [0]
    for wdt in (DIL_WIDTH, DIL_WIDTH, DIL_WIDTH, FOX_WIDTH, FOX_WIDTH, FOX_WIDTH, FOX_HEADS, MEM_WIDTH):
        offs.append(offs[-1] + wdt)
    w_fl = jnp.pad(wi[:, offs[6]:offs[7]], ((0, 0), (0, FL_PAD - FOX_HEADS)))
    w_all = jnp.concatenate([wi[:, :offs[6]], wi[:, offs[7]:], w_fl], axis=1).astype(BF16)
    assert offs[6] == INPROJ_COL["qm"] and w_all.shape[1] == INPROJ_COL["fl"] + FL_PAD
    n_cols = w_all.shape[1]
    bf_pad = jnp.pad(b_f[0], (0, FL_PAD - FOX_HEADS)).reshape(1, FL_PAD)
    cw = MXU_COLS
    q_scale = HEAD_DIM ** -0.5
    gqa = (jnp.tile(g_qA[0], cw // HEAD_DIM) * (q_scale * LOG2E)).reshape(1, cw)
    gka = jnp.tile(g_kA[0], cw // HEAD_DIM).reshape(1, cw)
    gqbt = jnp.broadcast_to((jnp.tile(g_qB[0], cw // HEAD_DIM) * (q_scale * LOG2E))[:, None], (cw, LANES))
    gkb = jnp.tile(g_kB[0], cw // HEAD_DIM).reshape(1, cw)
    gqm = (jnp.tile(g_qM[0], cw // MEM_HEAD_DIM) * MEM_HEAD_DIM ** -0.5).reshape(1, cw)
    rc, rs1, rs2 = _rope_tables(T, LANES)
    bd64 = _blockdiag(cw, HEAD_DIM)

    row = lambda width: pl.BlockSpec((tm, width), lambda i: (i, 0))
    tm1 = INPROJ_ROW_TILE
    per_seq = T // tm1
    assert T % tm1 == 0
    row1 = lambda width: pl.BlockSpec((tm1, width), lambda i: (i, 0))
    rope_spec = pl.BlockSpec((tm1, LANES), lambda i: (i % per_seq, 0))
    tspec = pl.BlockSpec((None, FOX_WIDTH, tm1), lambda i: (i // per_seq, 0, i % per_seq))
    tshape = jax.ShapeDtypeStruct((B, FOX_WIDTH, T), BF16)
    rshape = lambda width: jax.ShapeDtypeStruct((N, width), BF16)
    dils = [d for _, d in DIL_GROUPS]
    assert all(win // d == DIL_BACK and tm1 % d == 0 and (tm1 // d) % 16 == 0 for win, d in DIL_GROUPS)
    dspecs = [pl.BlockSpec((None, d, tm1 // d, DIL_OUT), lambda i: (i // per_seq, 0, i % per_seq, 0))
              for d in dils]
    dshapes = [jax.ShapeDtypeStruct((B, d, T // d, DIL_OUT), BF16) for d in dils]
    outs = pl.pallas_call(
        _inproj_kernel,
        grid=(N // tm1,),
        in_specs=[row1(D), _full((1, D)), _full((D, n_cols)), _full((cw, cw)),
                  _full((1, cw)), _full((1, cw)), _full((cw, LANES)), _full((1, cw)), _full((1, cw)),
                  _full((1, FL_PAD)), rope_spec, rope_spec, rope_spec],
        out_specs=[row1(D)] + dspecs * 3 + [tspec, row1(FOX_WIDTH), tspec, row1(MEM_WIDTH), row1(FL_PAD)],
        out_shape=[rshape(D)] + dshapes * 3 + [tshape, rshape(FOX_WIDTH), tshape, rshape(MEM_WIDTH),
                                               jax.ShapeDtypeStruct((N, FL_PAD), F32)],
        scratch_shapes=[pltpu.VMEM((cw // LANES, tm1, LANES), F32)],
        compiler_params=_params("parallel"),
        name="inproj",
    )(x2d, g_mix, w_all, bd64, gqa, gka, gqbt, gkb, gqm, bf_pad, rc, rs1, rs2)
    h, qa_g, ka_g, va_g = outs[0], outs[1:4], outs[4:7], outs[7:10]
    qbt, kb, vbt, qm, logf = outs[10:]

    n_hp = FOX_HEADS // 2
    tri = (jnp.arange(CUMSUM_BLOCK)[:, None] >= jnp.arange(CUMSUM_BLOCK)[None, :]).astype(BF16)
    pk, onek, pq, oneq = _decay_placement(n_hp)
    cqt, ck_ext = pl.pallas_call(
        _cumsum_kernel,
        grid=(B,),
        in_specs=[pl.BlockSpec((T, FL_PAD), lambda b: (b, 0)), _full((CUMSUM_BLOCK, CUMSUM_BLOCK)),
                  _full(pk.shape), _full(onek.shape), _full(pq.shape), _full(oneq.shape)],
        out_specs=[pl.BlockSpec((None, n_hp, 2, DECAY_ROWS, T), lambda b: (b, 0, 0, 0, 0)),
                   pl.BlockSpec((None, n_hp, T, LANES), lambda b: (b, 0, 0, 0))],
        out_shape=[jax.ShapeDtypeStruct((B, n_hp, 2, DECAY_ROWS, T), BF16),
                   jax.ShapeDtypeStruct((B, n_hp, T, LANES), BF16)],
        compiler_params=_params("parallel"),
        name="decay_cumsum",
    )(logf, tri, pk, onek, pq, oneq)

    tq = FOX_Q_TILE
    nq = T // tq
    n_units = 2 * tq // FOX_COLS
    yb = pl.pallas_call(
        _fox_kernel,
        grid=(B, n_hp, nq),
        in_specs=[pl.BlockSpec((None, LANES, tq), lambda b, hp, i: (b, hp, i)),
                  pl.BlockSpec((None, None, 2, DECAY_ROWS, tq), lambda b, hp, i: (b, hp, 0, 0, i)),
                  pl.BlockSpec((T, LANES), lambda b, hp, i: (b, hp)),
                  pl.BlockSpec((None, None, T, LANES), lambda b, hp, i: (b, hp, 0, 0)),
                  pl.BlockSpec((None, LANES, T), lambda b, hp, i: (b, hp, 0))],
        out_specs=pl.BlockSpec((tq, LANES), lambda b, hp, i: (b * nq + i, hp)),
        out_shape=jax.ShapeDtypeStruct((N, FOX_WIDTH), BF16),
        scratch_shapes=[pltpu.VMEM((2, MXU_COLS, tq), BF16),
                        pltpu.VMEM((2, n_units, FOX_K_TILE, FOX_COLS), F32),
                        pltpu.VMEM((2, n_units, 1, FOX_COLS), F32),
                        pltpu.VMEM((n_units, 1, FOX_COLS), F32),
                        pltpu.VMEM((n_units, HEAD_DIM + FOX_DEN_ROWS, FOX_COLS), F32)],
        compiler_params=_params("parallel", "parallel", "arbitrary"),
        name="fox_attention",
    )(qbt, cqt, kb, ck_ext, vbt)

    tile = DIL_TILE
    assert T % tile == 0 and all((tile // d) % DIL_BACK == 0 for d in dils)
    lane_head = jnp.arange(DIL_OUT) // HEAD_DIM
    head_mask = jnp.broadcast_to((lane_head[None, :] == jnp.arange(DIL_HEADS)[:, None])[:, None, :],
                                 (DIL_HEADS, DIL_BACK, DIL_OUT)).astype(BF16)
    dil_specs, dil_args = [], []
    for gi, d in enumerate(dils):
        per_tile = tile // d // DIL_BACK
        cur = pl.BlockSpec((None, d, tile // d, DIL_OUT), lambda b, j: (b, 0, j, 0))
        prev = pl.BlockSpec((None, d, DIL_BACK, DIL_OUT),
                            lambda b, j, per_tile=per_tile: (b, 0, jnp.maximum(j * per_tile - 1, 0), 0))
        dil_specs += [cur, cur, prev, cur, prev]
        dil_args += [qa_g[gi], ka_g[gi], ka_g[gi], va_g[gi], va_g[gi]]
    ya = pl.pallas_call(
        _dilated_kernel,
        grid=(B, T // tile),
        in_specs=[_full((DIL_HEADS, DIL_BACK, DIL_OUT))] + dil_specs,
        out_specs=pl.BlockSpec((tile, DIL_OUT), lambda b, j: (b * (T // tile) + j, 0)),
        out_shape=jax.ShapeDtypeStruct((N, DIL_OUT), BF16),
        scratch_shapes=[pltpu.VMEM((N_DIL_GROUPS, DIL_OUT // LANES, tile, LANES), F32)] * 2,
        compiler_params=_params("parallel", "arbitrary"),
        name="dilated_attention",
    )(head_mask, *dil_args)

    gkm = g_kM[0].reshape(1, MEM_HEAD_DIM)
    km, vm = pl.pallas_call(
        _memkv_kernel,
        grid=(B,),
        in_specs=[pl.BlockSpec((None, mem_len, D), lambda b: (b, 0, 0)), _full((1, D)),
                  _full((D, 2 * MEM_WIDTH)), _full((1, MEM_HEAD_DIM))],
        out_specs=[pl.BlockSpec((None, mem_len, MEM_WIDTH), lambda b: (b, 0, 0))] * 2,
        out_shape=[jax.ShapeDtypeStruct((B, mem_len, MEM_WIDTH), BF16)] * 2,
        compiler_params=_params("parallel"),
        name="mem_kv",
    )(mem, g_mem, w_mem_kv[0].astype(BF16), gkm)

    mem_spec = pl.BlockSpec((None, mem_len, MEM_WIDTH), lambda i: (i // (T // tm), 0, 0))
    x_mid = pl.pallas_call(
        _merge_kernel,
        grid=(N // tm,),
        in_specs=[row(D), row(D), row(DIL_OUT), row(FOX_WIDTH), row(MEM_WIDTH), mem_spec, mem_spec,
                  _full((D, 3 * D)), _full((1, 3 * D)), _full((DIL_OUT, D)), _full((FOX_WIDTH, D)),
                  _full((MEM_WIDTH, D)), _full((D, D))],
        out_specs=row(D),
        out_shape=jax.ShapeDtypeStruct((N, D), F32),
        scratch_shapes=[pltpu.VMEM((tm, D), BF16)],
        compiler_params=_params("parallel"),
        name="merge_outproj",
    )(x2d, h, ya, yb, qm, km, vm, w_gate[0].astype(BF16), b_gate, w_br_a[0].astype(BF16),
      w_br_b[0].astype(BF16), w_br_m[0].astype(BF16), w_out[0].astype(BF16))

    out = pl.pallas_call(
        _mlp_kernel,
        grid=(N // tm,),
        in_specs=[row(D), _full((1, D)), _full((D, D_FF)), _full((D_FF, D))],
        out_specs=row(D),
        out_shape=jax.ShapeDtypeStruct((N, D), F32),
        scratch_shapes=[pltpu.VMEM((tm, D), BF16)],
        compiler_params=_params("parallel"),
        name="mlp",
    )(x_mid, g_mlp, w_up[0].astype(BF16), w_down[0].astype(BF16))
    return out.reshape(B, T, D)
```

```python
import functools

import jax
import jax.numpy as jnp
import numpy as np
from jax import lax
from jax.experimental import pallas as pl
from jax.experimental.pallas import tpu as pltpu

D_MODEL = 1024
HEAD_DIM = 64
DIL_GROUPS = ((128, 1), (512, 4), (2048, 16))
N_DIL_GROUPS = 3
DIL_HEADS = 4
DIL_WIDTH = N_DIL_GROUPS * DIL_HEADS * HEAD_DIM
DIL_OUT = DIL_HEADS * HEAD_DIM
DIL_BACK = 128
FOX_HEADS = 8
FOX_WIDTH = FOX_HEADS * HEAD_DIM
MEM_HEADS = 4
MEM_HEAD_DIM = 128
MEM_WIDTH = MEM_HEADS * MEM_HEAD_DIM
ROT_DIM = HEAD_DIM // 4
ROPE_THETA = 500000.0
D_FF = 4 * D_MODEL
EPS = 1e-6

LANES = 128
MXU_COLS = 256
FL_PAD = LANES

ROW_TILE = 1024
FUSED_ROW_TILE = 512
INPROJ_ROW_TILE = 1024
INPROJ_COL = dict(qa=0, ka=DIL_WIDTH, va=2 * DIL_WIDTH, qb=3 * DIL_WIDTH, kb=3 * DIL_WIDTH + FOX_WIDTH,
                  vb=3 * DIL_WIDTH + 2 * FOX_WIDTH, qm=3 * DIL_WIDTH + 3 * FOX_WIDTH,
                  fl=3 * DIL_WIDTH + 3 * FOX_WIDTH + MEM_WIDTH)
INPROJ_ORDER = (("qa", 0), ("ka", 0), ("qa", 1), ("ka", 1), ("qa", 2), ("ka", 2), ("qb", 0), ("kb", 0),
                ("qb", 1), ("kb", 1), ("qm", 0), ("qm", 1), ("va", 2), ("va", 1), ("vb", 0), ("vb", 1), ("va", 0))
FOX_Q_TILE = 2048
FOX_K_TILE = 256
FOX_COLS = 512
FOX_DEN_ROWS = 16
FOX_BLOCKS_PER_TRIP = 4
DECAY_ROWS = 16
CUMSUM_BLOCK = 256
DIL_TILE = 2048
DIL_MERGE_ROWS = 256
DIL_CHAINS = 16
DIL_LATER_ITEMS = 15

LOG2E = 1.4426950408889634
MASKED = -jnp.inf
BF16 = jnp.bfloat16
F32 = jnp.float32
NT_DIMS = (((1,), (1,)), ((), ()))


def _dot(a, b):
    return jnp.dot(a, b, preferred_element_type=F32)


def _dot_nt(a, b):
    return lax.dot_general(a, b, NT_DIMS, preferred_element_type=F32)


def _row_rmsnorm(x, g):
    ms = jnp.mean(x * x, axis=-1, keepdims=True)
    return x * lax.rsqrt(ms + EPS) * g


def _head_rmsnorm(y, blockdiag, gain, head_dim):
    ss = _dot((y * y).astype(BF16), blockdiag)
    return y * lax.rsqrt(ss * (1.0 / head_dim) + EPS) * gain


def _split3(c):
    hi = c.astype(BF16)
    r1 = c - hi.astype(F32)
    mid = r1.astype(BF16)
    lo = (r1 - mid.astype(F32)).astype(BF16)
    return hi, mid, lo


def _inproj_kernel(x_ref, gmix_ref, w_ref, bd64_ref, gqa_ref, gka_ref, gqbt_ref, gkb_ref,
                   gqm_ref, bf_ref, rc_ref, rs1_ref, rs2_ref,
                   h_ref, qa0_ref, qa1_ref, qa2_ref, ka0_ref, ka1_ref, ka2_ref, va0_ref, va1_ref, va2_ref,
                   qbt_ref, kb_ref, vbt_ref, qm_ref, lf_ref, perm_ref):
    h = _row_rmsnorm(x_ref[...], gmix_ref[...]).astype(BF16)
    h_ref[...] = h
    bd64 = bd64_ref[...]
    rc = rc_ref[...]
    rs1 = rs1_ref[...]
    rs2 = rs2_ref[...]
    cw = MXU_COLS
    tm = x_ref.shape[0]

    def proj(col, width=cw):
        return _dot(h, w_ref[:, col:col + width])

    def rope(y):
        down = pltpu.roll(y, cw - ROT_DIM // 2, 1)
        up = pltpu.roll(y, ROT_DIM // 2, 1)
        return jnp.concatenate(
            [y[:, c:c + LANES] * rc + down[:, c:c + LANES] * rs1 + up[:, c:c + LANES] * rs2
             for c in range(0, cw, LANES)], axis=1)

    def store_rows(out_ref, c, y):
        out_ref[:, c:c + cw] = y.astype(out_ref.dtype)

    def store_transposed(out_ref, c, y):
        out_ref[c:c + cw, :] = y.T.astype(out_ref.dtype)

    def store_transposed_normed(out_ref, c, y):
        yt = y.T
        gain_col = gqbt_ref[...]
        parts = []
        for r0 in range(0, cw, HEAD_DIM):
            blk = yt[r0:r0 + HEAD_DIM]
            ss = jnp.sum(blk * blk, axis=0, keepdims=True)
            gain = jnp.concatenate([gain_col[r0:r0 + HEAD_DIM]] * (tm // LANES), axis=1)
            parts.append(blk * lax.rsqrt(ss * (1.0 / HEAD_DIM) + EPS) * gain)
        out_ref[c:c + cw, :] = jnp.concatenate(parts, axis=0).astype(out_ref.dtype)

    def store_by_residue(out_ref, y):
        d = out_ref.shape[0]
        if d == 1:
            out_ref[0] = y.astype(out_ref.dtype)
            return
        for half in range(cw // LANES):
            perm_ref[half] = y[:, half * LANES:(half + 1) * LANES]
        for r in range(d):
            out_ref[r] = jnp.concatenate(
                [perm_ref[half, pl.ds(r, tm // d, stride=d), :] for half in range(cw // LANES)],
                axis=1).astype(out_ref.dtype)

    rows_of = lambda ref: [(functools.partial(store_rows, ref, c)) for c in range(0, ref.shape[1], cw)]
    cols_of = lambda ref: [(functools.partial(store_transposed, ref, c)) for c in range(0, ref.shape[0], cw)]
    residues_of = lambda refs: [functools.partial(store_by_residue, ref) for ref in refs]
    segments = {
        "qa": (residues_of((qa0_ref, qa1_ref, qa2_ref)), gqa_ref, bd64, HEAD_DIM, True),
        "ka": (residues_of((ka0_ref, ka1_ref, ka2_ref)), gka_ref, bd64, HEAD_DIM, True),
        "va": (residues_of((va0_ref, va1_ref, va2_ref)), None, None, None, False),
        "qb": ([functools.partial(store_transposed_normed, qbt_ref, c) for c in range(0, qbt_ref.shape[0], cw)],
               None, None, None, False),
        "kb": (rows_of(kb_ref), gkb_ref, bd64, HEAD_DIM, False),
        "vb": (cols_of(vbt_ref), None, None, None, False),
        "qm": (rows_of(qm_ref), gqm_ref, None, MEM_HEAD_DIM, False)}
    chunks = [(INPROJ_COL[name] + piece * cw, segments[name][0][piece]) + segments[name][1:]
              for name, piece in INPROJ_ORDER]
    z = proj(INPROJ_COL["fl"], FL_PAD) + bf_ref[...]
    y_next = proj(chunks[0][0])
    lf_ref[...] = jnp.minimum(z, 0.0) - jnp.log1p(jnp.exp(-jnp.abs(z)))
    for idx, (_, store, gain_ref, bd, hd, rot) in enumerate(chunks):
        y = y_next
        y_next = proj(chunks[idx + 1][0]) if idx + 1 < len(chunks) else None
        if gain_ref is not None and hd == LANES:
            gain = gain_ref[...]
            y = jnp.concatenate([_row_rmsnorm(y[:, c:c + hd], gain[:, c:c + hd]) for c in range(0, cw, hd)], axis=1)
        elif gain_ref is not None:
            y = _head_rmsnorm(y, bd, gain_ref[...], hd)
        if rot:
            y = rope(y)
        store(y)


def _cumsum_kernel(lf_ref, tri_ref, pk_ref, onek_ref, pq_ref, oneq_ref, cqt_ref, ck_ref):
    tri = tri_ref[...]
    n_blocks = lf_ref.shape[0] // CUMSUM_BLOCK
    n_hp = ck_ref.shape[0]
    lane = lax.broadcasted_iota(jnp.int32, (CUMSUM_BLOCK, LANES), 1)
    carry = jnp.zeros((1, LANES), F32)
    for blk in range(n_blocks):
        rows = slice(blk * CUMSUM_BLOCK, (blk + 1) * CUMSUM_BLOCK)
        local = _dot(tri, jnp.concatenate(_split3(lf_ref[rows, :]), axis=1))
        c = local[:, :LANES] + local[:, LANES:2 * LANES] + local[:, 2 * LANES:] + carry
        carry = c[CUMSUM_BLOCK - 1:CUMSUM_BLOCK, :]
        c = c * LOG2E
        hi, mid, lo = (t.astype(F32) for t in _split3(c))
        pieces = jnp.where(lane < FOX_HEADS, hi,
                           jnp.where(lane < 2 * FOX_HEADS, pltpu.roll(mid, FOX_HEADS, 1),
                                     pltpu.roll(lo, 2 * FOX_HEADS, 1))).astype(BF16)
        ke = _dot(pieces, pk_ref[...]) + onek_ref[...]
        qe = _dot_nt(pq_ref[...], pieces) + oneq_ref[...]
        for hp in range(n_hp):
            ck_ref[hp, rows, :] = ke[:, hp * LANES:(hp + 1) * LANES].astype(BF16)
            for e in range(2):
                r0 = (2 * hp + e) * DECAY_ROWS
                cqt_ref[hp, e, :, rows] = qe[r0:r0 + DECAY_ROWS, :].astype(BF16)


def _fox_kernel(qt_ref, cqt_ref, k_ref, ck_ref, vt_ref, o_ref, qa_ref, st_ref, bm_ref, m_ref, acc_ref):
    blk = pl.program_id(2)
    tq = FOX_Q_TILE
    tk = FOX_K_TILE
    hd = HEAD_DIM
    cw = FOX_COLS
    zq = jnp.zeros((hd, tq), BF16)
    zd = jnp.zeros((DECAY_ROWS, tq), BF16)
    zpad = jnp.zeros((MXU_COLS - 2 * hd - 2 * DECAY_ROWS, tq), BF16)
    for e in range(2):
        parts = [zq, zq, zd, zd, zpad]
        parts[e] = qt_ref[e * hd:(e + 1) * hd, :]
        parts[2 + e] = cqt_ref[e]
        qa_ref[e] = jnp.concatenate(parts, axis=0)
    units = [(e, c0) for e in range(2) for c0 in range(0, tq, cw)]

    def key_block(key_start):
        rows = pl.ds(pl.multiple_of(key_start, tk), tk)
        return jnp.concatenate([k_ref[rows, :], ck_ref[rows, :]], axis=1)

    def value_block(key_start):
        return vt_ref[:, pl.ds(pl.multiple_of(key_start, tk), tk)]

    def first_visible(u, key_off):
        c0 = units[u][1]
        return cw // 2 if key_off is not None and key_off >= c0 + cw // 2 else 0

    def issue_scores(buf, u, k_aug, key_off=None):
        e, c0 = units[u]
        lo = first_visible(u, key_off)
        st = _dot(k_aug, qa_ref[e, :, c0 + lo:c0 + cw])
        st_ref[buf, u, :, lo:] = st
        bm_ref[buf, u, :, lo:] = jnp.max(st, axis=0, keepdims=True)

    def update(buf, u, vt, key_off):
        e, c0 = units[u]
        lo = first_visible(u, key_off)
        st = st_ref[buf, u, :, lo:]
        if key_off is not None and key_off + tk - 1 > c0 + lo:
            assert key_off == c0 + lo
            masked = jnp.where(causal, st[:, :tk], MASKED)
            st = masked if st.shape[1] == tk else jnp.concatenate([masked, st[:, tk:]], axis=1)
            block_max = jnp.max(st, axis=0, keepdims=True)
        else:
            block_max = bm_ref[buf, u, :, lo:]
        m = m_ref[u, :, lo:]
        m_new = jnp.maximum(m, block_max)
        alpha = jnp.exp2(m - m_new)
        p = jnp.exp2(st - m_new).astype(BF16)
        vt_aug = jnp.concatenate([vt[e * hd:(e + 1) * hd, :], ones_rows], axis=0)
        acc_ref[u, :, lo:] = alpha * acc_ref[u, :, lo:] + _dot(vt_aug, p)
        m_ref[u, :, lo:] = m_new

    ones_rows = jnp.ones((FOX_DEN_ROWS, tk), BF16)
    causal = lax.broadcasted_iota(jnp.int32, (tk, tk), 0) <= lax.broadcasted_iota(jnp.int32, (tk, tk), 1)
    m_ref[...] = jnp.full(m_ref.shape, MASKED, F32)
    acc_ref[...] = jnp.zeros(acc_ref.shape, F32)
    k_first = key_block(0)
    for u in range(len(units)):
        issue_scores(0, u, k_first)

    per_trip = FOX_BLOCKS_PER_TRIP
    assert per_trip % 2 == 0 and (tq // tk) % per_trip == 0

    def block_group(jg, carry):
        for step in range(per_trip):
            j = per_trip * jg + step
            k_next = key_block((j + 1) * tk)
            vt = value_block(j * tk)
            issue_scores(1 - step % 2, 0, k_next)
            for u in range(len(units)):
                if u + 1 < len(units):
                    issue_scores(1 - step % 2, u + 1, k_next)
                update(step % 2, u, vt, None)
        return carry

    lax.fori_loop(0, blk * (tq // tk // per_trip), block_group, 0)
    for jj in range(tq // tk):
        key_off = jj * tk
        active = [u for u, (e, c0) in enumerate(units) if c0 + cw > key_off]
        nxt_off = key_off + tk
        nxt_active = [u for u, (e, c0) in enumerate(units) if c0 + cw > nxt_off] if nxt_off < tq else []
        if nxt_active:
            k_next = key_block(blk * tq + nxt_off)
        vt = value_block(blk * tq + key_off)
        for u in active:
            if u in nxt_active:
                issue_scores((jj + 1) % 2, u, k_next, nxt_off)
            update(jj % 2, u, vt, key_off)
    per_head = []
    for e in range(2):
        cols = [acc_ref[u, :hd, :] * (1.0 / acc_ref[u, hd:hd + 1, :]) for u, unit in enumerate(units) if unit[0] == e]
        per_head.append(jnp.concatenate(cols, axis=1))
    ot = jnp.concatenate(per_head, axis=0)
    o_ref[...] = ot.T.astype(o_ref.dtype)


def _dilated_kernel(hm_ref, q0_ref, k0_ref, kp0_ref, v0_ref, vp0_ref, q1_ref, k1_ref, kp1_ref, v1_ref, vp1_ref,
                    q2_ref, k2_ref, kp2_ref, v2_ref, vp2_ref, ya_ref, o_scr, l_scr):
    jt = pl.program_id(1)
    c = DIL_BACK
    w = DIL_OUT
    nh = DIL_HEADS
    tile = ya_ref.shape[0]
    qi = lax.broadcasted_iota(jnp.int32, (nh * c, 2 * c), 0) & (c - 1)
    kj = lax.broadcasted_iota(jnp.int32, (nh * c, 2 * c), 1)
    dist = qi + c - kj
    in_band = (dist >= 0) & (dist <= DIL_BACK)
    in_band_first = in_band & (kj >= jnp.where(jt > 0, 0, c))
    lane = lax.broadcasted_iota(jnp.int32, (c, w), 1)
    in_head = [(lane >= hh * HEAD_DIM) & (lane < (hh + 1) * HEAD_DIM) for hh in range(nh)]

    def attend(gi, d, items, mask):
        def scores(item):
            q, kk, _, _ = item
            return _dot_nt(jnp.concatenate([q * hm_ref[hh] for hh in range(nh)], axis=0), kk)

        s_next = scores(items[0])
        for idx, (_, _, vv, tok0) in enumerate(items):
            s = s_next
            s_next = scores(items[idx + 1]) if idx + 1 < len(items) else None
            s = jnp.where(mask, s, MASKED)
            m = jnp.max(s, axis=-1, keepdims=True)
            p = jnp.exp2(s - m)
            den = jnp.sum(p, axis=-1, keepdims=True)
            o4 = _dot(p.astype(BF16), vv) * (1.0 / den)
            lse4 = m + jnp.log2(den)
            o = jnp.zeros((c, w), F32)
            lse = jnp.zeros((c, w), F32)
            for hh in range(nh):
                rows = slice(hh * c, (hh + 1) * c)
                o = jnp.where(in_head[hh], o4[rows], o)
                lse = jnp.where(in_head[hh], lse4[rows], lse)
            for half in range(w // LANES):
                lanes = slice(half * LANES, (half + 1) * LANES)
                o_scr[gi, half, pl.ds(tok0, c, stride=d), :] = o[:, lanes]
                l_scr[gi, half, pl.ds(tok0, c, stride=d), :] = lse[:, lanes]

    groups = ((q0_ref, k0_ref, kp0_ref, v0_ref, vp0_ref), (q1_ref, k1_ref, kp1_ref, v1_ref, vp1_ref),
              (q2_ref, k2_ref, kp2_ref, v2_ref, vp2_ref))
    for gi, (q_ref, k_ref, kp_ref, v_ref, vp_ref) in enumerate(groups):
        d = q_ref.shape[0]
        n_sb = q_ref.shape[1] // c
        res_chunk = min(d, DIL_CHAINS)
        assert d % res_chunk == 0

        def first_item(r, q_ref=q_ref, k_ref=k_ref, kp_ref=kp_ref, v_ref=v_ref, vp_ref=vp_ref):
            kk = jnp.concatenate([kp_ref[r], k_ref[r, 0:c]], axis=0)
            vv = jnp.concatenate([vp_ref[r], v_ref[r, 0:c]], axis=0)
            return q_ref[r, 0:c], kk, vv, r

        def later_item(r, sb, d=d, q_ref=q_ref, k_ref=k_ref, v_ref=v_ref):
            start = pl.multiple_of(sb * c, c)
            window = pl.ds(start - c, 2 * c)
            return q_ref[r, pl.ds(start, c)], k_ref[r, window], v_ref[r, window], sb * (c * d) + r

        def first_chunk(it, carry, gi=gi, d=d, res_chunk=res_chunk, first_item=first_item):
            attend(gi, d, [first_item(it * res_chunk + i) for i in range(res_chunk)], in_band_first)
            return carry

        if d == res_chunk:
            first_chunk(0, 0)
        else:
            lax.fori_loop(0, d // res_chunk, first_chunk, 0)
        if n_sb > 1:
            sb_chunk = max(1, DIL_LATER_ITEMS // res_chunk)
            sb_chunk = max(k for k in range(1, sb_chunk + 1) if (n_sb - 1) % k == 0)
            assert d == res_chunk

            def later_chunk(it, carry, gi=gi, d=d, sb_chunk=sb_chunk, later_item=later_item):
                attend(gi, d, [later_item(r, 1 + it * sb_chunk + i) for i in range(sb_chunk) for r in range(d)],
                       in_band)
                return carry

            lax.fori_loop(0, (n_sb - 1) // sb_chunk, later_chunk, 0)

    def merge(ch, carry):
        rows = pl.ds(pl.multiple_of(ch * DIL_MERGE_ROWS, DIL_MERGE_ROWS), DIL_MERGE_ROWS)
        for half in range(w // LANES):
            l = [l_scr[g, half, rows, :] for g in range(N_DIL_GROUPS)]
            mx = jnp.maximum(jnp.maximum(l[0], l[1]), l[2])
            e = [jnp.exp2(lg - mx) for lg in l]
            num = e[0] * o_scr[0, half, rows, :] + e[1] * o_scr[1, half, rows, :] + e[2] * o_scr[2, half, rows, :]
            ya_ref[rows, half * LANES:(half + 1) * LANES] = (num / (e[0] + e[1] + e[2])).astype(ya_ref.dtype)
        return carry

    lax.fori_loop(0, tile // DIL_MERGE_ROWS, merge, 0)


def _memkv_kernel(mem_ref, g_ref, w_ref, gk_ref, km_ref, vm_ref):
    mn = _row_rmsnorm(mem_ref[...], g_ref[...]).astype(BF16)
    kv = _dot(mn, w_ref[...])
    gk = gk_ref[...]
    for hh in range(MEM_HEADS):
        cols = slice(hh * MEM_HEAD_DIM, (hh + 1) * MEM_HEAD_DIM)
        km_ref[:, cols] = _row_rmsnorm(kv[:, cols], gk).astype(km_ref.dtype)
    vm_ref[...] = kv[:, MEM_WIDTH:].astype(vm_ref.dtype)


def _merge_kernel(x_ref, h_ref, ya_ref, yb_ref, qm_ref, km_ref, vm_ref,
                  wg_ref, bg_ref, wa_ref, wb_ref, wm_ref, wo_ref, out_ref, merged_ref):
    ya = ya_ref[...]
    ym = []
    for hh in range(MEM_HEADS):
        cols = slice(hh * MEM_HEAD_DIM, (hh + 1) * MEM_HEAD_DIM)
        s = _dot_nt(qm_ref[:, cols], km_ref[:, cols])
        p = jnp.exp(s - jnp.max(s, axis=-1, keepdims=True))
        den = jnp.sum(p, axis=-1, keepdims=True)
        ym.append((_dot(p.astype(BF16), vm_ref[:, cols]) / den).astype(BF16))
    ym = jnp.concatenate(ym, axis=1)
    yb = yb_ref[...]
    h = h_ref[...]
    cw = 2 * MXU_COLS
    for c in range(0, D_MODEL, cw):
        merged = None
        for k, (y, w_ref) in enumerate(((ya, wa_ref), (yb, wb_ref), (ym, wm_ref))):
            gcol = k * D_MODEL + c
            gate = jax.nn.sigmoid(_dot(h, wg_ref[:, gcol:gcol + cw]) + bg_ref[:, gcol:gcol + cw])
            term = gate * _dot(y, w_ref[:, c:c + cw])
            merged = term if merged is None else merged + term
        merged_ref[:, c:c + cw] = merged.astype(BF16)
    out_ref[...] = x_ref[...] + _dot(merged_ref[...], wo_ref[...])


def _mlp_kernel(x_ref, g_ref, wu_ref, wd_ref, out_ref, h2_ref):
    x = x_ref[...]
    h2_ref[...] = _row_rmsnorm(x, g_ref[...]).astype(BF16)
    cw = D_MODEL
    acc = x
    for c in range(0, D_FF, cw):
        u = jnp.maximum(_dot(h2_ref[...], wu_ref[:, c:c + cw]), 0.0)
        acc = acc + _dot((u * u).astype(BF16), wd_ref[c:c + cw, :])
    out_ref[...] = acc


def _merge_mlp_kernel(x_ref, h_ref, ya_ref, yb_ref, qm_ref, km_ref, vm_ref, wg_ref, bg_ref, wa_ref, wb_ref, wm_ref,
                      wo_ref, g_ref, wu_ref, wd_ref, out_ref, merged_ref, h2_ref):
    _merge_kernel(x_ref, h_ref, ya_ref, yb_ref, qm_ref, km_ref, vm_ref, wg_ref, bg_ref, wa_ref, wb_ref, wm_ref,
                  wo_ref, out_ref, merged_ref)
    _mlp_kernel(out_ref, g_ref, wu_ref, wd_ref, out_ref, h2_ref)


def _full(shape):
    return pl.BlockSpec(shape, lambda *_: (0,) * len(shape), pipeline_mode=pl.Buffered(1))


def _params(*sem):
    return pltpu.CompilerParams(dimension_semantics=sem)


def _blockdiag(width, head_dim):
    r = jnp.arange(width) // head_dim
    return (r[:, None] == r[None, :]).astype(BF16)


def _decay_placement(n_hp):
    pk = np.zeros((LANES, n_hp * LANES), np.float32)
    onek = np.zeros((1, n_hp * LANES), np.float32)
    pq = np.zeros((n_hp * 2 * DECAY_ROWS, LANES), np.float32)
    oneq = np.zeros((n_hp * 2 * DECAY_ROWS, CUMSUM_BLOCK), np.float32)
    for hp in range(n_hp):
        for e in range(2):
            head = 2 * hp + e
            kcol = hp * LANES + DECAY_ROWS * e
            qrow = (2 * hp + e) * DECAY_ROWS
            for t in range(3):
                onek[0, kcol + t] = 1.0
                pk[t * FOX_HEADS + head, kcol + 3 + t] = -1.0
                pq[qrow + t, t * FOX_HEADS + head] = 1.0
                oneq[qrow + 3 + t, :] = 1.0
    return jnp.asarray(pk, BF16), jnp.asarray(onek), jnp.asarray(pq, BF16), jnp.asarray(oneq)


def _rope_tables(seq, width):
    half = ROT_DIM // 2
    inv_freq = ROPE_THETA ** (-jnp.arange(0, ROT_DIM, 2, dtype=F32) / ROT_DIM)
    ang = jnp.arange(seq, dtype=F32)[:, None] * inv_freq[None, :]
    cos, sin = jnp.cos(ang), jnp.sin(ang)
    ones = jnp.ones((seq, HEAD_DIM - ROT_DIM), F32)
    zeros = jnp.zeros((seq, HEAD_DIM - ROT_DIM), F32)
    zhalf = jnp.zeros((seq, half), F32)
    rc = jnp.concatenate([cos, cos, ones], axis=1)
    rs1 = jnp.concatenate([-sin, zhalf, zeros], axis=1)
    rs2 = jnp.concatenate([zhalf, sin, zeros], axis=1)
    reps = width // HEAD_DIM
    return tuple(jnp.tile(t, (1, reps)) for t in (rc, rs1, rs2))


def kernel(x, mem, g_mix, w_in, b_f, g_qA, g_kA, g_qB, g_kB, g_mem, w_mem_kv, g_qM, g_kM, w_gate, b_gate,
           w_br_a, w_br_b, w_br_m, w_out, g_mlp, w_up, w_down):
    B, T, D = x.shape
    assert D == D_MODEL and w_in.shape[0] == 1, "single-layer kernel"
    N = B * T
    mem_len = mem.shape[1]
    tm = ROW_TILE
    assert T % tm == 0 and T % FOX_Q_TILE == 0 and FOX_Q_TILE % FOX_K_TILE == 0
    x2d = x.reshape(N, D)

    wi = w_in[0]
    offs = [0]
    for wdt in (DIL_WIDTH, DIL_WIDTH, DIL_WIDTH, FOX_WIDTH, FOX_WIDTH, FOX_WIDTH, FOX_HEADS, MEM_WIDTH):
        offs.append(offs[-1] + wdt)
    w_fl = jnp.pad(wi[:, offs[6]:offs[7]], ((0, 0), (0, FL_PAD - FOX_HEADS)))
    w_all = jnp.concatenate([wi[:, :offs[6]], wi[:, offs[7]:], w_fl], axis=1).astype(BF16)
    assert offs[6] == INPROJ_COL["qm"] and w_all.shape[1] == INPROJ_COL["fl"] + FL_PAD
    n_cols = w_all.shape[1]
    bf_pad = jnp.pad(b_f[0], (0, FL_PAD - FOX_HEADS)).reshape(1, FL_PAD)
    cw = MXU_COLS
    q_scale = HEAD_DIM ** -0.5
    gqa = (jnp.tile(g_qA[0], cw // HEAD_DIM) * (q_scale * LOG2E)).reshape(1, cw)
    gka = jnp.tile(g_kA[0], cw // HEAD_DIM).reshape(1, cw)
    gqbt = jnp.broadcast_to((jnp.tile(g_qB[0], cw // HEAD_DIM) * (q_scale * LOG2E))[:, None], (cw, LANES))
    gkb = jnp.tile(g_kB[0], cw // HEAD_DIM).reshape(1, cw)
    gqm = (jnp.tile(g_qM[0], cw // MEM_HEAD_DIM) * MEM_HEAD_DIM ** -0.5).reshape(1, cw)
    rc, rs1, rs2 = _rope_tables(T, LANES)
    bd64 = _blockdiag(cw, HEAD_DIM)

    row = lambda width: pl.BlockSpec((tm, width), lambda i: (i, 0))
    tm1 = INPROJ_ROW_TILE
    per_seq = T // tm1
    assert T % tm1 == 0
    row1 = lambda width: pl.BlockSpec((tm1, width), lambda i: (i, 0))
    rope_spec = pl.BlockSpec((tm1, LANES), lambda i: (i % per_seq, 0))
    tspec = pl.BlockSpec((None, FOX_WIDTH, tm1), lambda i: (i // per_seq, 0, i % per_seq))
    tshape = jax.ShapeDtypeStruct((B, FOX_WIDTH, T), BF16)
    rshape = lambda width: jax.ShapeDtypeStruct((N, width), BF16)
    dils = [d for _, d in DIL_GROUPS]
    assert all(win // d == DIL_BACK and tm1 % d == 0 and (tm1 // d) % 16 == 0 for win, d in DIL_GROUPS)
    dspecs = [pl.BlockSpec((None, d, tm1 // d, DIL_OUT), lambda i: (i // per_seq, 0, i % per_seq, 0))
              for d in dils]
    dshapes = [jax.ShapeDtypeStruct((B, d, T // d, DIL_OUT), BF16) for d in dils]
    outs = pl.pallas_call(
        _inproj_kernel,
        grid=(N // tm1,),
        in_specs=[row1(D), _full((1, D)), _full((D, n_cols)), _full((cw, cw)),
                  _full((1, cw)), _full((1, cw)), _full((cw, LANES)), _full((1, cw)), _full((1, cw)),
                  _full((1, FL_PAD)), rope_spec, rope_spec, rope_spec],
        out_specs=[row1(D)] + dspecs * 3 + [tspec, row1(FOX_WIDTH), tspec, row1(MEM_WIDTH), row1(FL_PAD)],
        out_shape=[rshape(D)] + dshapes * 3 + [tshape, rshape(FOX_WIDTH), tshape, rshape(MEM_WIDTH),
                                               jax.ShapeDtypeStruct((N, FL_PAD), F32)],
        scratch_shapes=[pltpu.VMEM((cw // LANES, tm1, LANES), F32)],
        compiler_params=_params("parallel"),
        name="inproj",
    )(x2d, g_mix, w_all, bd64, gqa, gka, gqbt, gkb, gqm, bf_pad, rc, rs1, rs2)
    h, qa_g, ka_g, va_g = outs[0], outs[1:4], outs[4:7], outs[7:10]
    qbt, kb, vbt, qm, logf = outs[10:]

    n_hp = FOX_HEADS // 2
    tri = (jnp.arange(CUMSUM_BLOCK)[:, None] >= jnp.arange(CUMSUM_BLOCK)[None, :]).astype(BF16)
    pk, onek, pq, oneq = _decay_placement(n_hp)
    cqt, ck_ext = pl.pallas_call(
        _cumsum_kernel,
        grid=(B,),
        in_specs=[pl.BlockSpec((T, FL_PAD), lambda b: (b, 0)), _full((CUMSUM_BLOCK, CUMSUM_BLOCK)),
                  _full(pk.shape), _full(onek.shape), _full(pq.shape), _full(oneq.shape)],
        out_specs=[pl.BlockSpec((None, n_hp, 2, DECAY_ROWS, T), lambda b: (b, 0, 0, 0, 0)),
                   pl.BlockSpec((None, n_hp, T, LANES), lambda b: (b, 0, 0, 0))],
        out_shape=[jax.ShapeDtypeStruct((B, n_hp, 2, DECAY_ROWS, T), BF16),
                   jax.ShapeDtypeStruct((B, n_hp, T, LANES), BF16)],
        compiler_params=_params("parallel"),
        name="decay_cumsum",
    )(logf, tri, pk, onek, pq, oneq)

    tq = FOX_Q_TILE
    nq = T // tq
    n_units = 2 * tq // FOX_COLS
    yb = pl.pallas_call(
        _fox_kernel,
        grid=(B, n_hp, nq),
        in_specs=[pl.BlockSpec((None, LANES, tq), lambda b, hp, i: (b, hp, i)),
                  pl.BlockSpec((None, None, 2, DECAY_ROWS, tq), lambda b, hp, i: (b, hp, 0, 0, i)),
                  pl.BlockSpec((T, LANES), lambda b, hp, i: (b, hp)),
                  pl.BlockSpec((None, None, T, LANES), lambda b, hp, i: (b, hp, 0, 0)),
                  pl.BlockSpec((None, LANES, T), lambda b, hp, i: (b, hp, 0))],
        out_specs=pl.BlockSpec((tq, LANES), lambda b, hp, i: (b * nq + i, hp)),
        out_shape=jax.ShapeDtypeStruct((N, FOX_WIDTH), BF16),
        scratch_shapes=[pltpu.VMEM((2, MXU_COLS, tq), BF16),
                        pltpu.VMEM((2, n_units, FOX_K_TILE, FOX_COLS), F32),
                        pltpu.VMEM((2, n_units, 1, FOX_COLS), F32),
                        pltpu.VMEM((n_units, 1, FOX_COLS), F32),
                        pltpu.VMEM((n_units, HEAD_DIM + FOX_DEN_ROWS, FOX_COLS), F32)],
        compiler_params=_params("parallel", "parallel", "arbitrary"),
        name="fox_attention",
    )(qbt, cqt, kb, ck_ext, vbt)

    tile = DIL_TILE
    assert T % tile == 0 and all((tile // d) % DIL_BACK == 0 for d in dils)
    lane_head = jnp.arange(DIL_OUT) // HEAD_DIM
    head_mask = jnp.broadcast_to((lane_head[None, :] == jnp.arange(DIL_HEADS)[:, None])[:, None, :],
                                 (DIL_HEADS, DIL_BACK, DIL_OUT)).astype(BF16)
    dil_specs, dil_args = [], []
    for gi, d in enumerate(dils):
        per_tile = tile // d // DIL_BACK
        cur = pl.BlockSpec((None, d, tile // d, DIL_OUT), lambda b, j: (b, 0, j, 0))
        prev = pl.BlockSpec((None, d, DIL_BACK, DIL_OUT),
                            lambda b, j, per_tile=per_tile: (b, 0, jnp.maximum(j * per_tile - 1, 0), 0))
        dil_specs += [cur, cur, prev, cur, prev]
        dil_args += [qa_g[gi], ka_g[gi], ka_g[gi], va_g[gi], va_g[gi]]
    ya = pl.pallas_call(
        _dilated_kernel,
        grid=(B, T // tile),
        in_specs=[_full((DIL_HEADS, DIL_BACK, DIL_OUT))] + dil_specs,
        out_specs=pl.BlockSpec((tile, DIL_OUT), lambda b, j: (b * (T // tile) + j, 0)),
        out_shape=jax.ShapeDtypeStruct((N, DIL_OUT), BF16),
        scratch_shapes=[pltpu.VMEM((N_DIL_GROUPS, DIL_OUT // LANES, tile, LANES), F32)] * 2,
        compiler_params=_params("parallel", "arbitrary"),
        name="dilated_attention",
    )(head_mask, *dil_args)

    gkm = g_kM[0].reshape(1, MEM_HEAD_DIM)
    km, vm = pl.pallas_call(
        _memkv_kernel,
        grid=(B,),
        in_specs=[pl.BlockSpec((None, mem_len, D), lambda b: (b, 0, 0)), _full((1, D)),
                  _full((D, 2 * MEM_WIDTH)), _full((1, MEM_HEAD_DIM))],
        out_specs=[pl.BlockSpec((None, mem_len, MEM_WIDTH), lambda b: (b, 0, 0))] * 2,
        out_shape=[jax.ShapeDtypeStruct((B, mem_len, MEM_WIDTH), BF16)] * 2,
        compiler_params=_params("parallel"),
        name="mem_kv",
    )(mem, g_mem, w_mem_kv[0].astype(BF16), gkm)

    tm2 = FUSED_ROW_TILE
    assert T % tm2 == 0
    row2 = lambda width: pl.BlockSpec((tm2, width), lambda i: (i, 0))
    mem_spec = pl.BlockSpec((None, mem_len, MEM_WIDTH), lambda i: (i // (T // tm2), 0, 0))
    out = pl.pallas_call(
        _merge_mlp_kernel,
        grid=(N // tm2,),
        in_specs=[row2(D), row2(D), row2(DIL_OUT), row2(FOX_WIDTH), row2(MEM_WIDTH), mem_spec, mem_spec,
                  _full((D, 3 * D)), _full((1, 3 * D)), _full((DIL_OUT, D)), _full((FOX_WIDTH, D)),
                  _full((MEM_WIDTH, D)), _full((D, D)), _full((1, D)), _full((D, D_FF)), _full((D_FF, D))],
        out_specs=row2(D),
        out_shape=jax.ShapeDtypeStruct((N, D), F32),
        scratch_shapes=[pltpu.VMEM((tm2, D), BF16), pltpu.VMEM((tm2, D), BF16)],
        compiler_params=_params("parallel"),
        name="merge_mlp",
    )(x2d, h, ya, yb, qm, km, vm, w_gate[0].astype(BF16), b_gate, w_br_a[0].astype(BF16),
      w_br_b[0].astype(BF16), w_br_m[0].astype(BF16), w_out[0].astype(BF16),
      g_mlp, w_up[0].astype(BF16), w_down[0].astype(BF16))
    return out.reshape(B, T, D)
```
